```python
import math
import jax, jax.numpy as jnp
from jax import lax
import numpy as np

D_MODEL = 1024
BATCH = 8
SEQ = 4096
DEPTH = 4

CHUNK = 64
Q_BLOCK = 128
CONV_WIDTH = 3
A_GROUPS = 8
A_GROUP_DIM = D_MODEL // 16
A_DIM = A_GROUPS * A_GROUP_DIM
SGU_BLOCK = 128
B_GROUPS = 8
B_GROUP_DIM = D_MODEL // 16
B_DIM = B_GROUPS * B_GROUP_DIM
EVEN_IN = 3 * A_DIM + 2 * B_DIM
EVEN_MIX = A_DIM + B_DIM
MLA_HEADS = 8
MLA_NOPE_DIM = 64
MLA_ROPE_DIM = 32
MLA_V_DIM = 64
MLA_Q_RANK = 256
MLA_KV_RANK = 128
ROPE_THETA = 10000.0
FOX_HEADS = 8
FOX_HEAD_DIM = 64
FOX_DIM = FOX_HEADS * FOX_HEAD_DIM
FOX_BIAS_INIT = 3.0
ODD_IN = MLA_Q_RANK + MLA_KV_RANK + MLA_ROPE_DIM + 3 * FOX_DIM + FOX_HEADS
ODD_MIX = MLA_HEADS * MLA_V_DIM + FOX_DIM
N_EXPERTS = 32
TOP_K = 4
D_FF = D_MODEL
SWIGLU_ALPHA = 1.702
SWIGLU_LIMIT = 7.0
DN_ALPHA = (2 * DEPTH) ** 0.25
DN_BETA = (8 * DEPTH) ** -0.25
N_EVEN = (DEPTH + 1) // 2
N_ODD = DEPTH // 2
NORM_EPS = 1e-5
NEG_INF = -1e30

kernel_name = "hybrid_conv_sgu_mla_fox_moe_deepnorm"


def _layernorm(x, g, b):
    xf = x.astype(jnp.float32)
    mu = jnp.mean(xf, axis=-1, keepdims=True)
    var = jnp.mean(jnp.square(xf - mu), axis=-1, keepdims=True)
    return ((xf - mu) * lax.rsqrt(var + NORM_EPS) * g + b).astype(x.dtype)


def _rmsnorm(x, g):
    xf = x.astype(jnp.float32)
    ms = jnp.mean(jnp.square(xf), axis=-1, keepdims=True)
    return (xf * lax.rsqrt(ms + NORM_EPS) * g).astype(x.dtype)


def _rope(x):
    s, d = x.shape[1], x.shape[-1]
    inv_freq = ROPE_THETA ** (-jnp.arange(0, d, 2, dtype=jnp.float32) / d)
    ang = jnp.arange(s, dtype=jnp.float32)[:, None] * inv_freq[None, :]
    cos = jnp.cos(ang)[None, :, None, :]
    sin = jnp.sin(ang)[None, :, None, :]
    xf = x.astype(jnp.float32)
    x1, x2 = xf[..., : d // 2], xf[..., d // 2:]
    return jnp.concatenate([x1 * cos - x2 * sin, x2 * cos + x1 * sin], axis=-1).astype(x.dtype)


def _heads(t, h):
    b, s, _ = t.shape
    return t.reshape(b, s, h, -1).transpose(0, 2, 1, 3)


def _merge(t):
    b, h, s, d = t.shape
    return t.transpose(0, 2, 1, 3).reshape(b, s, h * d)


def _block_attention(q, k, v, scale, per_frame, log_f_cum=None):
    b, h, s, dq = q.shape
    dv = v.shape[-1]
    nb = s // Q_BLOCK
    k_pos = jnp.arange(s)
    q_blocks = jnp.moveaxis(q.reshape(b, h, nb, Q_BLOCK, dq), 2, 0)
    idx = jnp.arange(nb)

    def attend(qb, i, fq):
        q_pos = i * Q_BLOCK + jnp.arange(Q_BLOCK)
        logits = jnp.einsum('bhqd,bhkd->bhqk', qb, k).astype(jnp.float32) * scale
        if per_frame:
            allowed = k_pos[None, :] <= q_pos[:, None]
        else:
            allowed = (k_pos // CHUNK)[None, :] <= (q_pos // CHUNK)[:, None]
        if fq is not None:
            logits = logits + (fq[..., :, None] - log_f_cum[..., None, :])
        logits = jnp.where(allowed, logits, NEG_INF)
        p = jax.nn.softmax(logits, axis=-1).astype(v.dtype)
        return jnp.einsum('bhqk,bhkd->bhqd', p, v)

    if log_f_cum is None:
        out = lax.map(lambda xs: attend(xs[0], xs[1], None), (q_blocks, idx))
    else:
        f_blocks = jnp.moveaxis(log_f_cum.reshape(b, h, nb, Q_BLOCK), 2, 0)
        out = lax.map(lambda xs: attend(xs[0], xs[1], xs[2]), (q_blocks, idx, f_blocks))
    return jnp.moveaxis(out, 0, 2).reshape(b, h, s, dv)


def _even_mixer(x, w_in, conv_w, sgu_ln_g, sgu_ln_b, sgu_w, sgu_b, w_out):
    b, s, _ = x.shape
    proj = x @ w_in
    a_c, a_b, a_v, b_uv = jnp.split(proj, [A_DIM, 2 * A_DIM, 3 * A_DIM], axis=-1)
    h = a_c * a_v
    hp = jnp.pad(h, ((0, 0), (CONV_WIDTH - 1, 0), (0, 0)))
    conv = hp[:, 0:s] * conv_w[0] + hp[:, 1:s + 1] * conv_w[1] + hp[:, 2:s + 2] * conv_w[2]
    y_a = a_b * conv
    z = jax.nn.gelu(b_uv)
    u, v = jnp.split(z, 2, axis=-1)
    v = v.reshape(b, s // SGU_BLOCK, SGU_BLOCK, B_GROUPS, B_GROUP_DIM)
    v = _layernorm(v, sgu_ln_g.reshape(B_GROUPS, B_GROUP_DIM), sgu_ln_b.reshape(B_GROUPS, B_GROUP_DIM))
    pos = jnp.arange(SGU_BLOCK)
    mask = (pos[None, :] // CHUNK) <= (pos[:, None] // CHUNK)
    w_s = jnp.where(mask, sgu_w, 0.0)
    v = jnp.einsum('gij,bnjgc->bnigc', w_s, v) + sgu_b.T[:, :, None]
    y_b = u * v.reshape(b, s, B_DIM)
    return jnp.concatenate([y_a, y_b], axis=-1) @ w_out


def _odd_mixer(x, w_in, q_norm_g, kv_norm_g, w_uq, w_ukv, f_bias, w_out):
    b, s, _ = x.shape
    proj = x @ w_in
    splits = np.cumsum([MLA_Q_RANK, MLA_KV_RANK, MLA_ROPE_DIM, FOX_DIM, FOX_DIM, FOX_DIM]).tolist()
    c_q, c_kv, k_r, f_q, f_k, f_v, f_z = jnp.split(proj, splits, axis=-1)
    q = (_rmsnorm(c_q, q_norm_g) @ w_uq).reshape(b, s, MLA_HEADS, MLA_NOPE_DIM + MLA_ROPE_DIM)
    q = jnp.concatenate([q[..., :MLA_NOPE_DIM], _rope(q[..., MLA_NOPE_DIM:])], axis=-1)
    kv = (_rmsnorm(c_kv, kv_norm_g) @ w_ukv).reshape(b, s, MLA_HEADS, MLA_NOPE_DIM + MLA_V_DIM)
    k_nope, v_c = kv[..., :MLA_NOPE_DIM], kv[..., MLA_NOPE_DIM:]
    k_rope = jnp.broadcast_to(_rope(k_r[:, :, None, :]), (b, s, MLA_HEADS, MLA_ROPE_DIM))
    k = jnp.concatenate([k_nope, k_rope], axis=-1)
    y_c = _block_attention(q.transpose(0, 2, 1, 3), k.transpose(0, 2, 1, 3), v_c.transpose(0, 2, 1, 3),
                           1.0 / math.sqrt(MLA_NOPE_DIM + MLA_ROPE_DIM), per_frame=False)
    log_f = jax.nn.log_sigmoid((f_z + f_bias).astype(jnp.float32))
    log_f_cum = jnp.cumsum(log_f, axis=1).transpose(0, 2, 1)
    y_d = _block_attention(_heads(f_q, FOX_HEADS), _heads(f_k, FOX_HEADS), _heads(f_v, FOX_HEADS),
                           1.0 / math.sqrt(FOX_HEAD_DIM), per_frame=True, log_f_cum=log_f_cum)
    return jnp.concatenate([_merge(y_c), _merge(y_d)], axis=-1) @ w_out


def _clamped_swiglu(h):
    glu, lin = h[..., ::2], h[..., 1::2]
    glu = jnp.minimum(glu, SWIGLU_LIMIT)
    lin = jnp.clip(lin, -SWIGLU_LIMIT, SWIGLU_LIMIT)
    return glu * jax.nn.sigmoid(SWIGLU_ALPHA * glu) * (lin + 1.0)


def _moe(x, w_router, b_router, w1, b1, w2, b2):
    b, s, d = x.shape
    xt = x.reshape(b * s, d)
    logits = (xt @ w_router + b_router).astype(jnp.float32)
    top_val, top_idx = lax.top_k(logits, TOP_K)
    top_w = jax.nn.softmax(top_val, axis=-1)
    gates = jnp.einsum('nk,nke->ne', top_w, jax.nn.one_hot(top_idx, N_EXPERTS, dtype=jnp.float32)).astype(x.dtype)
    out = jnp.zeros_like(xt)
    for e in range(N_EXPERTS):
        h = _clamped_swiglu(xt @ w1[e] + b1[e])
        out = out + gates[:, e:e + 1] * (h @ w2[e] + b2[e])
    return out.reshape(b, s, d)


def setup_inputs(seed: int = 0) -> dict:
    key = jax.random.key(seed)
    ks = jax.random.split(key, 32)
    nrm = jax.random.normal
    f32 = jnp.float32
    return {
        "x": nrm(ks[0], (BATCH, SEQ, D_MODEL), f32),
        "ev_w_in": nrm(ks[1], (N_EVEN, D_MODEL, EVEN_IN), f32) * D_MODEL ** -0.5,
        "ev_conv_w": nrm(ks[2], (N_EVEN, CONV_WIDTH, A_DIM), f32) * CONV_WIDTH ** -0.5,
        "ev_sgu_ln_g": 1.0 + 0.1 * nrm(ks[3], (N_EVEN, B_DIM), f32),
        "ev_sgu_ln_b": 0.1 * nrm(ks[4], (N_EVEN, B_DIM), f32),
        "ev_sgu_w": nrm(ks[5], (N_EVEN, B_GROUPS, SGU_BLOCK, SGU_BLOCK), f32) * SGU_BLOCK ** -0.5,
        "ev_sgu_b": 1.0 + 0.1 * nrm(ks[6], (N_EVEN, B_GROUPS, SGU_BLOCK), f32),
        "ev_w_out": nrm(ks[7], (N_EVEN, EVEN_MIX, D_MODEL), f32) * EVEN_MIX ** -0.5 * DN_BETA,
        "od_w_in": nrm(ks[8], (N_ODD, D_MODEL, ODD_IN), f32) * D_MODEL ** -0.5,
        "od_q_norm_g": 1.0 + 0.1 * nrm(ks[9], (N_ODD, MLA_Q_RANK), f32),
        "od_kv_norm_g": 1.0 + 0.1 * nrm(ks[10], (N_ODD, MLA_KV_RANK), f32),
        "od_w_uq": nrm(ks[11], (N_ODD, MLA_Q_RANK, MLA_HEADS * (MLA_NOPE_DIM + MLA_ROPE_DIM)), f32) * MLA_Q_RANK ** -0.5,
        "od_w_ukv": nrm(ks[12], (N_ODD, MLA_KV_RANK, MLA_HEADS * (MLA_NOPE_DIM + MLA_V_DIM)), f32) * MLA_KV_RANK ** -0.5,
        "od_f_bias": FOX_BIAS_INIT + 0.5 * nrm(ks[13], (N_ODD, FOX_HEADS), f32),
        "od_w_out": nrm(ks[14], (N_ODD, ODD_MIX, D_MODEL), f32) * ODD_MIX ** -0.5 * DN_BETA,
        "ln_mix_g": 1.0 + 0.1 * nrm(ks[15], (DEPTH, D_MODEL), f32),
        "ln_mix_b": 0.1 * nrm(ks[16], (DEPTH, D_MODEL), f32),
        "ln_ffn_g": 1.0 + 0.1 * nrm(ks[17], (DEPTH, D_MODEL), f32),
        "ln_ffn_b": 0.1 * nrm(ks[18], (DEPTH, D_MODEL), f32),
        "moe_w_router": nrm(ks[19], (DEPTH, D_MODEL, N_EXPERTS), f32) * D_MODEL ** -0.5,
        "moe_b_router": 0.01 * nrm(ks[20], (DEPTH, N_EXPERTS), f32),
        "moe_w1": nrm(ks[21], (DEPTH, N_EXPERTS, D_MODEL, 2 * D_FF), f32) * D_MODEL ** -0.5,
        "moe_b1": 0.01 * nrm(ks[22], (DEPTH, N_EXPERTS, 2 * D_FF), f32),
        "moe_w2": nrm(ks[23], (DEPTH, N_EXPERTS, D_FF, D_MODEL), f32) * D_FF ** -0.5 * DN_BETA,
        "moe_b2": 0.01 * nrm(ks[24], (DEPTH, N_EXPERTS, D_MODEL), f32),
    }


def reference(x, ev_w_in, ev_conv_w, ev_sgu_ln_g, ev_sgu_ln_b, ev_sgu_w, ev_sgu_b, ev_w_out,
              od_w_in, od_q_norm_g, od_kv_norm_g, od_w_uq, od_w_ukv, od_f_bias, od_w_out,
              ln_mix_g, ln_mix_b, ln_ffn_g, ln_ffn_b,
              moe_w_router, moe_b_router, moe_w1, moe_b1, moe_w2, moe_b2):
    for layer in range(DEPTH):
        i = layer // 2
        if layer % 2 == 0:
            mix = _even_mixer(x, ev_w_in[i], ev_conv_w[i], ev_sgu_ln_g[i], ev_sgu_ln_b[i],
                              ev_sgu_w[i], ev_sgu_b[i], ev_w_out[i])
        else:
            mix = _odd_mixer(x, od_w_in[i], od_q_norm_g[i], od_kv_norm_g[i], od_w_uq[i],
                             od_w_ukv[i], od_f_bias[i], od_w_out[i])
        x = _layernorm(DN_ALPHA * x + mix, ln_mix_g[layer], ln_mix_b[layer])
        ffn = _moe(x, moe_w_router[layer], moe_b_router[layer], moe_w1[layer], moe_b1[layer],
                   moe_w2[layer], moe_b2[layer])
        x = _layernorm(DN_ALPHA * x + ffn, ln_ffn_g[layer], ln_ffn_b[layer])
    return x
```

```python
import functools
import math

import numpy as np
import jax
import jax.numpy as jnp
from jax import lax
from jax.experimental import pallas as pl
from jax.experimental.pallas import tpu as pltpu

F32 = jnp.float32
BF16 = jnp.bfloat16
I32 = jnp.int32

CHUNK = 64
CONV_WIDTH = 3
SGU_BLOCK = 128
GROUP_DIM = 64
MLA_HEADS = 8
MLA_NOPE = 64
MLA_ROPE = 32
MLA_V = 64
ROPE_THETA = 10000.0
FOX_HEADS = 8
FOX_HEAD_DIM = 64
N_EXPERTS = 32
TOP_K = 4
SWIGLU_ALPHA = 1.702
SWIGLU_LIMIT = 7.0
NORM_EPS = 1e-5
NEG_INF = -1e30

LANES = 128
SUBLANES = 8
VMEM_LIMIT = 56 * 1024 * 1024

ROW_TILE = 512
ATTN_TILE = 256
EXPERT_TILE = 256
MOVE_TILE = 256


def _cparams(n_axes=1):
    return pltpu.CompilerParams(dimension_semantics=("arbitrary",) * n_axes,
                                vmem_limit_bytes=VMEM_LIMIT)


def _dot(a, b):
    return jnp.dot(a, b, preferred_element_type=F32)


def _split2(v):
    hi = v.astype(BF16)
    lo = (v - hi.astype(F32)).astype(BF16)
    return hi, lo


def _split3(v):
    hi = v.astype(BF16)
    r1 = v - hi.astype(F32)
    mid = r1.astype(BF16)
    lo = (r1 - mid.astype(F32)).astype(BF16)
    return hi, mid, lo


def _layernorm(x, g, b):
    mu = jnp.mean(x, axis=-1, keepdims=True)
    xc = x - mu
    var = jnp.mean(xc * xc, axis=-1, keepdims=True)
    return xc * lax.rsqrt(var + NORM_EPS) * g + b


def _gelu_tanh(x):
    c = math.sqrt(2.0 / math.pi)
    return x * (0.5 * (1.0 + jnp.tanh(c * (x + 0.044715 * (x * x * x)))))


def _full(shape):
    nd = len(shape)
    return pl.BlockSpec(shape, lambda *_: (0,) * nd)


def _even_kernel(x_ref, win_ref, convw_ref, gavg_ref, lng_ref, lnb_ref, ws_ref, sb_ref, wout_ref,
                 g_ref, b_ref, xo_ref, xob_ref, hs_ref, mix_ref, *, tm, seq, alpha, a_dim):
    i = pl.program_id(0)
    x = x_ref[...]
    proj = _dot(x.astype(BF16), win_ref[...])
    a_c = proj[:, 0:a_dim]
    a_b = proj[:, a_dim:2 * a_dim]
    a_v = proj[:, 2 * a_dim:3 * a_dim]
    b_u = proj[:, 3 * a_dim:4 * a_dim]
    b_v = proj[:, 4 * a_dim:5 * a_dim]

    h = a_c * a_v

    @pl.when((i * tm) % seq == 0)
    def _():
        hs_ref[0:SUBLANES, :] = jnp.zeros((SUBLANES, a_dim), F32)

    hs_ref[SUBLANES:SUBLANES + tm, :] = h
    conv = (hs_ref[SUBLANES - 2:SUBLANES - 2 + tm, :] * convw_ref[0:1, :]
            + hs_ref[SUBLANES - 1:SUBLANES - 1 + tm, :] * convw_ref[1:2, :]
            + h * convw_ref[2:3, :])
    hs_ref[0:SUBLANES, :] = h[tm - SUBLANES:tm, :]
    mix_ref[:, 0:a_dim] = (a_b * conv).astype(BF16)

    u = _gelu_tanh(b_u)
    v = _gelu_tanh(b_v)
    gavg = gavg_ref[...]
    lane = lax.broadcasted_iota(I32, (SGU_BLOCK, LANES), 1)
    low = lane < GROUP_DIM
    for c in range(a_dim // LANES):
        vc = v[:, c * LANES:(c + 1) * LANES]
        hi, lo = _split2(vc)
        mean = _dot(hi, gavg) + _dot(lo, gavg)
        d = vc - mean
        hi, lo = _split2(d * d)
        var = _dot(hi, gavg) + _dot(lo, gavg)
        vn = d * lax.rsqrt(var + NORM_EPS) * lng_ref[:, c * LANES:(c + 1) * LANES] \
            + lnb_ref[:, c * LANES:(c + 1) * LANES]
        for blk in range(tm // SGU_BLOCK):
            vb = vn[blk * SGU_BLOCK:(blk + 1) * SGU_BLOCK, :]
            rhs = jnp.concatenate([jnp.where(low, vb, 0.0), jnp.where(low, 0.0, vb)], axis=0).astype(BF16)
            sg = _dot(ws_ref[c], rhs) + sb_ref[:, c * LANES:(c + 1) * LANES]
            ub = u[blk * SGU_BLOCK:(blk + 1) * SGU_BLOCK, c * LANES:(c + 1) * LANES]
            mix_ref[blk * SGU_BLOCK:(blk + 1) * SGU_BLOCK, a_dim + c * LANES:a_dim + (c + 1) * LANES] = \
                (ub * sg).astype(BF16)

    out = _dot(mix_ref[...], wout_ref[...])
    xn = _layernorm(alpha * x + out, g_ref[...], b_ref[...])
    xo_ref[...] = xn
    xob_ref[...] = xn.astype(BF16)


def _even_layer(x, w_in, conv_w, sgu_ln_g, sgu_ln_b, sgu_w, sgu_b, w_out, ln_g, ln_b, *, seq, alpha):
    n, d = x.shape
    a_dim = conv_w.shape[1]
    tm = min(ROW_TILE, seq)
    n_groups = sgu_w.shape[0]
    pos = np.arange(SGU_BLOCK)
    mask = (pos[None, :] // CHUNK) <= (pos[:, None] // CHUNK)
    w_s = jnp.where(mask, sgu_w, 0.0)
    ws_pairs = jnp.concatenate([w_s[0::2], w_s[1::2]], axis=2).astype(BF16)
    sb_full = jnp.repeat(sgu_b.T, GROUP_DIM, axis=1)
    convw = jnp.zeros((SUBLANES, a_dim), F32).at[0:CONV_WIDTH].set(conv_w)
    g_idx = np.arange(LANES) // GROUP_DIM
    gavg = jnp.asarray((g_idx[:, None] == g_idx[None, :]).astype(np.float32) / GROUP_DIM, BF16)
    assert n_groups * GROUP_DIM == a_dim and w_in.shape[1] == 5 * a_dim
    kern = functools.partial(_even_kernel, tm=tm, seq=seq, alpha=alpha, a_dim=a_dim)
    return pl.pallas_call(
        kern,
        grid=(n // tm,),
        in_specs=[pl.BlockSpec((tm, d), lambda i: (i, 0)),
                  _full(w_in.shape), _full(convw.shape), _full(gavg.shape),
                  _full((1, a_dim)), _full((1, a_dim)), _full(ws_pairs.shape), _full(sb_full.shape),
                  _full(w_out.shape), _full((1, d)), _full((1, d))],
        out_specs=[pl.BlockSpec((tm, d), lambda i: (i, 0)), pl.BlockSpec((tm, d), lambda i: (i, 0))],
        out_shape=[jax.ShapeDtypeStruct((n, d), F32), jax.ShapeDtypeStruct((n, d), BF16)],
        scratch_shapes=[pltpu.VMEM((tm + SUBLANES, a_dim), F32), pltpu.VMEM((tm, 2 * a_dim), BF16)],
        compiler_params=_cparams(),
        name="even_mixer",
    )(x, w_in.astype(BF16), convw, gavg, sgu_ln_g.reshape(1, a_dim), sgu_ln_b.reshape(1, a_dim),
      ws_pairs, sb_full, w_out.astype(BF16), ln_g.reshape(1, d), ln_b.reshape(1, d))


MISC_KR = 0
MISC_KR_ROT = MLA_ROPE
MISC_FZ = 2 * MLA_ROPE


def _odd_proj_kernel(x_ref, win_ref, qg_ref, kvg_ref, wq_ref, wqr_ref, wk_ref, wkr_ref, wv_ref,
                     fb_ref, ctab_ref, stab_ref, ttab_ref,
                     q_ref, k_ref, v_ref, fq_ref, fk_ref, fv_ref, fcol_ref, frow_ref, carry_ref,
                     *, tm, seq, q_rank, kv_rank, fox_dim):
    i = pl.program_id(0)
    x = x_ref[...]
    proj = _dot(x.astype(BF16), win_ref[...])
    o = 0
    c_q = proj[:, o:o + q_rank]; o += q_rank
    c_kv = proj[:, o:o + kv_rank]; o += kv_rank
    misc = proj[:, o:o + LANES]; o += LANES
    fq_ref[...] = proj[:, o:o + fox_dim].astype(BF16); o += fox_dim
    fk_ref[...] = proj[:, o:o + fox_dim].astype(BF16); o += fox_dim
    fv_ref[...] = proj[:, o:o + fox_dim].astype(BF16)

    cqn = (c_q * lax.rsqrt(jnp.mean(c_q * c_q, axis=-1, keepdims=True) + NORM_EPS) * qg_ref[...]).astype(BF16)
    ckn = (c_kv * lax.rsqrt(jnp.mean(c_kv * c_kv, axis=-1, keepdims=True) + NORM_EPS) * kvg_ref[...]).astype(BF16)

    q1 = _dot(cqn, wq_ref[...])
    q2 = _dot(cqn, wqr_ref[...])
    ctab = ctab_ref[...]
    stab = stab_ref[...]
    for hh in range(MLA_HEADS):
        sl = slice(hh * LANES, (hh + 1) * LANES)
        q_ref[:, sl] = (q1[:, sl] * ctab + q2[:, sl] * stab).astype(BF16)

    km_hi, km_lo = _split2(misc * ttab_ref[...])
    k_ref[...] = (_dot(ckn, wk_ref[...]) + _dot(km_hi, wkr_ref[...]) + _dot(km_lo, wkr_ref[...])).astype(BF16)
    v_ref[...] = _dot(ckn, wv_ref[...]).astype(BF16)

    lane = lax.broadcasted_iota(I32, (tm, LANES), 1)
    z = misc + fb_ref[...]
    lf = -(jnp.maximum(-z, 0.0) + jnp.log1p(jnp.exp(-jnp.abs(z))))
    lf = jnp.where((lane >= MISC_FZ) & (lane < MISC_FZ + FOX_HEADS), lf, 0.0)
    r = lax.broadcasted_iota(I32, (tm, tm), 0)
    c = lax.broadcasted_iota(I32, (tm, tm), 1)
    tri = jnp.where(c <= r, 1.0, 0.0).astype(BF16)
    hi, mid, lo = _split3(lf)
    incl = _dot(tri, hi) + _dot(tri, mid) + _dot(tri, lo)

    @pl.when((i * tm) % seq == 0)
    def _():
        carry_ref[...] = jnp.zeros((SUBLANES, LANES), F32)

    fcum = incl + carry_ref[0:1, :]
    carry_ref[...] = jnp.broadcast_to(fcum[tm - 1:tm, :], (SUBLANES, LANES))
    fcol_ref[...] = fcum
    frow_ref[...] = fcum.T[MISC_FZ:MISC_FZ + FOX_HEADS, :]


def _rope_tables(seq):
    half = MLA_ROPE // 2
    inv_freq = ROPE_THETA ** (-np.arange(0, MLA_ROPE, 2, dtype=np.float64) / MLA_ROPE)
    ang = np.arange(seq, dtype=np.float64)[:, None] * inv_freq[None, :]
    cos = np.concatenate([np.cos(ang), np.cos(ang)], axis=1)
    sin = np.concatenate([np.sin(ang), np.sin(ang)], axis=1)
    assert cos.shape[1] == 2 * half
    ctab = np.zeros((seq, LANES)); ctab[:, :MLA_NOPE] = 1.0; ctab[:, MLA_NOPE:MLA_NOPE + MLA_ROPE] = cos
    stab = np.zeros((seq, LANES)); stab[:, MLA_NOPE:MLA_NOPE + MLA_ROPE] = sin
    ttab = np.zeros((seq, LANES)); ttab[:, MISC_KR:MISC_KR + MLA_ROPE] = cos
    ttab[:, MISC_KR_ROT:MISC_KR_ROT + MLA_ROPE] = sin
    return (jnp.asarray(ctab, F32), jnp.asarray(stab, F32), jnp.asarray(ttab, F32))


def _rot_cols(w):
    half = MLA_ROPE // 2
    return jnp.concatenate([-w[..., half:], w[..., :half]], axis=-1)


def _odd_proj(x, w_in, q_norm_g, kv_norm_g, w_uq, w_ukv, f_bias, *, batch, seq):
    n, d = x.shape
    q_rank, kv_rank = q_norm_g.shape[0], kv_norm_g.shape[0]
    fox_dim = FOX_HEADS * FOX_HEAD_DIM
    tm = min(ROW_TILE, seq)
    o = 0
    w_cq = w_in[:, o:o + q_rank]; o += q_rank
    w_ckv = w_in[:, o:o + kv_rank]; o += kv_rank
    w_kr = w_in[:, o:o + MLA_ROPE]; o += MLA_ROPE
    w_f = w_in[:, o:o + 3 * fox_dim]; o += 3 * fox_dim
    w_fz = w_in[:, o:o + FOX_HEADS]
    w_misc = jnp.zeros((d, LANES), F32)
    w_misc = w_misc.at[:, MISC_KR:MISC_KR + MLA_ROPE].set(w_kr)
    w_misc = w_misc.at[:, MISC_KR_ROT:MISC_KR_ROT + MLA_ROPE].set(_rot_cols(w_kr))
    w_misc = w_misc.at[:, MISC_FZ:MISC_FZ + FOX_HEADS].set(w_fz)
    w_in_p = jnp.concatenate([w_cq, w_ckv, w_misc, w_f], axis=1).astype(BF16)
    dq = MLA_NOPE + MLA_ROPE
    wq3 = w_uq.reshape(q_rank, MLA_HEADS, dq)
    wq = jnp.zeros((q_rank, MLA_HEADS, LANES), F32).at[:, :, :dq].set(wq3)
    wqr = jnp.zeros((q_rank, MLA_HEADS, LANES), F32).at[:, :, MLA_NOPE:dq].set(_rot_cols(wq3[:, :, MLA_NOPE:]))
    wkv3 = w_ukv.reshape(kv_rank, MLA_HEADS, MLA_NOPE + MLA_V)
    wk = jnp.zeros((kv_rank, MLA_HEADS, LANES), F32).at[:, :, :MLA_NOPE].set(wkv3[:, :, :MLA_NOPE])
    wv = wkv3[:, :, MLA_NOPE:].reshape(kv_rank, MLA_HEADS * MLA_V)
    place = np.zeros((LANES, MLA_HEADS, LANES), np.float32)
    for j in range(MLA_ROPE):
        place[MISC_KR + j, :, MLA_NOPE + j] = 1.0
        place[MISC_KR_ROT + j, :, MLA_NOPE + j] = 1.0
    wkr = jnp.asarray(place.reshape(LANES, MLA_HEADS * LANES), BF16)
    fb = jnp.zeros((1, LANES), F32).at[0, MISC_FZ:MISC_FZ + FOX_HEADS].set(f_bias)
    ctab, stab, ttab = _rope_tables(seq)
    hl = MLA_HEADS * LANES
    nseq = seq // tm
    row = lambda i: (i, 0)
    tab = lambda i: (i % nseq, 0)
    kern = functools.partial(_odd_proj_kernel, tm=tm, seq=seq, q_rank=q_rank, kv_rank=kv_rank, fox_dim=fox_dim)
    outs = pl.pallas_call(
        kern,
        grid=(n // tm,),
        in_specs=[pl.BlockSpec((tm, d), row), _full(w_in_p.shape), _full((1, q_rank)), _full((1, kv_rank)),
                  _full((q_rank, hl)), _full((q_rank, hl)), _full((kv_rank, hl)), _full((LANES, hl)),
                  _full((kv_rank, MLA_HEADS * MLA_V)), _full((1, LANES)),
                  pl.BlockSpec((tm, LANES), tab), pl.BlockSpec((tm, LANES), tab), pl.BlockSpec((tm, LANES), tab)],
        out_specs=[pl.BlockSpec((tm, hl), row), pl.BlockSpec((tm, hl), row),
                   pl.BlockSpec((tm, MLA_HEADS * MLA_V), row),
                   pl.BlockSpec((tm, fox_dim), row), pl.BlockSpec((tm, fox_dim), row), pl.BlockSpec((tm, fox_dim), row),
                   pl.BlockSpec((tm, LANES), row),
                   pl.BlockSpec((FOX_HEADS, tm), lambda i: (i // nseq, i % nseq))],
        out_shape=[jax.ShapeDtypeStruct((n, hl), BF16), jax.ShapeDtypeStruct((n, hl), BF16),
                   jax.ShapeDtypeStruct((n, MLA_HEADS * MLA_V), BF16),
                   jax.ShapeDtypeStruct((n, fox_dim), BF16), jax.ShapeDtypeStruct((n, fox_dim), BF16),
                   jax.ShapeDtypeStruct((n, fox_dim), BF16),
                   jax.ShapeDtypeStruct((n, LANES), F32),
                   jax.ShapeDtypeStruct((batch * FOX_HEADS, seq), F32)],
        scratch_shapes=[pltpu.VMEM((SUBLANES, LANES), F32)],
        compiler_params=_cparams(),
        name="odd_proj",
    )(x, w_in_p, q_norm_g.reshape(1, q_rank), kv_norm_g.reshape(1, kv_rank),
      wq.reshape(q_rank, hl).astype(BF16), wqr.reshape(q_rank, hl).astype(BF16),
      wk.reshape(kv_rank, hl).astype(BF16), wkr, wv.astype(BF16), fb, ctab, stab, ttab)
    return outs


def _attn_kernel(*refs, tq, scale, fox, head_lanes):
    if fox:
        q_ref, k_ref, v_ref, fcol_ref, frow_ref, o_ref = refs
    else:
        q_ref, k_ref, v_ref, o_ref = refs
    hp = pl.program_id(1)
    qi = pl.program_id(2)
    lane = lax.broadcasted_iota(I32, (tq, LANES), 1)
    qpos = qi * tq + lax.broadcasted_iota(I32, (tq, tq), 0)
    kpos = qi * tq + lax.broadcasted_iota(I32, (tq, tq), 1)
    if fox:
        allowed = kpos <= qpos
    else:
        allowed = (kpos // CHUNK) <= (qpos // CHUNK)
    outs = []
    for hh in range(2):
        if head_lanes == LANES:
            q = q_ref[0, :, hh * LANES:(hh + 1) * LANES]
        else:
            in_head = (lane >= hh * head_lanes) & (lane < (hh + 1) * head_lanes)
            q = jnp.where(in_head, q_ref[0], jnp.zeros((), BF16))
        if fox:
            h = hp * 2 + hh
            fq = jnp.sum(jnp.where(lane == MISC_FZ + h, fcol_ref[0], 0.0), axis=-1, keepdims=True)

        def step(j, carry, masked):
            m, l, acc = carry
            start = pl.multiple_of(j * tq, tq)
            if head_lanes == LANES:
                kt = k_ref[0, pl.ds(start, tq), hh * LANES:(hh + 1) * LANES]
            else:
                kt = k_ref[0, pl.ds(start, tq), :]
            s = lax.dot_general(q, kt, (((1,), (1,)), ((), ())), preferred_element_type=F32) * scale
            if fox:
                s = s + (fq - frow_ref[pl.ds(h, 1), pl.ds(start, tq)])
            if masked:
                s = jnp.where(allowed, s, NEG_INF)
            m_new = jnp.maximum(m, jnp.max(s, axis=-1, keepdims=True))
            a = jnp.exp(m - m_new)
            p = jnp.exp(s - m_new)
            l = a * l + jnp.sum(p, axis=-1, keepdims=True)
            acc = a * acc + _dot(p.astype(BF16), v_ref[0, pl.ds(start, tq), :])
            return m_new, l, acc

        init = (jnp.full((tq, 1), NEG_INF, F32), jnp.zeros((tq, 1), F32), jnp.zeros((tq, LANES), F32))
        carry = lax.fori_loop(0, qi, lambda j, cr: step(j, cr, False), init)
        m, l, acc = step(qi, carry, True)
        outs.append(acc / l)
    o_ref[0] = jnp.where(lane < LANES // 2, outs[0], outs[1]).astype(o_ref.dtype)


def _attention(q, k, v, fcol, frow, *, batch, seq, scale, fox, head_lanes):
    tq = min(ATTN_TILE, seq)
    n_heads = v.shape[2] // MLA_V
    qk_w = 2 * head_lanes
    in_specs = [pl.BlockSpec((1, tq, qk_w), lambda b, h, i: (b, i, h)),
                pl.BlockSpec((1, seq, qk_w), lambda b, h, i: (b, 0, h)),
                pl.BlockSpec((1, seq, LANES), lambda b, h, i: (b, 0, h))]
    args = [q, k, v]
    if fox:
        in_specs += [pl.BlockSpec((1, tq, LANES), lambda b, h, i: (b, i, 0)),
                     pl.BlockSpec((FOX_HEADS, seq), lambda b, h, i: (b, 0))]
        args += [fcol, frow]
    kern = functools.partial(_attn_kernel, tq=tq, scale=scale, fox=fox, head_lanes=head_lanes)
    return pl.pallas_call(
        kern,
        grid=(batch, n_heads // 2, seq // tq),
        in_specs=in_specs,
        out_specs=pl.BlockSpec((1, tq, LANES), lambda b, h, i: (b, i, h)),
        out_shape=jax.ShapeDtypeStruct((batch, seq, n_heads * MLA_V), BF16),
        compiler_params=_cparams(3),
        name="fox_attention" if fox else "mla_attention",
    )(*args)


def _outproj_kernel(x_ref, ya_ref, yb_ref, w_ref, g_ref, b_ref, xo_ref, xob_ref, *, alpha):
    mix = jnp.concatenate([ya_ref[...], yb_ref[...]], axis=-1)
    out = _dot(mix, w_ref[...])
    xn = _layernorm(alpha * x_ref[...] + out, g_ref[...], b_ref[...])
    xo_ref[...] = xn
    xob_ref[...] = xn.astype(BF16)


def _outproj_ln(x, ya, yb, w_out, ln_g, ln_b, *, alpha):
    n, d = x.shape
    tm = min(ROW_TILE, n)
    row = lambda i: (i, 0)
    return pl.pallas_call(
        functools.partial(_outproj_kernel, alpha=alpha),
        grid=(n // tm,),
        in_specs=[pl.BlockSpec((tm, d), row), pl.BlockSpec((tm, ya.shape[1]), row),
                  pl.BlockSpec((tm, yb.shape[1]), row), _full(w_out.shape), _full((1, d)), _full((1, d))],
        out_specs=[pl.BlockSpec((tm, d), row), pl.BlockSpec((tm, d), row)],
        out_shape=[jax.ShapeDtypeStruct((n, d), F32), jax.ShapeDtypeStruct((n, d), BF16)],
        compiler_params=_cparams(),
        name="odd_outproj",
    )(x, ya, yb, w_out.astype(BF16), ln_g.reshape(1, d), ln_b.reshape(1, d))


def _odd_layer(x, w_in, q_norm_g, kv_norm_g, w_uq, w_ukv, f_bias, w_out, ln_g, ln_b, *, batch, seq, alpha):
    n, d = x.shape
    q, k, v, fq, fk, fv, fcol, frow = _odd_proj(x, w_in, q_norm_g, kv_norm_g, w_uq, w_ukv, f_bias,
                                                batch=batch, seq=seq)
    r3 = lambda t: t.reshape(batch, seq, t.shape[1])
    y_c = _attention(r3(q), r3(k), r3(v), None, None, batch=batch, seq=seq,
                     scale=1.0 / math.sqrt(MLA_NOPE + MLA_ROPE), fox=False, head_lanes=LANES)
    y_d = _attention(r3(fq), r3(fk), r3(fv), r3(fcol), frow, batch=batch, seq=seq,
                     scale=1.0 / math.sqrt(FOX_HEAD_DIM), fox=True, head_lanes=FOX_HEAD_DIM)
    return _outproj_ln(x, y_c.reshape(n, -1), y_d.reshape(n, -1), w_out, ln_g, ln_b, alpha=alpha)


META_IDX = 0
META_RANK = TOP_K
META_GATE = 2 * TOP_K


def _router_kernel(x_ref, whi_ref, wlo_ref, b_ref, meta_ref, cnt_ref, run_ref, *, tm):
    i = pl.program_id(0)

    @pl.when(i == 0)
    def _():
        run_ref[...] = jnp.zeros((SUBLANES, LANES), F32)

    x = x_ref[...]
    xh, xl = _split2(x)
    whi = whi_ref[...]
    logits = _dot(xh, whi) + _dot(xl, whi) + _dot(xh, wlo_ref[...]) + b_ref[...]
    lane = lax.broadcasted_iota(I32, (tm, LANES), 1)
    work = logits
    vals, sels, idxs = [], [], []
    for _ in range(TOP_K):
        m = jnp.max(work, axis=-1, keepdims=True)
        idx = jnp.min(jnp.where(work == m, lane, LANES), axis=-1, keepdims=True)
        sel = lane == idx
        vals.append(m); sels.append(sel); idxs.append(idx)
        work = jnp.where(sel, -jnp.inf, work)
    es = [jnp.exp(vk - vals[0]) for vk in vals]
    den = es[0] + es[1] + es[2] + es[3]
    chosen = jnp.where(sels[0] | sels[1] | sels[2] | sels[3], 1.0, 0.0)
    r = lax.broadcasted_iota(I32, (tm, tm), 0)
    c = lax.broadcasted_iota(I32, (tm, tm), 1)
    tri = jnp.where(c < r, 1.0, 0.0).astype(BF16)
    before = _dot(tri, chosen.astype(BF16)) + run_ref[0:1, :]
    meta = jnp.zeros((tm, LANES), F32)
    for kk in range(TOP_K):
        rank = jnp.sum(jnp.where(sels[kk], before, 0.0), axis=-1, keepdims=True)
        meta = jnp.where(lane == META_IDX + kk, idxs[kk].astype(F32), meta)
        meta = jnp.where(lane == META_RANK + kk, rank, meta)
        meta = jnp.where(lane == META_GATE + kk, es[kk] / den, meta)
    meta_ref[...] = meta
    run = run_ref[0:1, :] + jnp.sum(chosen, axis=0, keepdims=True)
    run_ref[...] = jnp.broadcast_to(run, (SUBLANES, LANES))
    cnt_ref[...] = jnp.broadcast_to(run, (SUBLANES, LANES))


def _plan_kernel(cnt_ref, offs_ref, tile_ref, *, te, max_tiles):
    lane = lax.broadcasted_iota(I32, (SUBLANES, LANES), 1)
    cnt = cnt_ref[...]
    padded = jnp.floor((cnt + (te - 1)) / te) * te
    incl = padded
    for s in (1, 2, 4, 8, 16):
        incl = incl + jnp.where(lane >= s, pltpu.roll(incl, s, axis=1), 0.0)
    offs_ref[...] = incl - padded
    ends = incl[0:1, :]
    lane_t = lax.broadcasted_iota(I32, (max_tiles, LANES), 1)
    start = (lax.broadcasted_iota(I32, (max_tiles, LANES), 0) * te).astype(F32)
    done = jnp.where((ends <= start) & (lane_t < N_EXPERTS), 1.0, 0.0)
    expert = jnp.minimum(jnp.sum(done, axis=-1, keepdims=True), N_EXPERTS - 1.0)
    total = jnp.sum(jnp.where(lane_t == N_EXPERTS - 1, ends, 0.0), axis=-1, keepdims=True) / te
    tile_ref[...] = jnp.where(lane_t == 0, expert, jnp.where(lane_t == 1, total, 0.0)).astype(I32)


def _dest_kernel(meta_ref, offs_ref, dest_ref, *, tm):
    lane = lax.broadcasted_iota(I32, (tm, LANES), 1)
    meta = meta_ref[...]
    offs = offs_ref[0:1, :]
    dest = jnp.zeros((tm, LANES), F32)
    for kk in range(TOP_K):
        idx = jnp.sum(jnp.where(lane == META_IDX + kk, meta, 0.0), axis=-1, keepdims=True).astype(I32)
        rank = jnp.sum(jnp.where(lane == META_RANK + kk, meta, 0.0), axis=-1, keepdims=True)
        base = jnp.sum(jnp.where(lane == idx, offs, 0.0), axis=-1, keepdims=True)
        dest = jnp.where(lane == kk, base + rank, dest)
    dest_ref[...] = dest.astype(I32)


def _dispatch_kernel(dest_hbm, x_ref, xs_in, xs_hbm, idx_smem, isem, sem, *, tm):
    del xs_in
    i = pl.program_id(0)
    n_rows = tm * TOP_K
    icp = pltpu.make_async_copy(dest_hbm.at[pl.ds(i * n_rows, n_rows)], idx_smem, isem)
    icp.start()
    icp.wait()

    def row_copy(r, kk):
        d = idx_smem[r * TOP_K + kk]
        return pltpu.make_async_copy(x_ref.at[pl.ds(r, 1), :], xs_hbm.at[pl.ds(d, 1), :], sem)

    def issue(r, carry):
        for kk in range(TOP_K):
            row_copy(r, kk).start()
        return carry

    lax.fori_loop(0, tm, issue, 0)

    def drain(r, carry):
        for kk in range(TOP_K):
            row_copy(r, kk).wait()
        return carry

    lax.fori_loop(0, tm, drain, 0)


def _expert_kernel(te_ref, nt_ref, xs_ref, w1_ref, b1_ref, w2_ref, b2_ref, perm_ref, ys_ref, w1b_ref, w2b_ref,
                   *, te, d_ff):
    j = pl.program_id(0)
    e = te_ref[j]
    n_tiles = nt_ref[0]
    prev = te_ref[jnp.maximum(j - 1, 0)]

    @pl.when((j == 0) | (e != prev))
    def _():
        perm = perm_ref[...]
        for cb in range(2 * d_ff // (2 * LANES)):
            sl = slice(cb * 2 * LANES, (cb + 1) * 2 * LANES)
            w1b_ref[:, sl] = _dot(w1_ref[0, :, sl].astype(BF16), perm).astype(BF16)
        w2b_ref[...] = w2_ref[0].astype(BF16)

    @pl.when(j < n_tiles)
    def _():
        xb = xs_ref[...].astype(BF16)
        h = _dot(xb, w1b_ref[...]) + b1_ref[0]
        acts = []
        for cb in range(d_ff // LANES):
            glu = jnp.minimum(h[:, cb * 2 * LANES:cb * 2 * LANES + LANES], SWIGLU_LIMIT)
            lin = jnp.clip(h[:, cb * 2 * LANES + LANES:(cb + 1) * 2 * LANES], -SWIGLU_LIMIT, SWIGLU_LIMIT)
            sig = 1.0 / (1.0 + jnp.exp(-SWIGLU_ALPHA * glu))
            acts.append((glu * sig * (lin + 1.0)).astype(BF16))
        act = jnp.concatenate(acts, axis=-1)
        ys_ref[...] = _dot(act, w2b_ref[...]) + b2_ref[0]

    @pl.when(j >= n_tiles)
    def _():
        ys_ref[...] = jnp.zeros(ys_ref.shape, F32)


def _combine_kernel(dest_hbm, meta_ref, x_ref, g_ref, b_ref, ys_hbm, xo_ref, xob_ref, idx_smem, buf_ref, isem, sem,
                    *, tm, alpha):
    i = pl.program_id(0)
    n_rows = tm * TOP_K
    icp = pltpu.make_async_copy(dest_hbm.at[pl.ds(i * n_rows, n_rows)], idx_smem, isem)
    icp.start()
    icp.wait()

    def row_copy(r, kk):
        d = idx_smem[r * TOP_K + kk]
        return pltpu.make_async_copy(ys_hbm.at[pl.ds(d, 1), :], buf_ref.at[kk, pl.ds(r, 1), :], sem)

    def issue(r, carry):
        for kk in range(TOP_K):
            row_copy(r, kk).start()
        return carry

    lax.fori_loop(0, tm, issue, 0)

    def drain(r, carry):
        for kk in range(TOP_K):
            row_copy(r, kk).wait()
        return carry

    lax.fori_loop(0, tm, drain, 0)

    lane = lax.broadcasted_iota(I32, (tm, LANES), 1)
    meta = meta_ref[...]
    ffn = jnp.zeros(x_ref.shape, F32)
    for kk in range(TOP_K):
        gate = jnp.sum(jnp.where(lane == META_GATE + kk, meta, 0.0), axis=-1, keepdims=True)
        ffn = ffn + gate * buf_ref[kk]
    xn = _layernorm(alpha * x_ref[...] + ffn, g_ref[...], b_ref[...])
    xo_ref[...] = xn
    xob_ref[...] = xn.astype(BF16)


def _deinterleave_perm():
    p = np.zeros((2 * LANES, 2 * LANES), np.float32)
    j = np.arange(LANES)
    p[2 * j, j] = 1.0
    p[2 * j + 1, LANES + j] = 1.0
    return jnp.asarray(p, BF16)


def _moe_layer(x, w_router, b_router, w1, b1, w2, b2, ln_g, ln_b, *, alpha):
    n, d = x.shape
    n_exp, _, two_ff = w1.shape
    d_ff = two_ff // 2
    te = EXPERT_TILE
    max_tiles = (n * TOP_K) // te + n_exp
    max_tiles_p = -(-max_tiles // SUBLANES) * SUBLANES
    tm = min(ROW_TILE, n)
    row = lambda i: (i, 0)

    wr = jnp.zeros((d, LANES), F32).at[:, :n_exp].set(w_router)
    wr_hi = wr.astype(BF16)
    wr_lo = (wr - wr_hi.astype(F32)).astype(BF16)
    br = jnp.full((1, LANES), NEG_INF, F32).at[0, :n_exp].set(b_router)
    meta, counts = pl.pallas_call(
        functools.partial(_router_kernel, tm=tm),
        grid=(n // tm,),
        in_specs=[pl.BlockSpec((tm, d), row), _full((d, LANES)), _full((d, LANES)), _full((1, LANES))],
        out_specs=[pl.BlockSpec((tm, LANES), row), _full((SUBLANES, LANES))],
        out_shape=[jax.ShapeDtypeStruct((n, LANES), F32), jax.ShapeDtypeStruct((SUBLANES, LANES), F32)],
        scratch_shapes=[pltpu.VMEM((SUBLANES, LANES), F32)],
        compiler_params=_cparams(),
        name="moe_router",
    )(x, wr_hi, wr_lo, br)

    offs, tiles = pl.pallas_call(
        functools.partial(_plan_kernel, te=te, max_tiles=max_tiles_p),
        out_shape=[jax.ShapeDtypeStruct((SUBLANES, LANES), F32), jax.ShapeDtypeStruct((max_tiles_p, LANES), I32)],
        name="moe_plan",
    )(counts)

    dest = pl.pallas_call(
        functools.partial(_dest_kernel, tm=tm),
        grid=(n // tm,),
        in_specs=[pl.BlockSpec((tm, LANES), row), _full((SUBLANES, LANES))],
        out_specs=pl.BlockSpec((tm, LANES), row),
        out_shape=jax.ShapeDtypeStruct((n, LANES), I32),
        compiler_params=_cparams(),
        name="moe_dest",
    )(meta, offs)
    dest_flat = dest[:, :TOP_K].reshape(n * TOP_K)
    tile_expert = tiles[:, 0]
    n_tiles = tiles[0:1, 1]

    tmv = min(MOVE_TILE, n)
    n_slots = max_tiles * te
    xs = pl.pallas_call(
        functools.partial(_dispatch_kernel, tm=tmv),
        grid=(n // tmv,),
        in_specs=[pl.BlockSpec(memory_space=pl.ANY), pl.BlockSpec((tmv, d), row), pl.BlockSpec(memory_space=pl.ANY)],
        out_specs=pl.BlockSpec(memory_space=pl.ANY),
        out_shape=jax.ShapeDtypeStruct((n_slots, d), F32),
        scratch_shapes=[pltpu.SMEM((tmv * TOP_K,), I32), pltpu.SemaphoreType.DMA, pltpu.SemaphoreType.DMA],
        input_output_aliases={2: 0},
        compiler_params=_cparams(),
        name="moe_dispatch",
    )(dest_flat, x, jnp.zeros((n_slots, d), F32))

    b1_r = b1.reshape(n_exp, d_ff // LANES, LANES, 2).transpose(0, 1, 3, 2).reshape(n_exp, 1, two_ff)
    ys = pl.pallas_call(
        functools.partial(_expert_kernel, te=te, d_ff=d_ff),
        grid_spec=pltpu.PrefetchScalarGridSpec(
            num_scalar_prefetch=2,
            grid=(max_tiles,),
            in_specs=[pl.BlockSpec((te, d), lambda j, t, nt: (j, 0)),
                      pl.BlockSpec((1, d, two_ff), lambda j, t, nt: (t[j], 0, 0)),
                      pl.BlockSpec((1, 1, two_ff), lambda j, t, nt: (t[j], 0, 0)),
                      pl.BlockSpec((1, d_ff, d), lambda j, t, nt: (t[j], 0, 0)),
                      pl.BlockSpec((1, 1, d), lambda j, t, nt: (t[j], 0, 0)),
                      pl.BlockSpec((2 * LANES, 2 * LANES), lambda j, t, nt: (0, 0))],
            out_specs=pl.BlockSpec((te, d), lambda j, t, nt: (j, 0)),
            scratch_shapes=[pltpu.VMEM((d, two_ff), BF16), pltpu.VMEM((d_ff, d), BF16)]),
        out_shape=jax.ShapeDtypeStruct((n_slots, d), F32),
        compiler_params=_cparams(),
        name="moe_experts",
    )(tile_expert, n_tiles, xs, w1, b1_r, w2, b2.reshape(n_exp, 1, d), _deinterleave_perm())

    xo, xob = pl.pallas_call(
        functools.partial(_combine_kernel, tm=tmv, alpha=alpha),
        grid=(n // tmv,),
        in_specs=[pl.BlockSpec(memory_space=pl.ANY), pl.BlockSpec((tmv, LANES), row), pl.BlockSpec((tmv, d), row),
                  _full((1, d)), _full((1, d)), pl.BlockSpec(memory_space=pl.ANY)],
        out_specs=[pl.BlockSpec((tmv, d), row), pl.BlockSpec((tmv, d), row)],
        out_shape=[jax.ShapeDtypeStruct((n, d), F32), jax.ShapeDtypeStruct((n, d), BF16)],
        scratch_shapes=[pltpu.SMEM((tmv * TOP_K,), I32), pltpu.VMEM((TOP_K, tmv, d), F32),
                        pltpu.SemaphoreType.DMA, pltpu.SemaphoreType.DMA],
        compiler_params=_cparams(),
        name="moe_combine",
    )(dest_flat, meta, x, ln_g.reshape(1, d), ln_b.reshape(1, d), ys)
    return xo, xob


def kernel(x, ev_w_in, ev_conv_w, ev_sgu_ln_g, ev_sgu_ln_b, ev_sgu_w, ev_sgu_b, ev_w_out, od_w_in, od_q_norm_g, od_kv_norm_g, od_w_uq, od_w_ukv, od_f_bias, od_w_out, ln_mix_g, ln_mix_b, ln_ffn_g, ln_ffn_b, moe_w_router, moe_b_router, moe_w1, moe_b1, moe_w2, moe_b2):
    batch, seq, d = x.shape
    depth = ln_mix_g.shape[0]
    alpha = (2 * depth) ** 0.25
    xf = x.reshape(batch * seq, d)
    for layer in range(depth):
        i = layer // 2
        if layer % 2 == 0:
            xf, _ = _even_layer(xf, ev_w_in[i], ev_conv_w[i], ev_sgu_ln_g[i], ev_sgu_ln_b[i], ev_sgu_w[i],
                                ev_sgu_b[i], ev_w_out[i], ln_mix_g[layer], ln_mix_b[layer], seq=seq, alpha=alpha)
        else:
            xf, _ = _odd_layer(xf, od_w_in[i], od_q_norm_g[i], od_kv_norm_g[i], od_w_uq[i], od_w_ukv[i],
                               od_f_bias[i], od_w_out[i], ln_mix_g[layer], ln_mix_b[layer],
                               batch=batch, seq=seq, alpha=alpha)
        xf, _ = _moe_layer(xf, moe_w_router[layer], moe_b_router[layer], moe_w1[layer], moe_b1[layer],
                           moe_w2[layer], moe_b2[layer], ln_ffn_g[layer], ln_ffn_b[layer], alpha=alpha)
    return xf.reshape(batch, seq, d)
```

```python
import functools
import math

import numpy as np
import jax
import jax.numpy as jnp
from jax import lax
from jax.experimental import pallas as pl
from jax.experimental.pallas import tpu as pltpu

F32 = jnp.float32
BF16 = jnp.bfloat16
I32 = jnp.int32

CHUNK = 64
CONV_WIDTH = 3
SGU_BLOCK = 128
GROUP_DIM = 64
MLA_HEADS = 8
MLA_NOPE = 64
MLA_ROPE = 32
MLA_V = 64
ROPE_THETA = 10000.0
FOX_HEADS = 8
FOX_HEAD_DIM = 64
N_EXPERTS = 32
TOP_K = 4
SWIGLU_ALPHA = 1.702
SWIGLU_LIMIT = 7.0
NORM_EPS = 1e-5
NEG_INF = -1e30

LANES = 128
SUBLANES = 8
VMEM_LIMIT = 56 * 1024 * 1024

ROW_TILE = 512
ATTN_TILE = 512
LOG2E = math.log2(math.e)
EXPERT_TILE = 256
MOVE_TILE = 256


def _cparams(n_axes=1):
    return pltpu.CompilerParams(dimension_semantics=("arbitrary",) * n_axes,
                                vmem_limit_bytes=VMEM_LIMIT)


def _dot(a, b):
    return jnp.dot(a, b, preferred_element_type=F32)


def _split2(v):
    hi = v.astype(BF16)
    lo = (v - hi.astype(F32)).astype(BF16)
    return hi, lo


def _split3(v):
    hi = v.astype(BF16)
    r1 = v - hi.astype(F32)
    mid = r1.astype(BF16)
    lo = (r1 - mid.astype(F32)).astype(BF16)
    return hi, mid, lo


def _layernorm(x, g, b):
    mu = jnp.mean(x, axis=-1, keepdims=True)
    xc = x - mu
    var = jnp.mean(xc * xc, axis=-1, keepdims=True)
    return xc * lax.rsqrt(var + NORM_EPS) * g + b


def _gelu_tanh(x):
    c = math.sqrt(2.0 / math.pi)
    return x * (0.5 * (1.0 + jnp.tanh(c * (x + 0.044715 * (x * x * x)))))


def _full(shape):
    nd = len(shape)
    return pl.BlockSpec(shape, lambda *_: (0,) * nd)


def _even_kernel(x_ref, win_ref, convw_ref, gavg_ref, lng_ref, lnb_ref, ws_ref, sb_ref, wout_ref,
                 g_ref, b_ref, xo_ref, xob_ref, hs_ref, mix_ref, *, tm, seq, alpha, a_dim):
    i = pl.program_id(0)
    x = x_ref[...]
    proj = _dot(x.astype(BF16), win_ref[...])
    a_c = proj[:, 0:a_dim]
    a_b = proj[:, a_dim:2 * a_dim]
    a_v = proj[:, 2 * a_dim:3 * a_dim]
    b_u = proj[:, 3 * a_dim:4 * a_dim]
    b_v = proj[:, 4 * a_dim:5 * a_dim]

    h = a_c * a_v

    @pl.when((i * tm) % seq == 0)
    def _():
        hs_ref[0:SUBLANES, :] = jnp.zeros((SUBLANES, a_dim), F32)

    hs_ref[SUBLANES:SUBLANES + tm, :] = h
    conv = (hs_ref[SUBLANES - 2:SUBLANES - 2 + tm, :] * convw_ref[0:1, :]
            + hs_ref[SUBLANES - 1:SUBLANES - 1 + tm, :] * convw_ref[1:2, :]
            + h * convw_ref[2:3, :])
    hs_ref[0:SUBLANES, :] = h[tm - SUBLANES:tm, :]
    mix_ref[:, 0:a_dim] = (a_b * conv).astype(BF16)

    u = _gelu_tanh(b_u)
    v = _gelu_tanh(b_v)
    gavg = gavg_ref[...]
    lane = lax.broadcasted_iota(I32, (SGU_BLOCK, LANES), 1)
    low = lane < GROUP_DIM
    for c in range(a_dim // LANES):
        vc = v[:, c * LANES:(c + 1) * LANES]
        hi, lo = _split2(vc)
        mean = _dot(hi, gavg) + _dot(lo, gavg)
        d = vc - mean
        hi, lo = _split2(d * d)
        var = _dot(hi, gavg) + _dot(lo, gavg)
        vn = d * lax.rsqrt(var + NORM_EPS) * lng_ref[:, c * LANES:(c + 1) * LANES] \
            + lnb_ref[:, c * LANES:(c + 1) * LANES]
        for blk in range(tm // SGU_BLOCK):
            vb = vn[blk * SGU_BLOCK:(blk + 1) * SGU_BLOCK, :]
            rhs = jnp.concatenate([jnp.where(low, vb, 0.0), jnp.where(low, 0.0, vb)], axis=0).astype(BF16)
            sg = _dot(ws_ref[c], rhs) + sb_ref[:, c * LANES:(c + 1) * LANES]
            ub = u[blk * SGU_BLOCK:(blk + 1) * SGU_BLOCK, c * LANES:(c + 1) * LANES]
            mix_ref[blk * SGU_BLOCK:(blk + 1) * SGU_BLOCK, a_dim + c * LANES:a_dim + (c + 1) * LANES] = \
                (ub * sg).astype(BF16)

    out = _dot(mix_ref[...], wout_ref[...])
    xn = _layernorm(alpha * x + out, g_ref[...], b_ref[...])
    xo_ref[...] = xn
    xob_ref[...] = xn.astype(BF16)


def _even_layer(x, w_in, conv_w, sgu_ln_g, sgu_ln_b, sgu_w, sgu_b, w_out, ln_g, ln_b, *, seq, alpha):
    n, d = x.shape
    a_dim = conv_w.shape[1]
    tm = min(ROW_TILE, seq)
    n_groups = sgu_w.shape[0]
    pos = np.arange(SGU_BLOCK)
    mask = (pos[None, :] // CHUNK) <= (pos[:, None] // CHUNK)
    w_s = jnp.where(mask, sgu_w, 0.0)
    ws_pairs = jnp.concatenate([w_s[0::2], w_s[1::2]], axis=2).astype(BF16)
    sb_full = jnp.repeat(sgu_b.T, GROUP_DIM, axis=1)
    convw = jnp.zeros((SUBLANES, a_dim), F32).at[0:CONV_WIDTH].set(conv_w)
    g_idx = np.arange(LANES) // GROUP_DIM
    gavg = jnp.asarray((g_idx[:, None] == g_idx[None, :]).astype(np.float32) / GROUP_DIM, BF16)
    assert n_groups * GROUP_DIM == a_dim and w_in.shape[1] == 5 * a_dim
    kern = functools.partial(_even_kernel, tm=tm, seq=seq, alpha=alpha, a_dim=a_dim)
    return pl.pallas_call(
        kern,
        grid=(n // tm,),
        in_specs=[pl.BlockSpec((tm, d), lambda i: (i, 0)),
                  _full(w_in.shape), _full(convw.shape), _full(gavg.shape),
                  _full((1, a_dim)), _full((1, a_dim)), _full(ws_pairs.shape), _full(sb_full.shape),
                  _full(w_out.shape), _full((1, d)), _full((1, d))],
        out_specs=[pl.BlockSpec((tm, d), lambda i: (i, 0)), pl.BlockSpec((tm, d), lambda i: (i, 0))],
        out_shape=[jax.ShapeDtypeStruct((n, d), F32), jax.ShapeDtypeStruct((n, d), BF16)],
        scratch_shapes=[pltpu.VMEM((tm + SUBLANES, a_dim), F32), pltpu.VMEM((tm, 2 * a_dim), BF16)],
        compiler_params=_cparams(),
        name="even_mixer",
    )(x, w_in.astype(BF16), convw, gavg, sgu_ln_g.reshape(1, a_dim), sgu_ln_b.reshape(1, a_dim),
      ws_pairs, sb_full, w_out.astype(BF16), ln_g.reshape(1, d), ln_b.reshape(1, d))


MISC_KR = 0
MISC_KR_ROT = MLA_ROPE
MISC_FZ = 2 * MLA_ROPE


def _odd_proj_kernel(x_ref, win_ref, qg_ref, kvg_ref, wq_ref, wqr_ref, wk_ref, wkr_ref, wv_ref,
                     fb_ref, ctab_ref, stab_ref, ttab_ref,
                     q_ref, k_ref, v_ref, fq_ref, fk_ref, fv_ref, fcol_ref, frow_ref, carry_ref,
                     *, tm, seq, q_rank, kv_rank, fox_dim, q_scale, fq_scale):
    i = pl.program_id(0)
    x = x_ref[...]
    proj = _dot(x.astype(BF16), win_ref[...])
    o = 0
    c_q = proj[:, o:o + q_rank]; o += q_rank
    c_kv = proj[:, o:o + kv_rank]; o += kv_rank
    misc = proj[:, o:o + LANES]; o += LANES
    fq_ref[...] = (proj[:, o:o + fox_dim] * fq_scale).astype(BF16); o += fox_dim
    fk_ref[...] = proj[:, o:o + fox_dim].astype(BF16); o += fox_dim
    fv_ref[...] = proj[:, o:o + fox_dim].astype(BF16)

    cqn = (c_q * lax.rsqrt(jnp.mean(c_q * c_q, axis=-1, keepdims=True) + NORM_EPS) * qg_ref[...]).astype(BF16)
    ckn = (c_kv * lax.rsqrt(jnp.mean(c_kv * c_kv, axis=-1, keepdims=True) + NORM_EPS) * kvg_ref[...]).astype(BF16)

    q1 = _dot(cqn, wq_ref[...])
    q2 = _dot(cqn, wqr_ref[...])
    ctab = ctab_ref[...]
    stab = stab_ref[...]
    for hh in range(MLA_HEADS):
        sl = slice(hh * LANES, (hh + 1) * LANES)
        q_ref[:, sl] = ((q1[:, sl] * ctab + q2[:, sl] * stab) * q_scale).astype(BF16)

    km_hi, km_lo = _split2(misc * ttab_ref[...])
    k_ref[...] = (_dot(ckn, wk_ref[...]) + _dot(km_hi, wkr_ref[...]) + _dot(km_lo, wkr_ref[...])).astype(BF16)
    v_ref[...] = _dot(ckn, wv_ref[...]).astype(BF16)

    lane = lax.broadcasted_iota(I32, (tm, LANES), 1)
    z = misc + fb_ref[...]
    lf = -(jnp.maximum(-z, 0.0) + jnp.log1p(jnp.exp(-jnp.abs(z))))
    lf = jnp.where((lane >= MISC_FZ) & (lane < MISC_FZ + FOX_HEADS), lf, 0.0)
    r = lax.broadcasted_iota(I32, (tm, tm), 0)
    c = lax.broadcasted_iota(I32, (tm, tm), 1)
    tri = jnp.where(c <= r, 1.0, 0.0).astype(BF16)
    hi, mid, lo = _split3(lf)
    incl = _dot(tri, hi) + _dot(tri, mid) + _dot(tri, lo)

    @pl.when((i * tm) % seq == 0)
    def _():
        carry_ref[...] = jnp.zeros((SUBLANES, LANES), F32)

    fcum = incl + carry_ref[0:1, :]
    carry_ref[...] = jnp.broadcast_to(fcum[tm - 1:tm, :], (SUBLANES, LANES))
    fsc = fcum * LOG2E
    fcol_ref[...] = fsc
    frow_ref[...] = fsc.T[MISC_FZ:MISC_FZ + FOX_HEADS, :]


def _rope_tables(seq):
    half = MLA_ROPE // 2
    inv_freq = ROPE_THETA ** (-np.arange(0, MLA_ROPE, 2, dtype=np.float64) / MLA_ROPE)
    ang = np.arange(seq, dtype=np.float64)[:, None] * inv_freq[None, :]
    cos = np.concatenate([np.cos(ang), np.cos(ang)], axis=1)
    sin = np.concatenate([np.sin(ang), np.sin(ang)], axis=1)
    assert cos.shape[1] == 2 * half
    ctab = np.zeros((seq, LANES)); ctab[:, :MLA_NOPE] = 1.0; ctab[:, MLA_NOPE:MLA_NOPE + MLA_ROPE] = cos
    stab = np.zeros((seq, LANES)); stab[:, MLA_NOPE:MLA_NOPE + MLA_ROPE] = sin
    ttab = np.zeros((seq, LANES)); ttab[:, MISC_KR:MISC_KR + MLA_ROPE] = cos
    ttab[:, MISC_KR_ROT:MISC_KR_ROT + MLA_ROPE] = sin
    return (jnp.asarray(ctab, F32), jnp.asarray(stab, F32), jnp.asarray(ttab, F32))


def _rot_cols(w):
    half = MLA_ROPE // 2
    return jnp.concatenate([-w[..., half:], w[..., :half]], axis=-1)


def _odd_proj(x, w_in, q_norm_g, kv_norm_g, w_uq, w_ukv, f_bias, *, batch, seq):
    n, d = x.shape
    q_rank, kv_rank = q_norm_g.shape[0], kv_norm_g.shape[0]
    fox_dim = FOX_HEADS * FOX_HEAD_DIM
    tm = min(ROW_TILE, seq)
    o = 0
    w_cq = w_in[:, o:o + q_rank]; o += q_rank
    w_ckv = w_in[:, o:o + kv_rank]; o += kv_rank
    w_kr = w_in[:, o:o + MLA_ROPE]; o += MLA_ROPE
    w_f = w_in[:, o:o + 3 * fox_dim]; o += 3 * fox_dim
    w_fz = w_in[:, o:o + FOX_HEADS]
    w_misc = jnp.zeros((d, LANES), F32)
    w_misc = w_misc.at[:, MISC_KR:MISC_KR + MLA_ROPE].set(w_kr)
    w_misc = w_misc.at[:, MISC_KR_ROT:MISC_KR_ROT + MLA_ROPE].set(_rot_cols(w_kr))
    w_misc = w_misc.at[:, MISC_FZ:MISC_FZ + FOX_HEADS].set(w_fz)
    w_in_p = jnp.concatenate([w_cq, w_ckv, w_misc, w_f], axis=1).astype(BF16)
    dq = MLA_NOPE + MLA_ROPE
    wq3 = w_uq.reshape(q_rank, MLA_HEADS, dq)
    wq = jnp.zeros((q_rank, MLA_HEADS, LANES), F32).at[:, :, :dq].set(wq3)
    wqr = jnp.zeros((q_rank, MLA_HEADS, LANES), F32).at[:, :, MLA_NOPE:dq].set(_rot_cols(wq3[:, :, MLA_NOPE:]))
    wkv3 = w_ukv.reshape(kv_rank, MLA_HEADS, MLA_NOPE + MLA_V)
    wk = jnp.zeros((kv_rank, MLA_HEADS, LANES), F32).at[:, :, :MLA_NOPE].set(wkv3[:, :, :MLA_NOPE])
    wv = wkv3[:, :, MLA_NOPE:].reshape(kv_rank, MLA_HEADS * MLA_V)
    place = np.zeros((LANES, MLA_HEADS, LANES), np.float32)
    for j in range(MLA_ROPE):
        place[MISC_KR + j, :, MLA_NOPE + j] = 1.0
        place[MISC_KR_ROT + j, :, MLA_NOPE + j] = 1.0
    wkr = jnp.asarray(place.reshape(LANES, MLA_HEADS * LANES), BF16)
    fb = jnp.zeros((1, LANES), F32).at[0, MISC_FZ:MISC_FZ + FOX_HEADS].set(f_bias)
    ctab, stab, ttab = _rope_tables(seq)
    hl = MLA_HEADS * LANES
    nseq = seq // tm
    row = lambda i: (i, 0)
    tab = lambda i: (i % nseq, 0)
    kern = functools.partial(_odd_proj_kernel, tm=tm, seq=seq, q_rank=q_rank, kv_rank=kv_rank, fox_dim=fox_dim,
                             q_scale=LOG2E / math.sqrt(MLA_NOPE + MLA_ROPE), fq_scale=LOG2E / math.sqrt(FOX_HEAD_DIM))
    outs = pl.pallas_call(
        kern,
        grid=(n // tm,),
        in_specs=[pl.BlockSpec((tm, d), row), _full(w_in_p.shape), _full((1, q_rank)), _full((1, kv_rank)),
                  _full((q_rank, hl)), _full((q_rank, hl)), _full((kv_rank, hl)), _full((LANES, hl)),
                  _full((kv_rank, MLA_HEADS * MLA_V)), _full((1, LANES)),
                  pl.BlockSpec((tm, LANES), tab), pl.BlockSpec((tm, LANES), tab), pl.BlockSpec((tm, LANES), tab)],
        out_specs=[pl.BlockSpec((tm, hl), row), pl.BlockSpec((tm, hl), row),
                   pl.BlockSpec((tm, MLA_HEADS * MLA_V), row),
                   pl.BlockSpec((tm, fox_dim), row), pl.BlockSpec((tm, fox_dim), row), pl.BlockSpec((tm, fox_dim), row),
                   pl.BlockSpec((tm, LANES), row),
                   pl.BlockSpec((FOX_HEADS, tm), lambda i: (i // nseq, i % nseq))],
        out_shape=[jax.ShapeDtypeStruct((n, hl), BF16), jax.ShapeDtypeStruct((n, hl), BF16),
                   jax.ShapeDtypeStruct((n, MLA_HEADS * MLA_V), BF16),
                   jax.ShapeDtypeStruct((n, fox_dim), BF16), jax.ShapeDtypeStruct((n, fox_dim), BF16),
                   jax.ShapeDtypeStruct((n, fox_dim), BF16),
                   jax.ShapeDtypeStruct((n, LANES), F32),
                   jax.ShapeDtypeStruct((batch * FOX_HEADS, seq), F32)],
        scratch_shapes=[pltpu.VMEM((SUBLANES, LANES), F32)],
        compiler_params=_cparams(),
        name="odd_proj",
    )(x, w_in_p, q_norm_g.reshape(1, q_rank), kv_norm_g.reshape(1, kv_rank),
      wq.reshape(q_rank, hl).astype(BF16), wqr.reshape(q_rank, hl).astype(BF16),
      wk.reshape(kv_rank, hl).astype(BF16), wkr, wv.astype(BF16), fb, ctab, stab, ttab)
    return outs


def _attn_kernel(*refs, tq, fox, head_lanes):
    if fox:
        q_ref, k_ref, v_ref, fcol_ref, frow_ref, o_ref = refs
    else:
        q_ref, k_ref, v_ref, o_ref = refs
    hp = pl.program_id(1)
    qi = pl.program_id(2)
    lane = lax.broadcasted_iota(I32, (tq, LANES), 1)
    qpos = qi * tq + lax.broadcasted_iota(I32, (tq, tq), 0)
    kpos = qi * tq + lax.broadcasted_iota(I32, (tq, tq), 1)
    if fox:
        allowed = kpos <= qpos
    else:
        allowed = (kpos // CHUNK) <= (qpos // CHUNK)
    qs, fqs = [], []
    for hh in range(2):
        if head_lanes == LANES:
            qs.append(q_ref[0, :, hh * LANES:(hh + 1) * LANES])
        else:
            in_head = (lane >= hh * head_lanes) & (lane < (hh + 1) * head_lanes)
            qs.append(jnp.where(in_head, q_ref[0], jnp.zeros((), BF16)))
        if fox:
            fqs.append(jnp.sum(jnp.where(lane == MISC_FZ + hp * 2 + hh, fcol_ref[0], 0.0),
                               axis=-1, keepdims=True))

    def step(j, carry, masked):
        start = pl.multiple_of(j * tq, tq)
        vt = v_ref[0, pl.ds(start, tq), :]
        new = []
        for hh in range(2):
            m, l, acc = carry[hh]
            if head_lanes == LANES:
                kt = k_ref[0, pl.ds(start, tq), hh * LANES:(hh + 1) * LANES]
            else:
                kt = k_ref[0, pl.ds(start, tq), :]
            s = lax.dot_general(qs[hh], kt, (((1,), (1,)), ((), ())), preferred_element_type=F32)
            if fox:
                s = s + (fqs[hh] - frow_ref[pl.ds(hp * 2 + hh, 1), pl.ds(start, tq)])
            if masked:
                s = jnp.where(allowed, s, NEG_INF)
            m_new = jnp.maximum(m, jnp.max(s, axis=-1, keepdims=True))
            a = jnp.exp2(m - m_new)
            p = jnp.exp2(s - m_new)
            l = a * l + jnp.sum(p, axis=-1, keepdims=True)
            acc = a * acc + _dot(p.astype(BF16), vt)
            new.append((m_new, l, acc))
        return tuple(new)

    one = (jnp.full((tq, 1), NEG_INF, F32), jnp.zeros((tq, 1), F32), jnp.zeros((tq, LANES), F32))
    carry = lax.fori_loop(0, qi, lambda j, cr: step(j, cr, False), (one, one))
    (_, l0, acc0), (_, l1, acc1) = step(qi, carry, True)
    o_ref[0] = jnp.where(lane < LANES // 2, acc0 / l0, acc1 / l1).astype(o_ref.dtype)


def _attention(q, k, v, fcol, frow, *, batch, seq, fox, head_lanes):
    tq = min(ATTN_TILE, seq)
    n_heads = v.shape[2] // MLA_V
    qk_w = 2 * head_lanes
    in_specs = [pl.BlockSpec((1, tq, qk_w), lambda b, h, i: (b, i, h)),
                pl.BlockSpec((1, seq, qk_w), lambda b, h, i: (b, 0, h)),
                pl.BlockSpec((1, seq, LANES), lambda b, h, i: (b, 0, h))]
    args = [q, k, v]
    if fox:
        in_specs += [pl.BlockSpec((1, tq, LANES), lambda b, h, i: (b, i, 0)),
                     pl.BlockSpec((FOX_HEADS, seq), lambda b, h, i: (b, 0))]
        args += [fcol, frow]
    kern = functools.partial(_attn_kernel, tq=tq, fox=fox, head_lanes=head_lanes)
    return pl.pallas_call(
        kern,
        grid=(batch, n_heads // 2, seq // tq),
        in_specs=in_specs,
        out_specs=pl.BlockSpec((1, tq, LANES), lambda b, h, i: (b, i, h)),
        out_shape=jax.ShapeDtypeStruct((batch, seq, n_heads * MLA_V), BF16),
        compiler_params=_cparams(3),
        name="fox_attention" if fox else "mla_attention",
    )(*args)


def _outproj_kernel(x_ref, ya_ref, yb_ref, w_ref, g_ref, b_ref, xo_ref, xob_ref, *, alpha):
    mix = jnp.concatenate([ya_ref[...], yb_ref[...]], axis=-1)
    out = _dot(mix, w_ref[...])
    xn = _layernorm(alpha * x_ref[...] + out, g_ref[...], b_ref[...])
    xo_ref[...] = xn
    xob_ref[...] = xn.astype(BF16)


def _outproj_ln(x, ya, yb, w_out, ln_g, ln_b, *, alpha):
    n, d = x.shape
    tm = min(ROW_TILE, n)
    row = lambda i: (i, 0)
    return pl.pallas_call(
        functools.partial(_outproj_kernel, alpha=alpha),
        grid=(n // tm,),
        in_specs=[pl.BlockSpec((tm, d), row), pl.BlockSpec((tm, ya.shape[1]), row),
                  pl.BlockSpec((tm, yb.shape[1]), row), _full(w_out.shape), _full((1, d)), _full((1, d))],
        out_specs=[pl.BlockSpec((tm, d), row), pl.BlockSpec((tm, d), row)],
        out_shape=[jax.ShapeDtypeStruct((n, d), F32), jax.ShapeDtypeStruct((n, d), BF16)],
        compiler_params=_cparams(),
        name="odd_outproj",
    )(x, ya, yb, w_out.astype(BF16), ln_g.reshape(1, d), ln_b.reshape(1, d))


def _odd_layer(x, w_in, q_norm_g, kv_norm_g, w_uq, w_ukv, f_bias, w_out, ln_g, ln_b, *, batch, seq, alpha):
    n, d = x.shape
    q, k, v, fq, fk, fv, fcol, frow = _odd_proj(x, w_in, q_norm_g, kv_norm_g, w_uq, w_ukv, f_bias,
                                                batch=batch, seq=seq)
    r3 = lambda t: t.reshape(batch, seq, t.shape[1])
    y_c = _attention(r3(q), r3(k), r3(v), None, None, batch=batch, seq=seq, fox=False, head_lanes=LANES)
    y_d = _attention(r3(fq), r3(fk), r3(fv), r3(fcol), frow, batch=batch, seq=seq, fox=True,
                     head_lanes=FOX_HEAD_DIM)
    return _outproj_ln(x, y_c.reshape(n, -1), y_d.reshape(n, -1), w_out, ln_g, ln_b, alpha=alpha)


META_IDX = 0
META_RANK = TOP_K
META_GATE = 2 * TOP_K


def _router_kernel(x_ref, whi_ref, wlo_ref, b_ref, meta_ref, cnt_ref, run_ref, *, tm):
    i = pl.program_id(0)

    @pl.when(i == 0)
    def _():
        run_ref[...] = jnp.zeros((SUBLANES, LANES), F32)

    x = x_ref[...]
    xh, xl = _split2(x)
    whi = whi_ref[...]
    logits = _dot(xh, whi) + _dot(xl, whi) + _dot(xh, wlo_ref[...]) + b_ref[...]
    lane = lax.broadcasted_iota(I32, (tm, LANES), 1)
    work = logits
    vals, sels, idxs = [], [], []
    for _ in range(TOP_K):
        m = jnp.max(work, axis=-1, keepdims=True)
        idx = jnp.min(jnp.where(work == m, lane, LANES), axis=-1, keepdims=True)
        sel = lane == idx
        vals.append(m); sels.append(sel); idxs.append(idx)
        work = jnp.where(sel, -jnp.inf, work)
    es = [jnp.exp(vk - vals[0]) for vk in vals]
    den = es[0] + es[1] + es[2] + es[3]
    chosen = jnp.where(sels[0] | sels[1] | sels[2] | sels[3], 1.0, 0.0)
    r = lax.broadcasted_iota(I32, (tm, tm), 0)
    c = lax.broadcasted_iota(I32, (tm, tm), 1)
    tri = jnp.where(c < r, 1.0, 0.0).astype(BF16)
    before = _dot(tri, chosen.astype(BF16)) + run_ref[0:1, :]
    meta = jnp.zeros((tm, LANES), F32)
    for kk in range(TOP_K):
        rank = jnp.sum(jnp.where(sels[kk], before, 0.0), axis=-1, keepdims=True)
        meta = jnp.where(lane == META_IDX + kk, idxs[kk].astype(F32), meta)
        meta = jnp.where(lane == META_RANK + kk, rank, meta)
        meta = jnp.where(lane == META_GATE + kk, es[kk] / den, meta)
    meta_ref[...] = meta
    run = run_ref[0:1, :] + jnp.sum(chosen, axis=0, keepdims=True)
    run_ref[...] = jnp.broadcast_to(run, (SUBLANES, LANES))
    cnt_ref[...] = jnp.broadcast_to(run, (SUBLANES, LANES))


def _plan_kernel(cnt_ref, offs_ref, tile_ref, *, te, max_tiles):
    lane = lax.broadcasted_iota(I32, (SUBLANES, LANES), 1)
    cnt = cnt_ref[...]
    padded = jnp.floor((cnt + (te - 1)) / te) * te
    incl = padded
    for s in (1, 2, 4, 8, 16):
        incl = incl + jnp.where(lane >= s, pltpu.roll(incl, s, axis=1), 0.0)
    offs_ref[...] = incl - padded
    ends = incl[0:1, :]
    lane_t = lax.broadcasted_iota(I32, (max_tiles, LANES), 1)
    start = (lax.broadcasted_iota(I32, (max_tiles, LANES), 0) * te).astype(F32)
    done = jnp.where((ends <= start) & (lane_t < N_EXPERTS), 1.0, 0.0)
    expert = jnp.minimum(jnp.sum(done, axis=-1, keepdims=True), N_EXPERTS - 1.0)
    total = jnp.sum(jnp.where(lane_t == N_EXPERTS - 1, ends, 0.0), axis=-1, keepdims=True) / te
    tile_ref[...] = jnp.where(lane_t == 0, expert, jnp.where(lane_t == 1, total, 0.0)).astype(I32)


def _dest_kernel(meta_ref, offs_ref, dest_ref, *, tm):
    lane = lax.broadcasted_iota(I32, (tm, LANES), 1)
    meta = meta_ref[...]
    offs = offs_ref[0:1, :]
    dest = jnp.zeros((tm, LANES), F32)
    for kk in range(TOP_K):
        idx = jnp.sum(jnp.where(lane == META_IDX + kk, meta, 0.0), axis=-1, keepdims=True).astype(I32)
        rank = jnp.sum(jnp.where(lane == META_RANK + kk, meta, 0.0), axis=-1, keepdims=True)
        base = jnp.sum(jnp.where(lane == idx, offs, 0.0), axis=-1, keepdims=True)
        dest = jnp.where(lane == kk, base + rank, dest)
    dest_ref[...] = dest.astype(I32)


IDX_SLOTS = 3


def _idx_copy(dest_hbm, idx_smem, isem, step, n_rows):
    s = step % IDX_SLOTS
    return pltpu.make_async_copy(dest_hbm.at[pl.ds(step * n_rows, n_rows)], idx_smem.at[s], isem.at[s])


def _dispatch_kernel(dest_hbm, x_hbm, xs_in, xs_hbm, idx_smem, isem, sem, *, tm, n_steps):
    del xs_in
    i = pl.program_id(0)
    n_rows = tm * TOP_K

    @pl.when(i == 0)
    def _():
        _idx_copy(dest_hbm, idx_smem, isem, 0, n_rows).start()

    _idx_copy(dest_hbm, idx_smem, isem, i, n_rows).wait()

    @pl.when(i + 1 < n_steps)
    def _():
        _idx_copy(dest_hbm, idx_smem, isem, i + 1, n_rows).start()

    def row_copy(step, r, kk):
        d = idx_smem[step % IDX_SLOTS, r * TOP_K + kk]
        return pltpu.make_async_copy(x_hbm.at[pl.ds(step * tm + r, 1), :], xs_hbm.at[pl.ds(d, 1), :], sem)

    def issue(r, carry):
        for kk in range(TOP_K):
            row_copy(i, r, kk).start(priority=kk % 2)
        return carry

    lax.fori_loop(0, tm, issue, 0)

    def drain_step(step):
        def drain(r, carry):
            for kk in range(TOP_K):
                row_copy(step, r, kk).wait()
            return carry
        lax.fori_loop(0, tm, drain, 0)

    @pl.when(i > 0)
    def _():
        drain_step(i - 1)

    @pl.when(i == n_steps - 1)
    def _():
        drain_step(i)


def _expert_kernel(te_ref, nt_ref, xs_ref, w1_ref, b1_ref, w2_ref, b2_ref, perm_ref, ys_ref, w1b_ref, w2b_ref,
                   *, te, d_ff):
    j = pl.program_id(0)
    e = te_ref[j]
    n_tiles = nt_ref[0]
    prev = te_ref[jnp.maximum(j - 1, 0)]

    @pl.when((j == 0) | (e != prev))
    def _():
        perm = perm_ref[...]
        for cb in range(2 * d_ff // (2 * LANES)):
            sl = slice(cb * 2 * LANES, (cb + 1) * 2 * LANES)
            w1b_ref[:, sl] = _dot(w1_ref[0, 0, :, sl].astype(BF16), perm).astype(BF16)
        w2b_ref[...] = w2_ref[0, 0].astype(BF16)

    @pl.when(j < n_tiles)
    def _():
        xb = xs_ref[...].astype(BF16)
        h = _dot(xb, w1b_ref[...]) + b1_ref[0, 0]
        acts = []
        for cb in range(d_ff // LANES):
            glu = jnp.minimum(h[:, cb * 2 * LANES:cb * 2 * LANES + LANES], SWIGLU_LIMIT)
            lin = jnp.clip(h[:, cb * 2 * LANES + LANES:(cb + 1) * 2 * LANES], -SWIGLU_LIMIT, SWIGLU_LIMIT)
            sig = 1.0 / (1.0 + jnp.exp(-SWIGLU_ALPHA * glu))
            acts.append((glu * sig * (lin + 1.0)).astype(BF16))
        act = jnp.concatenate(acts, axis=-1)
        ys_ref[...] = _dot(act, w2b_ref[...]) + b2_ref[0, 0]

    @pl.when(j >= n_tiles)
    def _():
        ys_ref[...] = jnp.zeros(ys_ref.shape, F32)


def _combine_kernel(dest_hbm, meta_ref, x_ref, g_ref, b_ref, ys_hbm, xo_ref, xob_ref, idx_smem, buf_ref, isem, sem,
                    *, tm, alpha, n_steps):
    i = pl.program_id(0)
    n_rows = tm * TOP_K

    def row_copy(step, r, kk):
        d = idx_smem[step % IDX_SLOTS, r * TOP_K + kk]
        return pltpu.make_async_copy(ys_hbm.at[pl.ds(d, 1), :], buf_ref.at[step % 2, kk, pl.ds(r, 1), :],
                                     sem.at[step % 2])

    def issue_step(step):
        def issue(r, carry):
            for kk in range(TOP_K):
                row_copy(step, r, kk).start(priority=kk % 2)
            return carry
        lax.fori_loop(0, tm, issue, 0)

    @pl.when(i == 0)
    def _():
        _idx_copy(dest_hbm, idx_smem, isem, 0, n_rows).start()
        _idx_copy(dest_hbm, idx_smem, isem, 0, n_rows).wait()
        issue_step(0)

        @pl.when(n_steps > 1)
        def _():
            _idx_copy(dest_hbm, idx_smem, isem, 1, n_rows).start()

    @pl.when(i + 1 < n_steps)
    def _():
        _idx_copy(dest_hbm, idx_smem, isem, i + 1, n_rows).wait()

        @pl.when(i + 2 < n_steps)
        def _():
            _idx_copy(dest_hbm, idx_smem, isem, i + 2, n_rows).start()

        issue_step(i + 1)

    def drain(r, carry):
        for kk in range(TOP_K):
            row_copy(i, r, kk).wait()
        return carry

    lax.fori_loop(0, tm, drain, 0)

    lane = lax.broadcasted_iota(I32, (tm, LANES), 1)
    meta = meta_ref[...]
    ffn = jnp.zeros(x_ref.shape, F32)
    for kk in range(TOP_K):
        gate = jnp.sum(jnp.where(lane == META_GATE + kk, meta, 0.0), axis=-1, keepdims=True)
        ffn = ffn + gate * buf_ref[i % 2, kk]
    xn = _layernorm(alpha * x_ref[...] + ffn, g_ref[...], b_ref[...])
    xo_ref[...] = xn
    xob_ref[...] = xn.astype(BF16)


def _deinterleave_perm():
    p = np.zeros((2 * LANES, 2 * LANES), np.float32)
    j = np.arange(LANES)
    p[2 * j, j] = 1.0
    p[2 * j + 1, LANES + j] = 1.0
    return jnp.asarray(p, BF16)


def _regroup_expert_biases(b1, b2):
    n_l, n_exp, two_ff = b1.shape
    b1_r = b1.reshape(n_l, n_exp, two_ff // (2 * LANES), LANES, 2).transpose(0, 1, 2, 4, 3)
    return b1_r.reshape(n_l, n_exp, 1, two_ff), b2.reshape(n_l, n_exp, 1, b2.shape[-1])


def _moe_layer(x, w_router, b_router, w1, b1_r, w2, b2, ln_g, ln_b, *, layer, alpha):
    n, d = x.shape
    _, n_exp, _, two_ff = w1.shape
    d_ff = two_ff // 2
    te = EXPERT_TILE
    max_tiles = (n * TOP_K) // te + n_exp
    max_tiles_p = -(-max_tiles // SUBLANES) * SUBLANES
    tm = min(ROW_TILE, n)
    row = lambda i: (i, 0)

    wr = jnp.zeros((d, LANES), F32).at[:, :n_exp].set(w_router)
    wr_hi = wr.astype(BF16)
    wr_lo = (wr - wr_hi.astype(F32)).astype(BF16)
    br = jnp.full((1, LANES), NEG_INF, F32).at[0, :n_exp].set(b_router)
    meta, counts = pl.pallas_call(
        functools.partial(_router_kernel, tm=tm),
        grid=(n // tm,),
        in_specs=[pl.BlockSpec((tm, d), row), _full((d, LANES)), _full((d, LANES)), _full((1, LANES))],
        out_specs=[pl.BlockSpec((tm, LANES), row), _full((SUBLANES, LANES))],
        out_shape=[jax.ShapeDtypeStruct((n, LANES), F32), jax.ShapeDtypeStruct((SUBLANES, LANES), F32)],
        scratch_shapes=[pltpu.VMEM((SUBLANES, LANES), F32)],
        compiler_params=_cparams(),
        name="moe_router",
    )(x, wr_hi, wr_lo, br)

    offs, tiles = pl.pallas_call(
        functools.partial(_plan_kernel, te=te, max_tiles=max_tiles_p),
        out_shape=[jax.ShapeDtypeStruct((SUBLANES, LANES), F32), jax.ShapeDtypeStruct((max_tiles_p, LANES), I32)],
        name="moe_plan",
    )(counts)

    dest = pl.pallas_call(
        functools.partial(_dest_kernel, tm=tm),
        grid=(n // tm,),
        in_specs=[pl.BlockSpec((tm, LANES), row), _full((SUBLANES, LANES))],
        out_specs=pl.BlockSpec((tm, LANES), row),
        out_shape=jax.ShapeDtypeStruct((n, LANES), I32),
        compiler_params=_cparams(),
        name="moe_dest",
    )(meta, offs)
    dest_flat = dest[:, :TOP_K].reshape(n * TOP_K)
    tile_expert = tiles[:, 0]
    n_tiles = tiles[0:1, 1]

    tmv = min(MOVE_TILE, n)
    n_slots = max_tiles * te
    n_steps = n // tmv
    xs = pl.pallas_call(
        functools.partial(_dispatch_kernel, tm=tmv, n_steps=n_steps),
        grid=(n_steps,),
        in_specs=[pl.BlockSpec(memory_space=pl.ANY), pl.BlockSpec(memory_space=pl.ANY),
                  pl.BlockSpec(memory_space=pl.ANY)],
        out_specs=pl.BlockSpec(memory_space=pl.ANY),
        out_shape=jax.ShapeDtypeStruct((n_slots, d), F32),
        scratch_shapes=[pltpu.SMEM((IDX_SLOTS, tmv * TOP_K), I32), pltpu.SemaphoreType.DMA((IDX_SLOTS,)),
                        pltpu.SemaphoreType.DMA],
        input_output_aliases={2: 0},
        compiler_params=_cparams(),
        name="moe_dispatch",
    )(dest_flat, x, jnp.zeros((n_slots, d), F32))

    ys = pl.pallas_call(
        functools.partial(_expert_kernel, te=te, d_ff=d_ff),
        grid_spec=pltpu.PrefetchScalarGridSpec(
            num_scalar_prefetch=2,
            grid=(max_tiles,),
            in_specs=[pl.BlockSpec((te, d), lambda j, t, nt: (j, 0)),
                      pl.BlockSpec((1, 1, d, two_ff), lambda j, t, nt: (layer, t[j], 0, 0)),
                      pl.BlockSpec((1, 1, 1, two_ff), lambda j, t, nt: (layer, t[j], 0, 0)),
                      pl.BlockSpec((1, 1, d_ff, d), lambda j, t, nt: (layer, t[j], 0, 0)),
                      pl.BlockSpec((1, 1, 1, d), lambda j, t, nt: (layer, t[j], 0, 0)),
                      pl.BlockSpec((2 * LANES, 2 * LANES), lambda j, t, nt: (0, 0))],
            out_specs=pl.BlockSpec((te, d), lambda j, t, nt: (j, 0)),
            scratch_shapes=[pltpu.VMEM((d, two_ff), BF16), pltpu.VMEM((d_ff, d), BF16)]),
        out_shape=jax.ShapeDtypeStruct((n_slots, d), F32),
        compiler_params=_cparams(),
        name="moe_experts",
    )(tile_expert, n_tiles, xs, w1, b1_r, w2, b2, _deinterleave_perm())

    xo, xob = pl.pallas_call(
        functools.partial(_combine_kernel, tm=tmv, alpha=alpha, n_steps=n_steps),
        grid=(n_steps,),
        in_specs=[pl.BlockSpec(memory_space=pl.ANY), pl.BlockSpec((tmv, LANES), row), pl.BlockSpec((tmv, d), row),
                  _full((1, d)), _full((1, d)), pl.BlockSpec(memory_space=pl.ANY)],
        out_specs=[pl.BlockSpec((tmv, d), row), pl.BlockSpec((tmv, d), row)],
        out_shape=[jax.ShapeDtypeStruct((n, d), F32), jax.ShapeDtypeStruct((n, d), BF16)],
        scratch_shapes=[pltpu.SMEM((IDX_SLOTS, tmv * TOP_K), I32), pltpu.VMEM((2, TOP_K, tmv, d), F32),
                        pltpu.SemaphoreType.DMA((IDX_SLOTS,)), pltpu.SemaphoreType.DMA((2,))],
        compiler_params=_cparams(),
        name="moe_combine",
    )(dest_flat, meta, x, ln_g.reshape(1, d), ln_b.reshape(1, d), ys)
    return xo, xob


def kernel(x, ev_w_in, ev_conv_w, ev_sgu_ln_g, ev_sgu_ln_b, ev_sgu_w, ev_sgu_b, ev_w_out, od_w_in, od_q_norm_g, od_kv_norm_g, od_w_uq, od_w_ukv, od_f_bias, od_w_out, ln_mix_g, ln_mix_b, ln_ffn_g, ln_ffn_b, moe_w_router, moe_b_router, moe_w1, moe_b1, moe_w2, moe_b2):
    batch, seq, d = x.shape
    depth = ln_mix_g.shape[0]
    alpha = (2 * depth) ** 0.25
    xf = x.reshape(batch * seq, d)
    b1_r, b2_r = _regroup_expert_biases(moe_b1, moe_b2)
    for layer in range(depth):
        i = layer // 2
        if layer % 2 == 0:
            xf, _ = _even_layer(xf, ev_w_in[i], ev_conv_w[i], ev_sgu_ln_g[i], ev_sgu_ln_b[i], ev_sgu_w[i],
                                ev_sgu_b[i], ev_w_out[i], ln_mix_g[layer], ln_mix_b[layer], seq=seq, alpha=alpha)
        else:
            xf, _ = _odd_layer(xf, od_w_in[i], od_q_norm_g[i], od_kv_norm_g[i], od_w_uq[i], od_w_ukv[i],
                               od_f_bias[i], od_w_out[i], ln_mix_g[layer], ln_mix_b[layer],
                               batch=batch, seq=seq, alpha=alpha)
        xf, _ = _moe_layer(xf, moe_w_router[layer], moe_b_router[layer], moe_w1, b1_r, moe_w2, b2_r,
                           ln_ffn_g[layer], ln_ffn_b[layer], layer=layer, alpha=alpha)
    return xf.reshape(batch, seq, d)
```

```python
import functools
import math

import numpy as np
import jax
import jax.numpy as jnp
from jax import lax
from jax.experimental import pallas as pl
from jax.experimental.pallas import tpu as pltpu

F32 = jnp.float32
BF16 = jnp.bfloat16
I32 = jnp.int32

CHUNK = 64
CONV_WIDTH = 3
SGU_BLOCK = 128
GROUP_DIM = 64
MLA_HEADS = 8
MLA_NOPE = 64
MLA_ROPE = 32
MLA_V = 64
ROPE_THETA = 10000.0
FOX_HEADS = 8
FOX_HEAD_DIM = 64
N_EXPERTS = 32
TOP_K = 4
SWIGLU_ALPHA = 1.702
SWIGLU_LIMIT = 7.0
NORM_EPS = 1e-5
NEG_INF = -1e30

LANES = 128
SUBLANES = 8
VMEM_LIMIT = 56 * 1024 * 1024

ROW_TILE = 512
ATTN_TILE = 512
LOG2E = math.log2(math.e)
EXPERT_TILE = 256
MOVE_TILE = 256


def _cparams(n_axes=1):
    return pltpu.CompilerParams(dimension_semantics=("arbitrary",) * n_axes,
                                vmem_limit_bytes=VMEM_LIMIT)


def _dot(a, b):
    return jnp.dot(a, b, preferred_element_type=F32)


def _split2(v):
    hi = v.astype(BF16)
    lo = (v - hi.astype(F32)).astype(BF16)
    return hi, lo


def _split3(v):
    hi = v.astype(BF16)
    r1 = v - hi.astype(F32)
    mid = r1.astype(BF16)
    lo = (r1 - mid.astype(F32)).astype(BF16)
    return hi, mid, lo


def _layernorm(x, g, b):
    mu = jnp.mean(x, axis=-1, keepdims=True)
    xc = x - mu
    var = jnp.mean(xc * xc, axis=-1, keepdims=True)
    return xc * lax.rsqrt(var + NORM_EPS) * g + b


def _gelu_tanh(x):
    c = math.sqrt(2.0 / math.pi)
    return x * (0.5 * (1.0 + jnp.tanh(c * (x + 0.044715 * (x * x * x)))))


def _full(shape):
    nd = len(shape)
    return pl.BlockSpec(shape, lambda *_: (0,) * nd)


def _even_kernel(x_ref, win_ref, convw_ref, gavg_ref, lng_ref, lnb_ref, ws_ref, sb_ref, wout_ref,
                 g_ref, b_ref, xo_ref, xob_ref, hs_ref, mix_ref, *, tm, seq, alpha, a_dim):
    i = pl.program_id(0)
    x = x_ref[...]
    proj = _dot(x.astype(BF16), win_ref[...])
    a_c = proj[:, 0:a_dim]
    a_b = proj[:, a_dim:2 * a_dim]
    a_v = proj[:, 2 * a_dim:3 * a_dim]
    b_u = proj[:, 3 * a_dim:4 * a_dim]
    b_v = proj[:, 4 * a_dim:5 * a_dim]

    h = a_c * a_v

    @pl.when((i * tm) % seq == 0)
    def _():
        hs_ref[0:SUBLANES, :] = jnp.zeros((SUBLANES, a_dim), F32)

    hs_ref[SUBLANES:SUBLANES + tm, :] = h
    conv = (hs_ref[SUBLANES - 2:SUBLANES - 2 + tm, :] * convw_ref[0:1, :]
            + hs_ref[SUBLANES - 1:SUBLANES - 1 + tm, :] * convw_ref[1:2, :]
            + h * convw_ref[2:3, :])
    hs_ref[0:SUBLANES, :] = h[tm - SUBLANES:tm, :]
    mix_ref[:, 0:a_dim] = (a_b * conv).astype(BF16)

    u = _gelu_tanh(b_u)
    v = _gelu_tanh(b_v)
    gavg = gavg_ref[...]
    lane = lax.broadcasted_iota(I32, (SGU_BLOCK, LANES), 1)
    low = lane < GROUP_DIM
    for c in range(a_dim // LANES):
        vc = v[:, c * LANES:(c + 1) * LANES]
        hi, lo = _split2(vc)
        mean = _dot(hi, gavg) + _dot(lo, gavg)
        d = vc - mean
        hi, lo = _split2(d * d)
        var = _dot(hi, gavg) + _dot(lo, gavg)
        vn = d * lax.rsqrt(var + NORM_EPS) * lng_ref[:, c * LANES:(c + 1) * LANES] \
            + lnb_ref[:, c * LANES:(c + 1) * LANES]
        for blk in range(tm // SGU_BLOCK):
            vb = vn[blk * SGU_BLOCK:(blk + 1) * SGU_BLOCK, :]
            rhs = jnp.concatenate([jnp.where(low, vb, 0.0), jnp.where(low, 0.0, vb)], axis=0).astype(BF16)
            sg = _dot(ws_ref[c], rhs) + sb_ref[:, c * LANES:(c + 1) * LANES]
            ub = u[blk * SGU_BLOCK:(blk + 1) * SGU_BLOCK, c * LANES:(c + 1) * LANES]
            mix_ref[blk * SGU_BLOCK:(blk + 1) * SGU_BLOCK, a_dim + c * LANES:a_dim + (c + 1) * LANES] = \
                (ub * sg).astype(BF16)

    out = _dot(mix_ref[...], wout_ref[...])
    xn = _layernorm(alpha * x + out, g_ref[...], b_ref[...])
    xo_ref[...] = xn
    xob_ref[...] = xn.astype(BF16)


def _even_layer(x, w_in, conv_w, sgu_ln_g, sgu_ln_b, sgu_w, sgu_b, w_out, ln_g, ln_b, *, seq, alpha):
    n, d = x.shape
    a_dim = conv_w.shape[1]
    tm = min(ROW_TILE, seq)
    n_groups = sgu_w.shape[0]
    pos = np.arange(SGU_BLOCK)
    mask = (pos[None, :] // CHUNK) <= (pos[:, None] // CHUNK)
    w_s = jnp.where(mask, sgu_w, 0.0)
    ws_pairs = jnp.concatenate([w_s[0::2], w_s[1::2]], axis=2).astype(BF16)
    sb_full = jnp.repeat(sgu_b.T, GROUP_DIM, axis=1)
    convw = jnp.zeros((SUBLANES, a_dim), F32).at[0:CONV_WIDTH].set(conv_w)
    g_idx = np.arange(LANES) // GROUP_DIM
    gavg = jnp.asarray((g_idx[:, None] == g_idx[None, :]).astype(np.float32) / GROUP_DIM, BF16)
    assert n_groups * GROUP_DIM == a_dim and w_in.shape[1] == 5 * a_dim
    kern = functools.partial(_even_kernel, tm=tm, seq=seq, alpha=alpha, a_dim=a_dim)
    return pl.pallas_call(
        kern,
        grid=(n // tm,),
        in_specs=[pl.BlockSpec((tm, d), lambda i: (i, 0)),
                  _full(w_in.shape), _full(convw.shape), _full(gavg.shape),
                  _full((1, a_dim)), _full((1, a_dim)), _full(ws_pairs.shape), _full(sb_full.shape),
                  _full(w_out.shape), _full((1, d)), _full((1, d))],
        out_specs=[pl.BlockSpec((tm, d), lambda i: (i, 0)), pl.BlockSpec((tm, d), lambda i: (i, 0))],
        out_shape=[jax.ShapeDtypeStruct((n, d), F32), jax.ShapeDtypeStruct((n, d), BF16)],
        scratch_shapes=[pltpu.VMEM((tm + SUBLANES, a_dim), F32), pltpu.VMEM((tm, 2 * a_dim), BF16)],
        compiler_params=_cparams(),
        name="even_mixer",
    )(x, w_in.astype(BF16), convw, gavg, sgu_ln_g.reshape(1, a_dim), sgu_ln_b.reshape(1, a_dim),
      ws_pairs, sb_full, w_out.astype(BF16), ln_g.reshape(1, d), ln_b.reshape(1, d))


MISC_KR = 0
MISC_KR_ROT = MLA_ROPE
MISC_FZ = 2 * MLA_ROPE


def _odd_proj_kernel(x_ref, win_ref, qg_ref, kvg_ref, wq_ref, wqr_ref, wk_ref, wkr_ref, wv_ref,
                     fb_ref, ctab_ref, stab_ref, ttab_ref,
                     q_ref, k_ref, v_ref, fq_ref, fk_ref, fv_ref, fcol_ref, frow_ref, carry_ref,
                     *, tm, seq, q_rank, kv_rank, fox_dim, q_scale, fq_scale):
    i = pl.program_id(0)
    x = x_ref[...]
    proj = _dot(x.astype(BF16), win_ref[...])
    o = 0
    c_q = proj[:, o:o + q_rank]; o += q_rank
    c_kv = proj[:, o:o + kv_rank]; o += kv_rank
    misc = proj[:, o:o + LANES]; o += LANES
    fq_ref[...] = (proj[:, o:o + fox_dim] * fq_scale).astype(BF16); o += fox_dim
    fk_ref[...] = proj[:, o:o + fox_dim].astype(BF16); o += fox_dim
    fv_ref[...] = proj[:, o:o + fox_dim].astype(BF16)

    cqn = (c_q * lax.rsqrt(jnp.mean(c_q * c_q, axis=-1, keepdims=True) + NORM_EPS) * qg_ref[...]).astype(BF16)
    ckn = (c_kv * lax.rsqrt(jnp.mean(c_kv * c_kv, axis=-1, keepdims=True) + NORM_EPS) * kvg_ref[...]).astype(BF16)

    q1 = _dot(cqn, wq_ref[...])
    q2 = _dot(cqn, wqr_ref[...])
    ctab = ctab_ref[...]
    stab = stab_ref[...]
    for hh in range(MLA_HEADS):
        sl = slice(hh * LANES, (hh + 1) * LANES)
        q_ref[:, sl] = ((q1[:, sl] * ctab + q2[:, sl] * stab) * q_scale).astype(BF16)

    km_hi, km_lo = _split2(misc * ttab_ref[...])
    k_ref[...] = (_dot(ckn, wk_ref[...]) + _dot(km_hi, wkr_ref[...]) + _dot(km_lo, wkr_ref[...])).astype(BF16)
    v_ref[...] = _dot(ckn, wv_ref[...]).astype(BF16)

    lane = lax.broadcasted_iota(I32, (tm, LANES), 1)
    z = misc + fb_ref[...]
    lf = -(jnp.maximum(-z, 0.0) + jnp.log1p(jnp.exp(-jnp.abs(z))))
    lf = jnp.where((lane >= MISC_FZ) & (lane < MISC_FZ + FOX_HEADS), lf, 0.0)
    r = lax.broadcasted_iota(I32, (tm, tm), 0)
    c = lax.broadcasted_iota(I32, (tm, tm), 1)
    tri = jnp.where(c <= r, 1.0, 0.0).astype(BF16)
    hi, mid, lo = _split3(lf)
    incl = _dot(tri, hi) + _dot(tri, mid) + _dot(tri, lo)

    @pl.when((i * tm) % seq == 0)
    def _():
        carry_ref[...] = jnp.zeros((SUBLANES, LANES), F32)

    fcum = incl + carry_ref[0:1, :]
    carry_ref[...] = jnp.broadcast_to(fcum[tm - 1:tm, :], (SUBLANES, LANES))
    fsc = fcum * LOG2E
    fcol_ref[...] = fsc
    frow_ref[...] = fsc.T[MISC_FZ:MISC_FZ + FOX_HEADS, :]


def _rope_tables(seq):
    half = MLA_ROPE // 2
    inv_freq = ROPE_THETA ** (-np.arange(0, MLA_ROPE, 2, dtype=np.float64) / MLA_ROPE)
    ang = np.arange(seq, dtype=np.float64)[:, None] * inv_freq[None, :]
    cos = np.concatenate([np.cos(ang), np.cos(ang)], axis=1)
    sin = np.concatenate([np.sin(ang), np.sin(ang)], axis=1)
    assert cos.shape[1] == 2 * half
    ctab = np.zeros((seq, LANES)); ctab[:, :MLA_NOPE] = 1.0; ctab[:, MLA_NOPE:MLA_NOPE + MLA_ROPE] = cos
    stab = np.zeros((seq, LANES)); stab[:, MLA_NOPE:MLA_NOPE + MLA_ROPE] = sin
    ttab = np.zeros((seq, LANES)); ttab[:, MISC_KR:MISC_KR + MLA_ROPE] = cos
    ttab[:, MISC_KR_ROT:MISC_KR_ROT + MLA_ROPE] = sin
    return (jnp.asarray(ctab, F32), jnp.asarray(stab, F32), jnp.asarray(ttab, F32))


def _rot_cols(w):
    half = MLA_ROPE // 2
    return jnp.concatenate([-w[..., half:], w[..., :half]], axis=-1)


def _odd_proj(x, w_in, q_norm_g, kv_norm_g, w_uq, w_ukv, f_bias, *, batch, seq):
    n, d = x.shape
    q_rank, kv_rank = q_norm_g.shape[0], kv_norm_g.shape[0]
    fox_dim = FOX_HEADS * FOX_HEAD_DIM
    tm = min(ROW_TILE, seq)
    o = 0
    w_cq = w_in[:, o:o + q_rank]; o += q_rank
    w_ckv = w_in[:, o:o + kv_rank]; o += kv_rank
    w_kr = w_in[:, o:o + MLA_ROPE]; o += MLA_ROPE
    w_f = w_in[:, o:o + 3 * fox_dim]; o += 3 * fox_dim
    w_fz = w_in[:, o:o + FOX_HEADS]
    w_misc = jnp.zeros((d, LANES), F32)
    w_misc = w_misc.at[:, MISC_KR:MISC_KR + MLA_ROPE].set(w_kr)
    w_misc = w_misc.at[:, MISC_KR_ROT:MISC_KR_ROT + MLA_ROPE].set(_rot_cols(w_kr))
    w_misc = w_misc.at[:, MISC_FZ:MISC_FZ + FOX_HEADS].set(w_fz)
    w_in_p = jnp.concatenate([w_cq, w_ckv, w_misc, w_f], axis=1).astype(BF16)
    dq = MLA_NOPE + MLA_ROPE
    wq3 = w_uq.reshape(q_rank, MLA_HEADS, dq)
    wq = jnp.zeros((q_rank, MLA_HEADS, LANES), F32).at[:, :, :dq].set(wq3)
    wqr = jnp.zeros((q_rank, MLA_HEADS, LANES), F32).at[:, :, MLA_NOPE:dq].set(_rot_cols(wq3[:, :, MLA_NOPE:]))
    wkv3 = w_ukv.reshape(kv_rank, MLA_HEADS, MLA_NOPE + MLA_V)
    wk = jnp.zeros((kv_rank, MLA_HEADS, LANES), F32).at[:, :, :MLA_NOPE].set(wkv3[:, :, :MLA_NOPE])
    wv = wkv3[:, :, MLA_NOPE:].reshape(kv_rank, MLA_HEADS * MLA_V)
    place = np.zeros((LANES, MLA_HEADS, LANES), np.float32)
    for j in range(MLA_ROPE):
        place[MISC_KR + j, :, MLA_NOPE + j] = 1.0
        place[MISC_KR_ROT + j, :, MLA_NOPE + j] = 1.0
    wkr = jnp.asarray(place.reshape(LANES, MLA_HEADS * LANES), BF16)
    fb = jnp.zeros((1, LANES), F32).at[0, MISC_FZ:MISC_FZ + FOX_HEADS].set(f_bias)
    ctab, stab, ttab = _rope_tables(seq)
    hl = MLA_HEADS * LANES
    nseq = seq // tm
    row = lambda i: (i, 0)
    tab = lambda i: (i % nseq, 0)
    kern = functools.partial(_odd_proj_kernel, tm=tm, seq=seq, q_rank=q_rank, kv_rank=kv_rank, fox_dim=fox_dim,
                             q_scale=LOG2E / math.sqrt(MLA_NOPE + MLA_ROPE), fq_scale=LOG2E / math.sqrt(FOX_HEAD_DIM))
    outs = pl.pallas_call(
        kern,
        grid=(n // tm,),
        in_specs=[pl.BlockSpec((tm, d), row), _full(w_in_p.shape), _full((1, q_rank)), _full((1, kv_rank)),
                  _full((q_rank, hl)), _full((q_rank, hl)), _full((kv_rank, hl)), _full((LANES, hl)),
                  _full((kv_rank, MLA_HEADS * MLA_V)), _full((1, LANES)),
                  pl.BlockSpec((tm, LANES), tab), pl.BlockSpec((tm, LANES), tab), pl.BlockSpec((tm, LANES), tab)],
        out_specs=[pl.BlockSpec((tm, hl), row), pl.BlockSpec((tm, hl), row),
                   pl.BlockSpec((tm, MLA_HEADS * MLA_V), row),
                   pl.BlockSpec((tm, fox_dim), row), pl.BlockSpec((tm, fox_dim), row), pl.BlockSpec((tm, fox_dim), row),
                   pl.BlockSpec((tm, LANES), row),
                   pl.BlockSpec((FOX_HEADS, tm), lambda i: (i // nseq, i % nseq))],
        out_shape=[jax.ShapeDtypeStruct((n, hl), BF16), jax.ShapeDtypeStruct((n, hl), BF16),
                   jax.ShapeDtypeStruct((n, MLA_HEADS * MLA_V), BF16),
                   jax.ShapeDtypeStruct((n, fox_dim), BF16), jax.ShapeDtypeStruct((n, fox_dim), BF16),
                   jax.ShapeDtypeStruct((n, fox_dim), BF16),
                   jax.ShapeDtypeStruct((n, LANES), F32),
                   jax.ShapeDtypeStruct((batch * FOX_HEADS, seq), F32)],
        scratch_shapes=[pltpu.VMEM((SUBLANES, LANES), F32)],
        compiler_params=_cparams(),
        name="odd_proj",
    )(x, w_in_p, q_norm_g.reshape(1, q_rank), kv_norm_g.reshape(1, kv_rank),
      wq.reshape(q_rank, hl).astype(BF16), wqr.reshape(q_rank, hl).astype(BF16),
      wk.reshape(kv_rank, hl).astype(BF16), wkr, wv.astype(BF16), fb, ctab, stab, ttab)
    return outs


def _attn_kernel(*refs, tq, fox, head_lanes):
    if fox:
        q_ref, k_ref, v_ref, fcol_ref, frow_ref, o_ref = refs
    else:
        q_ref, k_ref, v_ref, o_ref = refs
    hp = pl.program_id(1)
    qi = pl.program_id(2)
    lane = lax.broadcasted_iota(I32, (tq, LANES), 1)
    qpos = qi * tq + lax.broadcasted_iota(I32, (tq, tq), 0)
    kpos = qi * tq + lax.broadcasted_iota(I32, (tq, tq), 1)
    if fox:
        allowed = kpos <= qpos
    else:
        allowed = (kpos // CHUNK) <= (qpos // CHUNK)
    qs, fqs = [], []
    for hh in range(2):
        if head_lanes == LANES:
            qs.append(q_ref[0, :, hh * LANES:(hh + 1) * LANES])
        else:
            in_head = (lane >= hh * head_lanes) & (lane < (hh + 1) * head_lanes)
            qs.append(jnp.where(in_head, q_ref[0], jnp.zeros((), BF16)))
        if fox:
            fqs.append(jnp.sum(jnp.where(lane == MISC_FZ + hp * 2 + hh, fcol_ref[0], 0.0),
                               axis=-1, keepdims=True))

    def step(j, carry, masked):
        start = pl.multiple_of(j * tq, tq)
        vt = v_ref[0, pl.ds(start, tq), :]
        new = []
        for hh in range(2):
            m, l, acc = carry[hh]
            if head_lanes == LANES:
                kt = k_ref[0, pl.ds(start, tq), hh * LANES:(hh + 1) * LANES]
            else:
                kt = k_ref[0, pl.ds(start, tq), :]
            s = lax.dot_general(qs[hh], kt, (((1,), (1,)), ((), ())), preferred_element_type=F32)
            if fox:
                s = s + (fqs[hh] - frow_ref[pl.ds(hp * 2 + hh, 1), pl.ds(start, tq)])
            if masked:
                s = jnp.where(allowed, s, NEG_INF)
            m_new = jnp.maximum(m, jnp.max(s, axis=-1, keepdims=True))
            a = jnp.exp2(m - m_new)
            p = jnp.exp2(s - m_new)
            l = a * l + jnp.sum(p, axis=-1, keepdims=True)
            acc = a * acc + _dot(p.astype(BF16), vt)
            new.append((m_new, l, acc))
        return tuple(new)

    one = (jnp.full((tq, 1), NEG_INF, F32), jnp.zeros((tq, 1), F32), jnp.zeros((tq, LANES), F32))
    carry = lax.fori_loop(0, qi, lambda j, cr: step(j, cr, False), (one, one))
    (_, l0, acc0), (_, l1, acc1) = step(qi, carry, True)
    o_ref[0] = jnp.where(lane < LANES // 2, acc0 / l0, acc1 / l1).astype(o_ref.dtype)


def _attention(q, k, v, fcol, frow, *, batch, seq, fox, head_lanes):
    tq = min(ATTN_TILE, seq)
    n_heads = v.shape[2] // MLA_V
    qk_w = 2 * head_lanes
    in_specs = [pl.BlockSpec((1, tq, qk_w), lambda b, h, i: (b, i, h)),
                pl.BlockSpec((1, seq, qk_w), lambda b, h, i: (b, 0, h)),
                pl.BlockSpec((1, seq, LANES), lambda b, h, i: (b, 0, h))]
    args = [q, k, v]
    if fox:
        in_specs += [pl.BlockSpec((1, tq, LANES), lambda b, h, i: (b, i, 0)),
                     pl.BlockSpec((FOX_HEADS, seq), lambda b, h, i: (b, 0))]
        args += [fcol, frow]
    kern = functools.partial(_attn_kernel, tq=tq, fox=fox, head_lanes=head_lanes)
    return pl.pallas_call(
        kern,
        grid=(batch, n_heads // 2, seq // tq),
        in_specs=in_specs,
        out_specs=pl.BlockSpec((1, tq, LANES), lambda b, h, i: (b, i, h)),
        out_shape=jax.ShapeDtypeStruct((batch, seq, n_heads * MLA_V), BF16),
        compiler_params=_cparams(3),
        name="fox_attention" if fox else "mla_attention",
    )(*args)


def _outproj_kernel(x_ref, ya_ref, yb_ref, w_ref, g_ref, b_ref, xo_ref, xob_ref, *, alpha):
    mix = jnp.concatenate([ya_ref[...], yb_ref[...]], axis=-1)
    out = _dot(mix, w_ref[...])
    xn = _layernorm(alpha * x_ref[...] + out, g_ref[...], b_ref[...])
    xo_ref[...] = xn
    xob_ref[...] = xn.astype(BF16)


def _outproj_ln(x, ya, yb, w_out, ln_g, ln_b, *, alpha):
    n, d = x.shape
    tm = min(ROW_TILE, n)
    row = lambda i: (i, 0)
    return pl.pallas_call(
        functools.partial(_outproj_kernel, alpha=alpha),
        grid=(n // tm,),
        in_specs=[pl.BlockSpec((tm, d), row), pl.BlockSpec((tm, ya.shape[1]), row),
                  pl.BlockSpec((tm, yb.shape[1]), row), _full(w_out.shape), _full((1, d)), _full((1, d))],
        out_specs=[pl.BlockSpec((tm, d), row), pl.BlockSpec((tm, d), row)],
        out_shape=[jax.ShapeDtypeStruct((n, d), F32), jax.ShapeDtypeStruct((n, d), BF16)],
        compiler_params=_cparams(),
        name="odd_outproj",
    )(x, ya, yb, w_out.astype(BF16), ln_g.reshape(1, d), ln_b.reshape(1, d))


def _odd_layer(x, w_in, q_norm_g, kv_norm_g, w_uq, w_ukv, f_bias, w_out, ln_g, ln_b, *, batch, seq, alpha):
    n, d = x.shape
    q, k, v, fq, fk, fv, fcol, frow = _odd_proj(x, w_in, q_norm_g, kv_norm_g, w_uq, w_ukv, f_bias,
                                                batch=batch, seq=seq)
    r3 = lambda t: t.reshape(batch, seq, t.shape[1])
    y_c = _attention(r3(q), r3(k), r3(v), None, None, batch=batch, seq=seq, fox=False, head_lanes=LANES)
    y_d = _attention(r3(fq), r3(fk), r3(fv), r3(fcol), frow, batch=batch, seq=seq, fox=True,
                     head_lanes=FOX_HEAD_DIM)
    return _outproj_ln(x, y_c.reshape(n, -1), y_d.reshape(n, -1), w_out, ln_g, ln_b, alpha=alpha)


META_IDX = 0
META_RANK = TOP_K
META_GATE = 2 * TOP_K


def _router_kernel(x_ref, whi_ref, wlo_ref, b_ref, meta_ref, cnt_ref, run_ref, *, tm):
    i = pl.program_id(0)

    @pl.when(i == 0)
    def _():
        run_ref[...] = jnp.zeros((SUBLANES, LANES), F32)

    x = x_ref[...]
    xh, xl = _split2(x)
    whi = whi_ref[...]
    logits = _dot(xh, whi) + _dot(xl, whi) + _dot(xh, wlo_ref[...]) + b_ref[...]
    lane = lax.broadcasted_iota(I32, (tm, LANES), 1)
    work = logits
    vals, sels, idxs = [], [], []
    for _ in range(TOP_K):
        m = jnp.max(work, axis=-1, keepdims=True)
        idx = jnp.min(jnp.where(work == m, lane, LANES), axis=-1, keepdims=True)
        sel = lane == idx
        vals.append(m); sels.append(sel); idxs.append(idx)
        work = jnp.where(sel, -jnp.inf, work)
    es = [jnp.exp(vk - vals[0]) for vk in vals]
    den = es[0] + es[1] + es[2] + es[3]
    chosen = jnp.where(sels[0] | sels[1] | sels[2] | sels[3], 1.0, 0.0)
    r = lax.broadcasted_iota(I32, (tm, tm), 0)
    c = lax.broadcasted_iota(I32, (tm, tm), 1)
    tri = jnp.where(c < r, 1.0, 0.0).astype(BF16)
    before = _dot(tri, chosen.astype(BF16)) + run_ref[0:1, :]
    meta = jnp.zeros((tm, LANES), F32)
    for kk in range(TOP_K):
        rank = jnp.sum(jnp.where(sels[kk], before, 0.0), axis=-1, keepdims=True)
        meta = jnp.where(lane == META_IDX + kk, idxs[kk].astype(F32), meta)
        meta = jnp.where(lane == META_RANK + kk, rank, meta)
        meta = jnp.where(lane == META_GATE + kk, es[kk] / den, meta)
    meta_ref[...] = meta
    run = run_ref[0:1, :] + jnp.sum(chosen, axis=0, keepdims=True)
    run_ref[...] = jnp.broadcast_to(run, (SUBLANES, LANES))
    cnt_ref[...] = jnp.broadcast_to(run, (SUBLANES, LANES))


def _plan_kernel(cnt_ref, offs_ref, tile_ref, *, te, max_tiles):
    lane = lax.broadcasted_iota(I32, (SUBLANES, LANES), 1)
    cnt = cnt_ref[...]
    padded = jnp.floor((cnt + (te - 1)) / te) * te
    incl = padded
    for s in (1, 2, 4, 8, 16):
        incl = incl + jnp.where(lane >= s, pltpu.roll(incl, s, axis=1), 0.0)
    offs_ref[...] = incl - padded
    ends = incl[0:1, :]
    lane_t = lax.broadcasted_iota(I32, (max_tiles, LANES), 1)
    start = (lax.broadcasted_iota(I32, (max_tiles, LANES), 0) * te).astype(F32)
    done = jnp.where((ends <= start) & (lane_t < N_EXPERTS), 1.0, 0.0)
    expert = jnp.minimum(jnp.sum(done, axis=-1, keepdims=True), N_EXPERTS - 1.0)
    total = jnp.sum(jnp.where(lane_t == N_EXPERTS - 1, ends, 0.0), axis=-1, keepdims=True) / te
    tile_ref[...] = jnp.where(lane_t == 0, expert, jnp.where(lane_t == 1, total, 0.0)).astype(I32)


def _dest_kernel(meta_ref, offs_ref, dest_ref, *, tm):
    lane = lax.broadcasted_iota(I32, (tm, LANES), 1)
    meta = meta_ref[...]
    offs = offs_ref[0:1, :]
    dest = jnp.zeros((tm, LANES), F32)
    for kk in range(TOP_K):
        idx = jnp.sum(jnp.where(lane == META_IDX + kk, meta, 0.0), axis=-1, keepdims=True).astype(I32)
        rank = jnp.sum(jnp.where(lane == META_RANK + kk, meta, 0.0), axis=-1, keepdims=True)
        base = jnp.sum(jnp.where(lane == idx, offs, 0.0), axis=-1, keepdims=True)
        dest = jnp.where(lane == kk, base + rank, dest)
    dest_ref[...] = dest.astype(I32)


def _dispatch_kernel(dest_hbm, x_ref, xs_in, xs_hbm, idx_smem, isem, sem, *, tm):
    del xs_in
    i = pl.program_id(0)
    n_rows = tm * TOP_K
    icp = pltpu.make_async_copy(dest_hbm.at[pl.ds(i * n_rows, n_rows)], idx_smem, isem)
    icp.start()
    icp.wait()

    def row_copy(r, kk):
        d = idx_smem[r * TOP_K + kk]
        return pltpu.make_async_copy(x_ref.at[pl.ds(r, 1), :], xs_hbm.at[pl.ds(d, 1), :], sem)

    def issue(r, carry):
        for kk in range(TOP_K):
            row_copy(r, kk).start()
        return carry

    lax.fori_loop(0, tm, issue, 0)

    def drain(r, carry):
        for kk in range(TOP_K):
            row_copy(r, kk).wait()
        return carry

    lax.fori_loop(0, tm, drain, 0)


def _expert_kernel(te_ref, nt_ref, xs_ref, w1_ref, b1_ref, w2_ref, b2_ref, perm_ref, ys_ref, w1b_ref, w2b_ref,
                   *, te, d_ff):
    j = pl.program_id(0)
    e = te_ref[j]
    n_tiles = nt_ref[0]
    prev = te_ref[jnp.maximum(j - 1, 0)]

    @pl.when((j == 0) | (e != prev))
    def _():
        perm = perm_ref[...]
        for cb in range(2 * d_ff // (2 * LANES)):
            sl = slice(cb * 2 * LANES, (cb + 1) * 2 * LANES)
            w1b_ref[:, sl] = _dot(w1_ref[0, 0, :, sl].astype(BF16), perm).astype(BF16)
        w2b_ref[...] = w2_ref[0, 0].astype(BF16)

    @pl.when(j < n_tiles)
    def _():
        xb = xs_ref[...].astype(BF16)
        h = _dot(xb, w1b_ref[...]) + b1_ref[0, 0]
        acts = []
        for cb in range(d_ff // LANES):
            glu = jnp.minimum(h[:, cb * 2 * LANES:cb * 2 * LANES + LANES], SWIGLU_LIMIT)
            lin = jnp.clip(h[:, cb * 2 * LANES + LANES:(cb + 1) * 2 * LANES], -SWIGLU_LIMIT, SWIGLU_LIMIT)
            sig = 1.0 / (1.0 + jnp.exp(-SWIGLU_ALPHA * glu))
            acts.append((glu * sig * (lin + 1.0)).astype(BF16))
        act = jnp.concatenate(acts, axis=-1)
        ys_ref[...] = _dot(act, w2b_ref[...]) + b2_ref[0, 0]

    @pl.when(j >= n_tiles)
    def _():
        ys_ref[...] = jnp.zeros(ys_ref.shape, F32)


def _combine_kernel(dest_hbm, meta_ref, x_ref, g_ref, b_ref, ys_hbm, xo_ref, xob_ref, idx_smem, buf_ref, isem, sem,
                    *, tm, alpha):
    i = pl.program_id(0)
    n_rows = tm * TOP_K
    icp = pltpu.make_async_copy(dest_hbm.at[pl.ds(i * n_rows, n_rows)], idx_smem, isem)
    icp.start()
    icp.wait()

    def row_copy(r, kk):
        d = idx_smem[r * TOP_K + kk]
        return pltpu.make_async_copy(ys_hbm.at[pl.ds(d, 1), :], buf_ref.at[kk, pl.ds(r, 1), :], sem)

    def issue(r, carry):
        for kk in range(TOP_K):
            row_copy(r, kk).start()
        return carry

    lax.fori_loop(0, tm, issue, 0)

    def drain(r, carry):
        for kk in range(TOP_K):
            row_copy(r, kk).wait()
        return carry

    lax.fori_loop(0, tm, drain, 0)

    lane = lax.broadcasted_iota(I32, (tm, LANES), 1)
    meta = meta_ref[...]
    ffn = jnp.zeros(x_ref.shape, F32)
    for kk in range(TOP_K):
        gate = jnp.sum(jnp.where(lane == META_GATE + kk, meta, 0.0), axis=-1, keepdims=True)
        ffn = ffn + gate * buf_ref[kk]
    xn = _layernorm(alpha * x_ref[...] + ffn, g_ref[...], b_ref[...])
    xo_ref[...] = xn
    xob_ref[...] = xn.astype(BF16)


def _deinterleave_perm():
    p = np.zeros((2 * LANES, 2 * LANES), np.float32)
    j = np.arange(LANES)
    p[2 * j, j] = 1.0
    p[2 * j + 1, LANES + j] = 1.0
    return jnp.asarray(p, BF16)


def _regroup_expert_biases(b1, b2):
    n_l, n_exp, two_ff = b1.shape
    b1_r = b1.reshape(n_l, n_exp, two_ff // (2 * LANES), LANES, 2).transpose(0, 1, 2, 4, 3)
    return b1_r.reshape(n_l, n_exp, 1, two_ff), b2.reshape(n_l, n_exp, 1, b2.shape[-1])


def _moe_layer(x, w_router, b_router, w1, b1_r, w2, b2, ln_g, ln_b, *, layer, alpha):
    n, d = x.shape
    _, n_exp, _, two_ff = w1.shape
    d_ff = two_ff // 2
    te = EXPERT_TILE
    max_tiles = (n * TOP_K) // te + n_exp
    max_tiles_p = -(-max_tiles // SUBLANES) * SUBLANES
    tm = min(ROW_TILE, n)
    row = lambda i: (i, 0)

    wr = jnp.zeros((d, LANES), F32).at[:, :n_exp].set(w_router)
    wr_hi = wr.astype(BF16)
    wr_lo = (wr - wr_hi.astype(F32)).astype(BF16)
    br = jnp.full((1, LANES), NEG_INF, F32).at[0, :n_exp].set(b_router)
    meta, counts = pl.pallas_call(
        functools.partial(_router_kernel, tm=tm),
        grid=(n // tm,),
        in_specs=[pl.BlockSpec((tm, d), row), _full((d, LANES)), _full((d, LANES)), _full((1, LANES))],
        out_specs=[pl.BlockSpec((tm, LANES), row), _full((SUBLANES, LANES))],
        out_shape=[jax.ShapeDtypeStruct((n, LANES), F32), jax.ShapeDtypeStruct((SUBLANES, LANES), F32)],
        scratch_shapes=[pltpu.VMEM((SUBLANES, LANES), F32)],
        compiler_params=_cparams(),
        name="moe_router",
    )(x, wr_hi, wr_lo, br)

    offs, tiles = pl.pallas_call(
        functools.partial(_plan_kernel, te=te, max_tiles=max_tiles_p),
        out_shape=[jax.ShapeDtypeStruct((SUBLANES, LANES), F32), jax.ShapeDtypeStruct((max_tiles_p, LANES), I32)],
        name="moe_plan",
    )(counts)

    dest = pl.pallas_call(
        functools.partial(_dest_kernel, tm=tm),
        grid=(n // tm,),
        in_specs=[pl.BlockSpec((tm, LANES), row), _full((SUBLANES, LANES))],
        out_specs=pl.BlockSpec((tm, LANES), row),
        out_shape=jax.ShapeDtypeStruct((n, LANES), I32),
        compiler_params=_cparams(),
        name="moe_dest",
    )(meta, offs)
    dest_flat = dest[:, :TOP_K].reshape(n * TOP_K)
    tile_expert = tiles[:, 0]
    n_tiles = tiles[0:1, 1]

    tmv = min(MOVE_TILE, n)
    n_slots = max_tiles * te
    n_steps = n // tmv
    xs = pl.pallas_call(
        functools.partial(_dispatch_kernel, tm=tmv),
        grid=(n_steps,),
        in_specs=[pl.BlockSpec(memory_space=pl.ANY), pl.BlockSpec((tmv, d), row),
                  pl.BlockSpec(memory_space=pl.ANY)],
        out_specs=pl.BlockSpec(memory_space=pl.ANY),
        out_shape=jax.ShapeDtypeStruct((n_slots, d), F32),
        scratch_shapes=[pltpu.SMEM((tmv * TOP_K,), I32), pltpu.SemaphoreType.DMA, pltpu.SemaphoreType.DMA],
        input_output_aliases={2: 0},
        compiler_params=_cparams(),
        name="moe_dispatch",
    )(dest_flat, x, jnp.zeros((n_slots, d), F32))

    ys = pl.pallas_call(
        functools.partial(_expert_kernel, te=te, d_ff=d_ff),
        grid_spec=pltpu.PrefetchScalarGridSpec(
            num_scalar_prefetch=2,
            grid=(max_tiles,),
            in_specs=[pl.BlockSpec((te, d), lambda j, t, nt: (j, 0)),
                      pl.BlockSpec((1, 1, d, two_ff), lambda j, t, nt: (layer, t[j], 0, 0)),
                      pl.BlockSpec((1, 1, 1, two_ff), lambda j, t, nt: (layer, t[j], 0, 0)),
                      pl.BlockSpec((1, 1, d_ff, d), lambda j, t, nt: (layer, t[j], 0, 0)),
                      pl.BlockSpec((1, 1, 1, d), lambda j, t, nt: (layer, t[j], 0, 0)),
                      pl.BlockSpec((2 * LANES, 2 * LANES), lambda j, t, nt: (0, 0))],
            out_specs=pl.BlockSpec((te, d), lambda j, t, nt: (j, 0)),
            scratch_shapes=[pltpu.VMEM((d, two_ff), BF16), pltpu.VMEM((d_ff, d), BF16)]),
        out_shape=jax.ShapeDtypeStruct((n_slots, d), F32),
        compiler_params=_cparams(),
        name="moe_experts",
    )(tile_expert, n_tiles, xs, w1, b1_r, w2, b2, _deinterleave_perm())

    xo, xob = pl.pallas_call(
        functools.partial(_combine_kernel, tm=tmv, alpha=alpha),
        grid=(n_steps,),
        in_specs=[pl.BlockSpec(memory_space=pl.ANY), pl.BlockSpec((tmv, LANES), row), pl.BlockSpec((tmv, d), row),
                  _full((1, d)), _full((1, d)), pl.BlockSpec(memory_space=pl.ANY)],
        out_specs=[pl.BlockSpec((tmv, d), row), pl.BlockSpec((tmv, d), row)],
        out_shape=[jax.ShapeDtypeStruct((n, d), F32), jax.ShapeDtypeStruct((n, d), BF16)],
        scratch_shapes=[pltpu.SMEM((tmv * TOP_K,), I32), pltpu.VMEM((TOP_K, tmv, d), F32),
                        pltpu.SemaphoreType.DMA, pltpu.SemaphoreType.DMA],
        compiler_params=_cparams(),
        name="moe_combine",
    )(dest_flat, meta, x, ln_g.reshape(1, d), ln_b.reshape(1, d), ys)
    return xo, xob


def kernel(x, ev_w_in, ev_conv_w, ev_sgu_ln_g, ev_sgu_ln_b, ev_sgu_w, ev_sgu_b, ev_w_out, od_w_in, od_q_norm_g, od_kv_norm_g, od_w_uq, od_w_ukv, od_f_bias, od_w_out, ln_mix_g, ln_mix_b, ln_ffn_g, ln_ffn_b, moe_w_router, moe_b_router, moe_w1, moe_b1, moe_w2, moe_b2):
    batch, seq, d = x.shape
    depth = ln_mix_g.shape[0]
    alpha = (2 * depth) ** 0.25
    xf = x.reshape(batch * seq, d)
    b1_r, b2_r = _regroup_expert_biases(moe_b1, moe_b2)
    for layer in range(depth):
        i = layer // 2
        if layer % 2 == 0:
            xf, _ = _even_layer(xf, ev_w_in[i], ev_conv_w[i], ev_sgu_ln_g[i], ev_sgu_ln_b[i], ev_sgu_w[i],
                                ev_sgu_b[i], ev_w_out[i], ln_mix_g[layer], ln_mix_b[layer], seq=seq, alpha=alpha)
        else:
            xf, _ = _odd_layer(xf, od_w_in[i], od_q_norm_g[i], od_kv_norm_g[i], od_w_uq[i], od_w_ukv[i],
                               od_f_bias[i], od_w_out[i], ln_mix_g[layer], ln_mix_b[layer],
                               batch=batch, seq=seq, alpha=alpha)
        xf, _ = _moe_layer(xf, moe_w_router[layer], moe_b_router[layer], moe_w1, b1_r, moe_w2, b2_r,
                           ln_ffn_g[layer], ln_ffn_b[layer], layer=layer, alpha=alpha)
    return xf.reshape(batch, seq, d)
```

```python
import functools
import math

import numpy as np
import jax
import jax.numpy as jnp
from jax import lax
from jax.experimental import pallas as pl
from jax.experimental.pallas import tpu as pltpu

F32 = jnp.float32
BF16 = jnp.bfloat16
I32 = jnp.int32

CHUNK = 64
CONV_WIDTH = 3
SGU_BLOCK = 128
GROUP_DIM = 64
MLA_HEADS = 8
MLA_NOPE = 64
MLA_ROPE = 32
MLA_V = 64
ROPE_THETA = 10000.0
FOX_HEADS = 8
FOX_HEAD_DIM = 64
N_EXPERTS = 32
TOP_K = 4
SWIGLU_ALPHA = 1.702
SWIGLU_LIMIT = 7.0
NORM_EPS = 1e-5
NEG_INF = -1e30

LANES = 128
SUBLANES = 8
ROW_SUBTILES = 8
VMEM_LIMIT = 56 * 1024 * 1024

ROW_TILE = 512
ATTN_TILE = 512
LOG2E = math.log2(math.e)
EXPERT_TILE = 256
MOVE_TILE = 256
ISSUE_UNROLL = 4


def _cparams(n_axes=1):
    return pltpu.CompilerParams(dimension_semantics=("arbitrary",) * n_axes,
                                vmem_limit_bytes=VMEM_LIMIT)


def _dot(a, b):
    return jnp.dot(a, b, preferred_element_type=F32)


def _split2(v):
    hi = v.astype(BF16)
    lo = (v - hi.astype(F32)).astype(BF16)
    return hi, lo


def _split3(v):
    hi = v.astype(BF16)
    r1 = v - hi.astype(F32)
    mid = r1.astype(BF16)
    lo = (r1 - mid.astype(F32)).astype(BF16)
    return hi, mid, lo


def _layernorm(x, g, b):
    mu = jnp.mean(x, axis=-1, keepdims=True)
    xc = x - mu
    var = jnp.mean(xc * xc, axis=-1, keepdims=True)
    return xc * lax.rsqrt(var + NORM_EPS) * g + b


def _gelu_tanh(x):
    c = math.sqrt(2.0 / math.pi)
    return x * (0.5 * (1.0 + jnp.tanh(c * (x + 0.044715 * (x * x * x)))))


def _full(shape):
    nd = len(shape)
    return pl.BlockSpec(shape, lambda *_: (0,) * nd)


def _even_kernel(x_ref, win_ref, convw_ref, gavg_ref, lng_ref, lnb_ref, ws_ref, sb_ref, wout_ref,
                 g_ref, b_ref, xo_ref, xob_ref, hs_ref, mix_ref, *, tm, seq, alpha, a_dim):
    i = pl.program_id(0)
    x = x_ref[...]
    proj = _dot(x.astype(BF16), win_ref[...])
    a_c = proj[:, 0:a_dim]
    a_b = proj[:, a_dim:2 * a_dim]
    a_v = proj[:, 2 * a_dim:3 * a_dim]
    b_u = proj[:, 3 * a_dim:4 * a_dim]
    b_v = proj[:, 4 * a_dim:5 * a_dim]

    h = a_c * a_v

    @pl.when((i * tm) % seq == 0)
    def _():
        hs_ref[0:SUBLANES, :] = jnp.zeros((SUBLANES, a_dim), F32)

    hs_ref[SUBLANES:SUBLANES + tm, :] = h
    conv = (hs_ref[SUBLANES - 2:SUBLANES - 2 + tm, :] * convw_ref[0:1, :]
            + hs_ref[SUBLANES - 1:SUBLANES - 1 + tm, :] * convw_ref[1:2, :]
            + h * convw_ref[2:3, :])
    hs_ref[0:SUBLANES, :] = h[tm - SUBLANES:tm, :]
    mix_ref[:, 0:a_dim] = (a_b * conv).astype(BF16)

    u = _gelu_tanh(b_u)
    v = _gelu_tanh(b_v)
    gavg = gavg_ref[...]
    lane = lax.broadcasted_iota(I32, (SGU_BLOCK, LANES), 1)
    low = lane < GROUP_DIM
    for c in range(a_dim // LANES):
        vc = v[:, c * LANES:(c + 1) * LANES]
        hi, lo = _split2(vc)
        mean = _dot(hi, gavg) + _dot(lo, gavg)
        d = vc - mean
        hi, lo = _split2(d * d)
        var = _dot(hi, gavg) + _dot(lo, gavg)
        vn = d * lax.rsqrt(var + NORM_EPS) * lng_ref[:, c * LANES:(c + 1) * LANES] \
            + lnb_ref[:, c * LANES:(c + 1) * LANES]
        for blk in range(tm // SGU_BLOCK):
            vb = vn[blk * SGU_BLOCK:(blk + 1) * SGU_BLOCK, :]
            rhs = jnp.concatenate([jnp.where(low, vb, 0.0), jnp.where(low, 0.0, vb)], axis=0).astype(BF16)
            sg = _dot(ws_ref[c], rhs) + sb_ref[:, c * LANES:(c + 1) * LANES]
            ub = u[blk * SGU_BLOCK:(blk + 1) * SGU_BLOCK, c * LANES:(c + 1) * LANES]
            mix_ref[blk * SGU_BLOCK:(blk + 1) * SGU_BLOCK, a_dim + c * LANES:a_dim + (c + 1) * LANES] = \
                (ub * sg).astype(BF16)

    out = _dot(mix_ref[...], wout_ref[...])
    xn = _layernorm(alpha * x + out, g_ref[...], b_ref[...])
    xo_ref[...] = xn
    xob_ref[...] = xn.astype(BF16)


def _even_layer(x, w_in, conv_w, sgu_ln_g, sgu_ln_b, sgu_w, sgu_b, w_out, ln_g, ln_b, *, seq, alpha):
    n, d = x.shape
    a_dim = conv_w.shape[1]
    tm = min(ROW_TILE, seq)
    n_groups = sgu_w.shape[0]
    pos = np.arange(SGU_BLOCK)
    mask = (pos[None, :] // CHUNK) <= (pos[:, None] // CHUNK)
    w_s = jnp.where(mask, sgu_w, 0.0)
    ws_pairs = jnp.concatenate([w_s[0::2], w_s[1::2]], axis=2).astype(BF16)
    sb_full = jnp.repeat(sgu_b.T, GROUP_DIM, axis=1)
    convw = jnp.zeros((SUBLANES, a_dim), F32).at[0:CONV_WIDTH].set(conv_w)
    g_idx = np.arange(LANES) // GROUP_DIM
    gavg = jnp.asarray((g_idx[:, None] == g_idx[None, :]).astype(np.float32) / GROUP_DIM, BF16)
    assert n_groups * GROUP_DIM == a_dim and w_in.shape[1] == 5 * a_dim
    kern = functools.partial(_even_kernel, tm=tm, seq=seq, alpha=alpha, a_dim=a_dim)
    return pl.pallas_call(
        kern,
        grid=(n // tm,),
        in_specs=[pl.BlockSpec((tm, d), lambda i: (i, 0)),
                  _full(w_in.shape), _full(convw.shape), _full(gavg.shape),
                  _full((1, a_dim)), _full((1, a_dim)), _full(ws_pairs.shape), _full(sb_full.shape),
                  _full(w_out.shape), _full((1, d)), _full((1, d))],
        out_specs=[pl.BlockSpec((tm, d), lambda i: (i, 0)), pl.BlockSpec((tm, d), lambda i: (i, 0))],
        out_shape=[jax.ShapeDtypeStruct((n, d), F32), jax.ShapeDtypeStruct((n, d), BF16)],
        scratch_shapes=[pltpu.VMEM((tm + SUBLANES, a_dim), F32), pltpu.VMEM((tm, 2 * a_dim), BF16)],
        compiler_params=_cparams(),
        name="even_mixer",
    )(x, w_in.astype(BF16), convw, gavg, sgu_ln_g.reshape(1, a_dim), sgu_ln_b.reshape(1, a_dim),
      ws_pairs, sb_full, w_out.astype(BF16), ln_g.reshape(1, d), ln_b.reshape(1, d))


MISC_KR = 0
MISC_KR_ROT = MLA_ROPE
MISC_FZ = 2 * MLA_ROPE


def _odd_proj_kernel(x_ref, win_ref, qg_ref, kvg_ref, wq_ref, wqr_ref, wk_ref, wkr_ref, wv_ref,
                     fb_ref, ctab_ref, stab_ref, ttab_ref,
                     q_ref, k_ref, v_ref, fq_ref, fk_ref, fv_ref, fcol_ref, frow_ref, carry_ref,
                     *, tm, seq, q_rank, kv_rank, fox_dim, q_scale, fq_scale):
    i = pl.program_id(0)
    x = x_ref[...]
    proj = _dot(x.astype(BF16), win_ref[...])
    o = 0
    c_q = proj[:, o:o + q_rank]; o += q_rank
    c_kv = proj[:, o:o + kv_rank]; o += kv_rank
    misc = proj[:, o:o + LANES]; o += LANES
    fq_ref[...] = (proj[:, o:o + fox_dim] * fq_scale).astype(BF16); o += fox_dim
    fk_ref[...] = proj[:, o:o + fox_dim].astype(BF16); o += fox_dim
    fv_ref[...] = proj[:, o:o + fox_dim].astype(BF16)

    cqn = (c_q * lax.rsqrt(jnp.mean(c_q * c_q, axis=-1, keepdims=True) + NORM_EPS) * qg_ref[...]).astype(BF16)
    ckn = (c_kv * lax.rsqrt(jnp.mean(c_kv * c_kv, axis=-1, keepdims=True) + NORM_EPS) * kvg_ref[...]).astype(BF16)

    q1 = _dot(cqn, wq_ref[...])
    q2 = _dot(cqn, wqr_ref[...])
    ctab = ctab_ref[...]
    stab = stab_ref[...]
    for hh in range(MLA_HEADS):
        sl = slice(hh * LANES, (hh + 1) * LANES)
        q_ref[:, sl] = ((q1[:, sl] * ctab + q2[:, sl] * stab) * q_scale).astype(BF16)

    km_hi, km_lo = _split2(misc * ttab_ref[...])
    k_ref[...] = (_dot(ckn, wk_ref[...]) + _dot(km_hi, wkr_ref[...]) + _dot(km_lo, wkr_ref[...])).astype(BF16)
    v_ref[...] = _dot(ckn, wv_ref[...]).astype(BF16)

    lane = lax.broadcasted_iota(I32, (tm, LANES), 1)
    z = misc + fb_ref[...]
    lf = -(jnp.maximum(-z, 0.0) + jnp.log1p(jnp.exp(-jnp.abs(z))))
    lf = jnp.where((lane >= MISC_FZ) & (lane < MISC_FZ + FOX_HEADS), lf, 0.0)
    r = lax.broadcasted_iota(I32, (tm, tm), 0)
    c = lax.broadcasted_iota(I32, (tm, tm), 1)
    tri = jnp.where(c <= r, 1.0, 0.0).astype(BF16)
    hi, mid, lo = _split3(lf)
    incl = _dot(tri, hi) + _dot(tri, mid) + _dot(tri, lo)

    @pl.when((i * tm) % seq == 0)
    def _():
        carry_ref[...] = jnp.zeros((SUBLANES, LANES), F32)

    fcum = incl + carry_ref[0:1, :]
    carry_ref[...] = jnp.broadcast_to(fcum[tm - 1:tm, :], (SUBLANES, LANES))
    fsc = fcum * LOG2E
    fcol_ref[...] = fsc
    frow_ref[...] = fsc.T[MISC_FZ:MISC_FZ + FOX_HEADS, :]


def _rope_tables(seq):
    half = MLA_ROPE // 2
    inv_freq = ROPE_THETA ** (-np.arange(0, MLA_ROPE, 2, dtype=np.float64) / MLA_ROPE)
    ang = np.arange(seq, dtype=np.float64)[:, None] * inv_freq[None, :]
    cos = np.concatenate([np.cos(ang), np.cos(ang)], axis=1)
    sin = np.concatenate([np.sin(ang), np.sin(ang)], axis=1)
    assert cos.shape[1] == 2 * half
    ctab = np.zeros((seq, LANES)); ctab[:, :MLA_NOPE] = 1.0; ctab[:, MLA_NOPE:MLA_NOPE + MLA_ROPE] = cos
    stab = np.zeros((seq, LANES)); stab[:, MLA_NOPE:MLA_NOPE + MLA_ROPE] = sin
    ttab = np.zeros((seq, LANES)); ttab[:, MISC_KR:MISC_KR + MLA_ROPE] = cos
    ttab[:, MISC_KR_ROT:MISC_KR_ROT + MLA_ROPE] = sin
    return (jnp.asarray(ctab, F32), jnp.asarray(stab, F32), jnp.asarray(ttab, F32))


def _rot_cols(w):
    half = MLA_ROPE // 2
    return jnp.concatenate([-w[..., half:], w[..., :half]], axis=-1)


def _odd_proj(x, w_in, q_norm_g, kv_norm_g, w_uq, w_ukv, f_bias, *, batch, seq):
    n, d = x.shape
    q_rank, kv_rank = q_norm_g.shape[0], kv_norm_g.shape[0]
    fox_dim = FOX_HEADS * FOX_HEAD_DIM
    tm = min(ROW_TILE, seq)
    o = 0
    w_cq = w_in[:, o:o + q_rank]; o += q_rank
    w_ckv = w_in[:, o:o + kv_rank]; o += kv_rank
    w_kr = w_in[:, o:o + MLA_ROPE]; o += MLA_ROPE
    w_f = w_in[:, o:o + 3 * fox_dim]; o += 3 * fox_dim
    w_fz = w_in[:, o:o + FOX_HEADS]
    w_misc = jnp.zeros((d, LANES), F32)
    w_misc = w_misc.at[:, MISC_KR:MISC_KR + MLA_ROPE].set(w_kr)
    w_misc = w_misc.at[:, MISC_KR_ROT:MISC_KR_ROT + MLA_ROPE].set(_rot_cols(w_kr))
    w_misc = w_misc.at[:, MISC_FZ:MISC_FZ + FOX_HEADS].set(w_fz)
    w_in_p = jnp.concatenate([w_cq, w_ckv, w_misc, w_f], axis=1).astype(BF16)
    dq = MLA_NOPE + MLA_ROPE
    wq3 = w_uq.reshape(q_rank, MLA_HEADS, dq)
    wq = jnp.zeros((q_rank, MLA_HEADS, LANES), F32).at[:, :, :dq].set(wq3)
    wqr = jnp.zeros((q_rank, MLA_HEADS, LANES), F32).at[:, :, MLA_NOPE:dq].set(_rot_cols(wq3[:, :, MLA_NOPE:]))
    wkv3 = w_ukv.reshape(kv_rank, MLA_HEADS, MLA_NOPE + MLA_V)
    wk = jnp.zeros((kv_rank, MLA_HEADS, LANES), F32).at[:, :, :MLA_NOPE].set(wkv3[:, :, :MLA_NOPE])
    wv = wkv3[:, :, MLA_NOPE:].reshape(kv_rank, MLA_HEADS * MLA_V)
    place = np.zeros((LANES, MLA_HEADS, LANES), np.float32)
    for j in range(MLA_ROPE):
        place[MISC_KR + j, :, MLA_NOPE + j] = 1.0
        place[MISC_KR_ROT + j, :, MLA_NOPE + j] = 1.0
    wkr = jnp.asarray(place.reshape(LANES, MLA_HEADS * LANES), BF16)
    fb = jnp.zeros((1, LANES), F32).at[0, MISC_FZ:MISC_FZ + FOX_HEADS].set(f_bias)
    ctab, stab, ttab = _rope_tables(seq)
    hl = MLA_HEADS * LANES
    nseq = seq // tm
    row = lambda i: (i, 0)
    tab = lambda i: (i % nseq, 0)
    kern = functools.partial(_odd_proj_kernel, tm=tm, seq=seq, q_rank=q_rank, kv_rank=kv_rank, fox_dim=fox_dim,
                             q_scale=LOG2E / math.sqrt(MLA_NOPE + MLA_ROPE), fq_scale=LOG2E / math.sqrt(FOX_HEAD_DIM))
    outs = pl.pallas_call(
        kern,
        grid=(n // tm,),
        in_specs=[pl.BlockSpec((tm, d), row), _full(w_in_p.shape), _full((1, q_rank)), _full((1, kv_rank)),
                  _full((q_rank, hl)), _full((q_rank, hl)), _full((kv_rank, hl)), _full((LANES, hl)),
                  _full((kv_rank, MLA_HEADS * MLA_V)), _full((1, LANES)),
                  pl.BlockSpec((tm, LANES), tab), pl.BlockSpec((tm, LANES), tab), pl.BlockSpec((tm, LANES), tab)],
        out_specs=[pl.BlockSpec((tm, hl), row), pl.BlockSpec((tm, hl), row),
                   pl.BlockSpec((tm, MLA_HEADS * MLA_V), row),
                   pl.BlockSpec((tm, fox_dim), row), pl.BlockSpec((tm, fox_dim), row), pl.BlockSpec((tm, fox_dim), row),
                   pl.BlockSpec((tm, LANES), row),
                   pl.BlockSpec((FOX_HEADS, tm), lambda i: (i // nseq, i % nseq))],
        out_shape=[jax.ShapeDtypeStruct((n, hl), BF16), jax.ShapeDtypeStruct((n, hl), BF16),
                   jax.ShapeDtypeStruct((n, MLA_HEADS * MLA_V), BF16),
                   jax.ShapeDtypeStruct((n, fox_dim), BF16), jax.ShapeDtypeStruct((n, fox_dim), BF16),
                   jax.ShapeDtypeStruct((n, fox_dim), BF16),
                   jax.ShapeDtypeStruct((n, LANES), F32),
                   jax.ShapeDtypeStruct((batch * FOX_HEADS, seq), F32)],
        scratch_shapes=[pltpu.VMEM((SUBLANES, LANES), F32)],
        compiler_params=_cparams(),
        name="odd_proj",
    )(x, w_in_p, q_norm_g.reshape(1, q_rank), kv_norm_g.reshape(1, kv_rank),
      wq.reshape(q_rank, hl).astype(BF16), wqr.reshape(q_rank, hl).astype(BF16),
      wk.reshape(kv_rank, hl).astype(BF16), wkr, wv.astype(BF16), fb, ctab, stab, ttab)
    return outs


def _attn_kernel(*refs, tq, fox, head_lanes):
    if fox:
        q_ref, k_ref, v_ref, fcol_ref, frow_ref, o_ref = refs
    else:
        q_ref, k_ref, v_ref, o_ref = refs
    hp = pl.program_id(1)
    qi = pl.program_id(2)
    lane = lax.broadcasted_iota(I32, (tq, LANES), 1)
    qpos = qi * tq + lax.broadcasted_iota(I32, (tq, tq), 0)
    kpos = qi * tq + lax.broadcasted_iota(I32, (tq, tq), 1)
    if fox:
        allowed = kpos <= qpos
    else:
        allowed = (kpos // CHUNK) <= (qpos // CHUNK)
    qs, fqs = [], []
    for hh in range(2):
        if head_lanes == LANES:
            qs.append(q_ref[0, :, hh * LANES:(hh + 1) * LANES])
        else:
            in_head = (lane >= hh * head_lanes) & (lane < (hh + 1) * head_lanes)
            qs.append(jnp.where(in_head, q_ref[0], jnp.zeros((), BF16)))
        if fox:
            fqs.append(jnp.sum(jnp.where(lane == MISC_FZ + hp * 2 + hh, fcol_ref[0], 0.0),
                               axis=-1, keepdims=True))

    def step(j, carry, masked):
        start = pl.multiple_of(j * tq, tq)
        vt = v_ref[0, pl.ds(start, tq), :]
        new = []
        for hh in range(2):
            m, l, acc = carry[hh]
            if head_lanes == LANES:
                kt = k_ref[0, pl.ds(start, tq), hh * LANES:(hh + 1) * LANES]
            else:
                kt = k_ref[0, pl.ds(start, tq), :]
            s = lax.dot_general(qs[hh], kt, (((1,), (1,)), ((), ())), preferred_element_type=F32)
            if fox:
                s = s + (fqs[hh] - frow_ref[pl.ds(hp * 2 + hh, 1), pl.ds(start, tq)])
            if masked:
                s = jnp.where(allowed, s, NEG_INF)
            m_new = jnp.maximum(m, jnp.max(s, axis=-1, keepdims=True))
            a = jnp.exp2(m - m_new)
            p = jnp.exp2(s - m_new)
            l = a * l + jnp.sum(p, axis=-1, keepdims=True)
            acc = a * acc + _dot(p.astype(BF16), vt)
            new.append((m_new, l, acc))
        return tuple(new)

    one = (jnp.full((tq, 1), NEG_INF, F32), jnp.zeros((tq, 1), F32), jnp.zeros((tq, LANES), F32))
    carry = lax.fori_loop(0, qi, lambda j, cr: step(j, cr, False), (one, one))
    (_, l0, acc0), (_, l1, acc1) = step(qi, carry, True)
    o_ref[0] = jnp.where(lane < LANES // 2, acc0 / l0, acc1 / l1).astype(o_ref.dtype)


def _attention(q, k, v, fcol, frow, *, batch, seq, fox, head_lanes):
    tq = min(ATTN_TILE, seq)
    n_heads = v.shape[2] // MLA_V
    qk_w = 2 * head_lanes
    in_specs = [pl.BlockSpec((1, tq, qk_w), lambda b, h, i: (b, i, h)),
                pl.BlockSpec((1, seq, qk_w), lambda b, h, i: (b, 0, h)),
                pl.BlockSpec((1, seq, LANES), lambda b, h, i: (b, 0, h))]
    args = [q, k, v]
    if fox:
        in_specs += [pl.BlockSpec((1, tq, LANES), lambda b, h, i: (b, i, 0)),
                     pl.BlockSpec((FOX_HEADS, seq), lambda b, h, i: (b, 0))]
        args += [fcol, frow]
    kern = functools.partial(_attn_kernel, tq=tq, fox=fox, head_lanes=head_lanes)
    return pl.pallas_call(
        kern,
        grid=(batch, n_heads // 2, seq // tq),
        in_specs=in_specs,
        out_specs=pl.BlockSpec((1, tq, LANES), lambda b, h, i: (b, i, h)),
        out_shape=jax.ShapeDtypeStruct((batch, seq, n_heads * MLA_V), BF16),
        compiler_params=_cparams(3),
        name="fox_attention" if fox else "mla_attention",
    )(*args)


def _outproj_kernel(x_ref, ya_ref, yb_ref, w_ref, g_ref, b_ref, xo_ref, xob_ref, *, alpha):
    mix = jnp.concatenate([ya_ref[...], yb_ref[...]], axis=-1)
    out = _dot(mix, w_ref[...])
    xn = _layernorm(alpha * x_ref[...] + out, g_ref[...], b_ref[...])
    xo_ref[...] = xn
    xob_ref[...] = xn.astype(BF16)


def _outproj_ln(x, ya, yb, w_out, ln_g, ln_b, *, alpha):
    n, d = x.shape
    tm = min(ROW_TILE, n)
    row = lambda i: (i, 0)
    return pl.pallas_call(
        functools.partial(_outproj_kernel, alpha=alpha),
        grid=(n // tm,),
        in_specs=[pl.BlockSpec((tm, d), row), pl.BlockSpec((tm, ya.shape[1]), row),
                  pl.BlockSpec((tm, yb.shape[1]), row), _full(w_out.shape), _full((1, d)), _full((1, d))],
        out_specs=[pl.BlockSpec((tm, d), row), pl.BlockSpec((tm, d), row)],
        out_shape=[jax.ShapeDtypeStruct((n, d), F32), jax.ShapeDtypeStruct((n, d), BF16)],
        compiler_params=_cparams(),
        name="odd_outproj",
    )(x, ya, yb, w_out.astype(BF16), ln_g.reshape(1, d), ln_b.reshape(1, d))


def _odd_layer(x, w_in, q_norm_g, kv_norm_g, w_uq, w_ukv, f_bias, w_out, ln_g, ln_b, *, batch, seq, alpha):
    n, d = x.shape
    q, k, v, fq, fk, fv, fcol, frow = _odd_proj(x, w_in, q_norm_g, kv_norm_g, w_uq, w_ukv, f_bias,
                                                batch=batch, seq=seq)
    r3 = lambda t: t.reshape(batch, seq, t.shape[1])
    y_c = _attention(r3(q), r3(k), r3(v), None, None, batch=batch, seq=seq, fox=False, head_lanes=LANES)
    y_d = _attention(r3(fq), r3(fk), r3(fv), r3(fcol), frow, batch=batch, seq=seq, fox=True,
                     head_lanes=FOX_HEAD_DIM)
    return _outproj_ln(x, y_c.reshape(n, -1), y_d.reshape(n, -1), w_out, ln_g, ln_b, alpha=alpha)


META_IDX = 0
META_RANK = TOP_K
META_GATE = 2 * TOP_K


def _router_kernel(x_ref, whi_ref, wlo_ref, b_ref, meta_ref, cnt_ref, run_ref, *, tm):
    i = pl.program_id(0)

    @pl.when(i == 0)
    def _():
        run_ref[...] = jnp.zeros((SUBLANES, LANES), F32)

    x = x_ref[...]
    xh, xl = _split2(x)
    whi = whi_ref[...]
    logits = _dot(xh, whi) + _dot(xl, whi) + _dot(xh, wlo_ref[...]) + b_ref[...]
    lane = lax.broadcasted_iota(I32, (tm, LANES), 1)
    work = logits
    vals, sels, idxs = [], [], []
    for _ in range(TOP_K):
        m = jnp.max(work, axis=-1, keepdims=True)
        idx = jnp.min(jnp.where(work == m, lane, LANES), axis=-1, keepdims=True)
        sel = lane == idx
        vals.append(m); sels.append(sel); idxs.append(idx)
        work = jnp.where(sel, -jnp.inf, work)
    es = [jnp.exp(vk - vals[0]) for vk in vals]
    den = es[0] + es[1] + es[2] + es[3]
    chosen = jnp.where(sels[0] | sels[1] | sels[2] | sels[3], 1.0, 0.0)
    r = lax.broadcasted_iota(I32, (tm, tm), 0)
    c = lax.broadcasted_iota(I32, (tm, tm), 1)
    tri = jnp.where(c < r, 1.0, 0.0).astype(BF16)
    before = _dot(tri, chosen.astype(BF16)) + run_ref[0:1, :]
    meta = jnp.zeros((tm, LANES), F32)
    for kk in range(TOP_K):
        rank = jnp.sum(jnp.where(sels[kk], before, 0.0), axis=-1, keepdims=True)
        meta = jnp.where(lane == META_IDX + kk, idxs[kk].astype(F32), meta)
        meta = jnp.where(lane == META_RANK + kk, rank, meta)
        meta = jnp.where(lane == META_GATE + kk, es[kk] / den, meta)
    meta_ref[...] = meta
    run = run_ref[0:1, :] + jnp.sum(chosen, axis=0, keepdims=True)
    run_ref[...] = jnp.broadcast_to(run, (SUBLANES, LANES))
    cnt_ref[...] = jnp.broadcast_to(run, (SUBLANES, LANES))


def _plan_kernel(cnt_ref, offs_ref, tile_ref, *, te, max_tiles):
    lane = lax.broadcasted_iota(I32, (SUBLANES, LANES), 1)
    cnt = cnt_ref[...]
    padded = jnp.floor((cnt + (te - 1)) / te) * te
    incl = padded
    for s in (1, 2, 4, 8, 16):
        incl = incl + jnp.where(lane >= s, pltpu.roll(incl, s, axis=1), 0.0)
    offs_ref[...] = incl - padded
    ends = incl[0:1, :]
    lane_t = lax.broadcasted_iota(I32, (max_tiles, LANES), 1)
    start = (lax.broadcasted_iota(I32, (max_tiles, LANES), 0) * te).astype(F32)
    done = jnp.where((ends <= start) & (lane_t < N_EXPERTS), 1.0, 0.0)
    expert = jnp.minimum(jnp.sum(done, axis=-1, keepdims=True), N_EXPERTS - 1.0)
    total = jnp.sum(jnp.where(lane_t == N_EXPERTS - 1, ends, 0.0), axis=-1, keepdims=True) / te
    tile_ref[...] = jnp.where(lane_t == 0, expert, jnp.where(lane_t == 1, total, 0.0)).astype(I32)


def _dest_kernel(meta_ref, offs_ref, dest_ref, *, tm):
    lane = lax.broadcasted_iota(I32, (tm, LANES), 1)
    meta = meta_ref[...]
    offs = offs_ref[0:1, :]
    dest = jnp.zeros((tm, LANES), F32)
    for kk in range(TOP_K):
        idx = jnp.sum(jnp.where(lane == META_IDX + kk, meta, 0.0), axis=-1, keepdims=True).astype(I32)
        rank = jnp.sum(jnp.where(lane == META_RANK + kk, meta, 0.0), axis=-1, keepdims=True)
        base = jnp.sum(jnp.where(lane == idx, offs, 0.0), axis=-1, keepdims=True)
        dest = jnp.where(lane == kk, (base + rank) * ROW_SUBTILES, dest)
    dest_ref[...] = dest.astype(I32)


def _to_token_tiles(ref, x, n_rows):
    for c in range(ROW_SUBTILES):
        ref[pl.ds(c, n_rows, stride=ROW_SUBTILES), :] = x[:, c * LANES:(c + 1) * LANES]


def _from_token_tiles(ref, n_rows, lead=()):
    return jnp.concatenate([ref[lead + (pl.ds(c, n_rows, stride=ROW_SUBTILES), slice(None))]
                            for c in range(ROW_SUBTILES)], axis=-1)


def _dispatch_kernel(dest_hbm, x_ref, xs_in, xs_hbm, idx_smem, xt_ref, isem, sem, *, tm):
    del xs_in
    i = pl.program_id(0)
    n_rows = tm * TOP_K
    icp = pltpu.make_async_copy(dest_hbm.at[pl.ds(i * n_rows, n_rows)], idx_smem, isem)
    icp.start()
    _to_token_tiles(xt_ref, x_ref[...], tm)
    icp.wait()

    def row_copy(r, kk):
        d = pl.multiple_of(idx_smem[r * TOP_K + kk], ROW_SUBTILES)
        src = pl.multiple_of(r * ROW_SUBTILES, ROW_SUBTILES)
        return pltpu.make_async_copy(xt_ref.at[pl.ds(src, ROW_SUBTILES), :],
                                     xs_hbm.at[pl.ds(d, ROW_SUBTILES), :], sem)

    def issue(r, carry):
        for kk in range(TOP_K):
            row_copy(r, kk).start()
        return carry

    lax.fori_loop(0, tm, issue, 0, unroll=ISSUE_UNROLL)

    def drain(r, carry):
        for kk in range(TOP_K):
            row_copy(r, kk).wait()
        return carry

    lax.fori_loop(0, tm, drain, 0)


def _expert_kernel(te_ref, nt_ref, xs_ref, w1_ref, b1_ref, w2_ref, b2_ref, perm_ref, ys_ref, w1b_ref, w2b_ref,
                   *, te, d_ff):
    j = pl.program_id(0)
    e = te_ref[j]
    n_tiles = nt_ref[0]
    prev = te_ref[jnp.maximum(j - 1, 0)]

    @pl.when((j == 0) | (e != prev))
    def _():
        perm = perm_ref[...]
        for cb in range(2 * d_ff // (2 * LANES)):
            sl = slice(cb * 2 * LANES, (cb + 1) * 2 * LANES)
            w1b_ref[:, sl] = _dot(w1_ref[0, 0, :, sl].astype(BF16), perm).astype(BF16)
        w2b_ref[...] = w2_ref[0, 0].astype(BF16)

    @pl.when(j < n_tiles)
    def _():
        xb = _from_token_tiles(xs_ref, te).astype(BF16)
        h = _dot(xb, w1b_ref[...]) + b1_ref[0, 0]
        acts = []
        for cb in range(d_ff // LANES):
            glu = jnp.minimum(h[:, cb * 2 * LANES:cb * 2 * LANES + LANES], SWIGLU_LIMIT)
            lin = jnp.clip(h[:, cb * 2 * LANES + LANES:(cb + 1) * 2 * LANES], -SWIGLU_LIMIT, SWIGLU_LIMIT)
            sig = 1.0 / (1.0 + jnp.exp(-SWIGLU_ALPHA * glu))
            acts.append((glu * sig * (lin + 1.0)).astype(BF16))
        act = jnp.concatenate(acts, axis=-1)
        _to_token_tiles(ys_ref, _dot(act, w2b_ref[...]) + b2_ref[0, 0], te)

    @pl.when(j >= n_tiles)
    def _():
        ys_ref[...] = jnp.zeros(ys_ref.shape, F32)


def _combine_kernel(dest_hbm, meta_ref, x_ref, g_ref, b_ref, ys_hbm, xo_ref, xob_ref, idx_smem, buf_ref, isem, sem,
                    *, tm, alpha):
    i = pl.program_id(0)
    n_rows = tm * TOP_K
    icp = pltpu.make_async_copy(dest_hbm.at[pl.ds(i * n_rows, n_rows)], idx_smem, isem)
    icp.start()
    icp.wait()

    def row_copy(r, kk):
        d = pl.multiple_of(idx_smem[r * TOP_K + kk], ROW_SUBTILES)
        dst = pl.multiple_of(r * ROW_SUBTILES, ROW_SUBTILES)
        return pltpu.make_async_copy(ys_hbm.at[pl.ds(d, ROW_SUBTILES), :],
                                     buf_ref.at[kk, pl.ds(dst, ROW_SUBTILES), :], sem)

    def issue(r, carry):
        for kk in range(TOP_K):
            row_copy(r, kk).start()
        return carry

    lax.fori_loop(0, tm, issue, 0, unroll=ISSUE_UNROLL)

    def drain(r, carry):
        for kk in range(TOP_K):
            row_copy(r, kk).wait()
        return carry

    lax.fori_loop(0, tm, drain, 0)

    lane = lax.broadcasted_iota(I32, (tm, LANES), 1)
    meta = meta_ref[...]
    ffn = jnp.zeros(x_ref.shape, F32)
    for kk in range(TOP_K):
        gate = jnp.sum(jnp.where(lane == META_GATE + kk, meta, 0.0), axis=-1, keepdims=True)
        ffn = ffn + gate * _from_token_tiles(buf_ref, tm, lead=(kk,))
    xn = _layernorm(alpha * x_ref[...] + ffn, g_ref[...], b_ref[...])
    xo_ref[...] = xn
    xob_ref[...] = xn.astype(BF16)


def _deinterleave_perm():
    p = np.zeros((2 * LANES, 2 * LANES), np.float32)
    j = np.arange(LANES)
    p[2 * j, j] = 1.0
    p[2 * j + 1, LANES + j] = 1.0
    return jnp.asarray(p, BF16)


def _regroup_expert_biases(b1, b2):
    n_l, n_exp, two_ff = b1.shape
    b1_r = b1.reshape(n_l, n_exp, two_ff // (2 * LANES), LANES, 2).transpose(0, 1, 2, 4, 3)
    return b1_r.reshape(n_l, n_exp, 1, two_ff), b2.reshape(n_l, n_exp, 1, b2.shape[-1])


def _moe_layer(x, w_router, b_router, w1, b1_r, w2, b2, ln_g, ln_b, *, layer, alpha):
    n, d = x.shape
    assert d == ROW_SUBTILES * LANES
    _, n_exp, _, two_ff = w1.shape
    d_ff = two_ff // 2
    te = EXPERT_TILE
    max_tiles = (n * TOP_K) // te + n_exp
    max_tiles_p = -(-max_tiles // SUBLANES) * SUBLANES
    tm = min(ROW_TILE, n)
    row = lambda i: (i, 0)

    wr = jnp.zeros((d, LANES), F32).at[:, :n_exp].set(w_router)
    wr_hi = wr.astype(BF16)
    wr_lo = (wr - wr_hi.astype(F32)).astype(BF16)
    br = jnp.full((1, LANES), NEG_INF, F32).at[0, :n_exp].set(b_router)
    meta, counts = pl.pallas_call(
        functools.partial(_router_kernel, tm=tm),
        grid=(n // tm,),
        in_specs=[pl.BlockSpec((tm, d), row), _full((d, LANES)), _full((d, LANES)), _full((1, LANES))],
        out_specs=[pl.BlockSpec((tm, LANES), row), _full((SUBLANES, LANES))],
        out_shape=[jax.ShapeDtypeStruct((n, LANES), F32), jax.ShapeDtypeStruct((SUBLANES, LANES), F32)],
        scratch_shapes=[pltpu.VMEM((SUBLANES, LANES), F32)],
        compiler_params=_cparams(),
        name="moe_router",
    )(x, wr_hi, wr_lo, br)

    offs, tiles = pl.pallas_call(
        functools.partial(_plan_kernel, te=te, max_tiles=max_tiles_p),
        out_shape=[jax.ShapeDtypeStruct((SUBLANES, LANES), F32), jax.ShapeDtypeStruct((max_tiles_p, LANES), I32)],
        name="moe_plan",
    )(counts)

    dest = pl.pallas_call(
        functools.partial(_dest_kernel, tm=tm),
        grid=(n // tm,),
        in_specs=[pl.BlockSpec((tm, LANES), row), _full((SUBLANES, LANES))],
        out_specs=pl.BlockSpec((tm, LANES), row),
        out_shape=jax.ShapeDtypeStruct((n, LANES), I32),
        compiler_params=_cparams(),
        name="moe_dest",
    )(meta, offs)
    dest_flat = dest[:, :TOP_K].reshape(n * TOP_K)
    tile_expert = tiles[:, 0]
    n_tiles = tiles[0:1, 1]

    tmv = min(MOVE_TILE, n)
    n_slots = max_tiles * te
    n_steps = n // tmv
    xs = pl.pallas_call(
        functools.partial(_dispatch_kernel, tm=tmv),
        grid=(n_steps,),
        in_specs=[pl.BlockSpec(memory_space=pl.ANY), pl.BlockSpec((tmv, d), row),
                  pl.BlockSpec(memory_space=pl.ANY)],
        out_specs=pl.BlockSpec(memory_space=pl.ANY),
        out_shape=jax.ShapeDtypeStruct((n_slots * ROW_SUBTILES, LANES), F32),
        scratch_shapes=[pltpu.SMEM((tmv * TOP_K,), I32), pltpu.VMEM((tmv * ROW_SUBTILES, LANES), F32),
                        pltpu.SemaphoreType.DMA, pltpu.SemaphoreType.DMA],
        input_output_aliases={2: 0},
        compiler_params=_cparams(),
        name="moe_dispatch",
    )(dest_flat, x, jnp.zeros((n_slots * ROW_SUBTILES, LANES), F32))

    ys = pl.pallas_call(
        functools.partial(_expert_kernel, te=te, d_ff=d_ff),
        grid_spec=pltpu.PrefetchScalarGridSpec(
            num_scalar_prefetch=2,
            grid=(max_tiles,),
            in_specs=[pl.BlockSpec((te * ROW_SUBTILES, LANES), lambda j, t, nt: (j, 0)),
                      pl.BlockSpec((1, 1, d, two_ff), lambda j, t, nt: (layer, t[j], 0, 0)),
                      pl.BlockSpec((1, 1, 1, two_ff), lambda j, t, nt: (layer, t[j], 0, 0)),
                      pl.BlockSpec((1, 1, d_ff, d), lambda j, t, nt: (layer, t[j], 0, 0)),
                      pl.BlockSpec((1, 1, 1, d), lambda j, t, nt: (layer, t[j], 0, 0)),
                      pl.BlockSpec((2 * LANES, 2 * LANES), lambda j, t, nt: (0, 0))],
            out_specs=pl.BlockSpec((te * ROW_SUBTILES, LANES), lambda j, t, nt: (j, 0)),
            scratch_shapes=[pltpu.VMEM((d, two_ff), BF16), pltpu.VMEM((d_ff, d), BF16)]),
        out_shape=jax.ShapeDtypeStruct((n_slots * ROW_SUBTILES, LANES), F32),
        compiler_params=_cparams(),
        name="moe_experts",
    )(tile_expert, n_tiles, xs, w1, b1_r, w2, b2, _deinterleave_perm())

    xo, xob = pl.pallas_call(
        functools.partial(_combine_kernel, tm=tmv, alpha=alpha),
        grid=(n_steps,),
        in_specs=[pl.BlockSpec(memory_space=pl.ANY), pl.BlockSpec((tmv, LANES), row), pl.BlockSpec((tmv, d), row),
                  _full((1, d)), _full((1, d)), pl.BlockSpec(memory_space=pl.ANY)],
        out_specs=[pl.BlockSpec((tmv, d), row), pl.BlockSpec((tmv, d), row)],
        out_shape=[jax.ShapeDtypeStruct((n, d), F32), jax.ShapeDtypeStruct((n, d), BF16)],
        scratch_shapes=[pltpu.SMEM((tmv * TOP_K,), I32), pltpu.VMEM((TOP_K, tmv * ROW_SUBTILES, LANES), F32),
                        pltpu.SemaphoreType.DMA, pltpu.SemaphoreType.DMA],
        compiler_params=_cparams(),
        name="moe_combine",
    )(dest_flat, meta, x, ln_g.reshape(1, d), ln_b.reshape(1, d), ys)
    return xo, xob


def kernel(x, ev_w_in, ev_conv_w, ev_sgu_ln_g, ev_sgu_ln_b, ev_sgu_w, ev_sgu_b, ev_w_out, od_w_in, od_q_norm_g, od_kv_norm_g, od_w_uq, od_w_ukv, od_f_bias, od_w_out, ln_mix_g, ln_mix_b, ln_ffn_g, ln_ffn_b, moe_w_router, moe_b_router, moe_w1, moe_b1, moe_w2, moe_b2):
    batch, seq, d = x.shape
    depth = ln_mix_g.shape[0]
    alpha = (2 * depth) ** 0.25
    xf = x.reshape(batch * seq, d)
    b1_r, b2_r = _regroup_expert_biases(moe_b1, moe_b2)
    for layer in range(depth):
        i = layer // 2
        if layer % 2 == 0:
            xf, _ = _even_layer(xf, ev_w_in[i], ev_conv_w[i], ev_sgu_ln_g[i], ev_sgu_ln_b[i], ev_sgu_w[i],
                                ev_sgu_b[i], ev_w_out[i], ln_mix_g[layer], ln_mix_b[layer], seq=seq, alpha=alpha)
        else:
            xf, _ = _odd_layer(xf, od_w_in[i], od_q_norm_g[i], od_kv_norm_g[i], od_w_uq[i], od_w_ukv[i],
                               od_f_bias[i], od_w_out[i], ln_mix_g[layer], ln_mix_b[layer],
                               batch=batch, seq=seq, alpha=alpha)
        xf, _ = _moe_layer(xf, moe_w_router[layer], moe_b_router[layer], moe_w1, b1_r, moe_w2, b2_r,
                           ln_ffn_g[layer], ln_ffn_b[layer], layer=layer, alpha=alpha)
    return xf.reshape(batch, seq, d)
```

```python
import functools
import math

import numpy as np
import jax
import jax.numpy as jnp
from jax import lax
from jax.experimental import pallas as pl
from jax.experimental.pallas import tpu as pltpu

F32 = jnp.float32
BF16 = jnp.bfloat16
I32 = jnp.int32

CHUNK = 64
CONV_WIDTH = 3
SGU_BLOCK = 128
GROUP_DIM = 64
MLA_HEADS = 8
MLA_NOPE = 64
MLA_ROPE = 32
MLA_V = 64
ROPE_THETA = 10000.0
FOX_HEADS = 8
FOX_HEAD_DIM = 64
N_EXPERTS = 32
TOP_K = 4
SWIGLU_ALPHA = 1.702
SWIGLU_LIMIT = 7.0
NORM_EPS = 1e-5
NEG_INF = -1e30

LANES = 128
SUBLANES = 8
ROW_SUBTILES = 8
VMEM_LIMIT = 56 * 1024 * 1024

ROW_TILE = 512
ATTN_TILE = 1024
LOG2E = math.log2(math.e)
EXPERT_TILE = 256
MOVE_TILE = 512
ISSUE_UNROLL = 4


def _cparams(n_axes=1):
    return pltpu.CompilerParams(dimension_semantics=("arbitrary",) * n_axes,
                                vmem_limit_bytes=VMEM_LIMIT)


def _dot(a, b):
    return jnp.dot(a, b, preferred_element_type=F32)


def _split2(v):
    hi = v.astype(BF16)
    lo = (v - hi.astype(F32)).astype(BF16)
    return hi, lo


def _split3(v):
    hi = v.astype(BF16)
    r1 = v - hi.astype(F32)
    mid = r1.astype(BF16)
    lo = (r1 - mid.astype(F32)).astype(BF16)
    return hi, mid, lo


def _layernorm(x, g, b):
    mu = jnp.mean(x, axis=-1, keepdims=True)
    xc = x - mu
    var = jnp.mean(xc * xc, axis=-1, keepdims=True)
    return xc * lax.rsqrt(var + NORM_EPS) * g + b


def _gelu_tanh(x):
    c = math.sqrt(2.0 / math.pi)
    return x * (0.5 * (1.0 + jnp.tanh(c * (x + 0.044715 * (x * x * x)))))


def _full(shape):
    nd = len(shape)
    return pl.BlockSpec(shape, lambda *_: (0,) * nd)


def _even_kernel(x_ref, win_ref, convw_ref, gavg_ref, lng_ref, lnb_ref, ws_ref, sb_ref, wout_ref,
                 g_ref, b_ref, xo_ref, hs_ref, mix_ref, *, tm, seq, alpha, a_dim):
    i = pl.program_id(0)
    x = x_ref[...]
    proj = _dot(x.astype(BF16), win_ref[...])
    a_c = proj[:, 0:a_dim]
    a_b = proj[:, a_dim:2 * a_dim]
    a_v = proj[:, 2 * a_dim:3 * a_dim]
    b_u = proj[:, 3 * a_dim:4 * a_dim]
    b_v = proj[:, 4 * a_dim:5 * a_dim]

    h = a_c * a_v

    @pl.when((i * tm) % seq == 0)
    def _():
        hs_ref[0:SUBLANES, :] = jnp.zeros((SUBLANES, a_dim), F32)

    hs_ref[SUBLANES:SUBLANES + tm, :] = h
    conv = (hs_ref[SUBLANES - 2:SUBLANES - 2 + tm, :] * convw_ref[0:1, :]
            + hs_ref[SUBLANES - 1:SUBLANES - 1 + tm, :] * convw_ref[1:2, :]
            + h * convw_ref[2:3, :])
    hs_ref[0:SUBLANES, :] = h[tm - SUBLANES:tm, :]
    mix_ref[:, 0:a_dim] = (a_b * conv).astype(BF16)

    u = _gelu_tanh(b_u)
    v = _gelu_tanh(b_v)
    gavg = gavg_ref[...]
    lane = lax.broadcasted_iota(I32, (SGU_BLOCK, LANES), 1)
    low = lane < GROUP_DIM
    for c in range(a_dim // LANES):
        vc = v[:, c * LANES:(c + 1) * LANES]
        hi, lo = _split2(vc)
        mean = _dot(hi, gavg) + _dot(lo, gavg)
        d = vc - mean
        hi, lo = _split2(d * d)
        var = _dot(hi, gavg) + _dot(lo, gavg)
        vn = d * lax.rsqrt(var + NORM_EPS) * lng_ref[:, c * LANES:(c + 1) * LANES] \
            + lnb_ref[:, c * LANES:(c + 1) * LANES]
        for blk in range(tm // SGU_BLOCK):
            vb = vn[blk * SGU_BLOCK:(blk + 1) * SGU_BLOCK, :]
            rhs = jnp.concatenate([jnp.where(low, vb, 0.0), jnp.where(low, 0.0, vb)], axis=0).astype(BF16)
            sg = _dot(ws_ref[c], rhs) + sb_ref[:, c * LANES:(c + 1) * LANES]
            ub = u[blk * SGU_BLOCK:(blk + 1) * SGU_BLOCK, c * LANES:(c + 1) * LANES]
            mix_ref[blk * SGU_BLOCK:(blk + 1) * SGU_BLOCK, a_dim + c * LANES:a_dim + (c + 1) * LANES] = \
                (ub * sg).astype(BF16)

    out = _dot(mix_ref[...], wout_ref[...])
    xn = _layernorm(alpha * x + out, g_ref[...], b_ref[...])
    xo_ref[...] = xn


def _even_layer(x, w_in, conv_w, sgu_ln_g, sgu_ln_b, sgu_w, sgu_b, w_out, ln_g, ln_b, *, seq, alpha):
    n, d = x.shape
    a_dim = conv_w.shape[1]
    tm = min(ROW_TILE, seq)
    n_groups = sgu_w.shape[0]
    pos = np.arange(SGU_BLOCK)
    mask = (pos[None, :] // CHUNK) <= (pos[:, None] // CHUNK)
    w_s = jnp.where(mask, sgu_w, 0.0)
    ws_pairs = jnp.concatenate([w_s[0::2], w_s[1::2]], axis=2).astype(BF16)
    sb_full = jnp.repeat(sgu_b.T, GROUP_DIM, axis=1)
    convw = jnp.zeros((SUBLANES, a_dim), F32).at[0:CONV_WIDTH].set(conv_w)
    g_idx = np.arange(LANES) // GROUP_DIM
    gavg = jnp.asarray((g_idx[:, None] == g_idx[None, :]).astype(np.float32) / GROUP_DIM, BF16)
    assert n_groups * GROUP_DIM == a_dim and w_in.shape[1] == 5 * a_dim
    kern = functools.partial(_even_kernel, tm=tm, seq=seq, alpha=alpha, a_dim=a_dim)
    return pl.pallas_call(
        kern,
        grid=(n // tm,),
        in_specs=[pl.BlockSpec((tm, d), lambda i: (i, 0)),
                  _full(w_in.shape), _full(convw.shape), _full(gavg.shape),
                  _full((1, a_dim)), _full((1, a_dim)), _full(ws_pairs.shape), _full(sb_full.shape),
                  _full(w_out.shape), _full((1, d)), _full((1, d))],
        out_specs=pl.BlockSpec((tm, d), lambda i: (i, 0)),
        out_shape=jax.ShapeDtypeStruct((n, d), F32),
        scratch_shapes=[pltpu.VMEM((tm + SUBLANES, a_dim), F32), pltpu.VMEM((tm, 2 * a_dim), BF16)],
        compiler_params=_cparams(),
        name="even_mixer",
    )(x, w_in.astype(BF16), convw, gavg, sgu_ln_g.reshape(1, a_dim), sgu_ln_b.reshape(1, a_dim),
      ws_pairs, sb_full, w_out.astype(BF16), ln_g.reshape(1, d), ln_b.reshape(1, d))


MISC_KR = 0
MISC_KR_ROT = MLA_ROPE
MISC_FZ = 2 * MLA_ROPE


def _odd_proj_kernel(x_ref, win_ref, qg_ref, kvg_ref, wq_ref, wqr_ref, wk_ref, wkr_ref, wv_ref,
                     fb_ref, ctab_ref, stab_ref, ttab_ref,
                     q_ref, k_ref, v_ref, fq_ref, fk_ref, fv_ref, fcol_ref, frow_ref, carry_ref,
                     *, tm, seq, q_rank, kv_rank, fox_dim, q_scale, fq_scale):
    i = pl.program_id(0)
    x = x_ref[...]
    proj = _dot(x.astype(BF16), win_ref[...])
    o = 0
    c_q = proj[:, o:o + q_rank]; o += q_rank
    c_kv = proj[:, o:o + kv_rank]; o += kv_rank
    misc = proj[:, o:o + LANES]; o += LANES
    fq_ref[...] = (proj[:, o:o + fox_dim] * fq_scale).astype(BF16); o += fox_dim
    fk_ref[...] = proj[:, o:o + fox_dim].astype(BF16); o += fox_dim
    fv_ref[...] = proj[:, o:o + fox_dim].astype(BF16)

    cqn = (c_q * lax.rsqrt(jnp.mean(c_q * c_q, axis=-1, keepdims=True) + NORM_EPS) * qg_ref[...]).astype(BF16)
    ckn = (c_kv * lax.rsqrt(jnp.mean(c_kv * c_kv, axis=-1, keepdims=True) + NORM_EPS) * kvg_ref[...]).astype(BF16)

    q1 = _dot(cqn, wq_ref[...])
    q2 = _dot(cqn, wqr_ref[...])
    ctab = ctab_ref[...]
    stab = stab_ref[...]
    for hh in range(MLA_HEADS):
        sl = slice(hh * LANES, (hh + 1) * LANES)
        q_ref[:, sl] = ((q1[:, sl] * ctab + q2[:, sl] * stab) * q_scale).astype(BF16)

    km_hi, km_lo = _split2(misc * ttab_ref[...])
    k_ref[...] = (_dot(ckn, wk_ref[...]) + _dot(km_hi, wkr_ref[...]) + _dot(km_lo, wkr_ref[...])).astype(BF16)
    v_ref[...] = _dot(ckn, wv_ref[...]).astype(BF16)

    lane = lax.broadcasted_iota(I32, (tm, LANES), 1)
    z = misc + fb_ref[...]
    lf = -(jnp.maximum(-z, 0.0) + jnp.log1p(jnp.exp(-jnp.abs(z))))
    lf = jnp.where((lane >= MISC_FZ) & (lane < MISC_FZ + FOX_HEADS), lf, 0.0)
    r = lax.broadcasted_iota(I32, (tm, tm), 0)
    c = lax.broadcasted_iota(I32, (tm, tm), 1)
    tri = jnp.where(c <= r, 1.0, 0.0).astype(BF16)
    hi, mid, lo = _split3(lf)
    incl = _dot(tri, hi) + _dot(tri, mid) + _dot(tri, lo)

    @pl.when((i * tm) % seq == 0)
    def _():
        carry_ref[...] = jnp.zeros((SUBLANES, LANES), F32)

    fcum = incl + carry_ref[0:1, :]
    carry_ref[...] = jnp.broadcast_to(fcum[tm - 1:tm, :], (SUBLANES, LANES))
    fsc = fcum * LOG2E
    fcol_ref[...] = fsc
    frow_ref[...] = fsc.T[MISC_FZ:MISC_FZ + FOX_HEADS, :]


def _rope_tables(seq):
    half = MLA_ROPE // 2
    inv_freq = ROPE_THETA ** (-np.arange(0, MLA_ROPE, 2, dtype=np.float64) / MLA_ROPE)
    ang = np.arange(seq, dtype=np.float64)[:, None] * inv_freq[None, :]
    cos = np.concatenate([np.cos(ang), np.cos(ang)], axis=1)
    sin = np.concatenate([np.sin(ang), np.sin(ang)], axis=1)
    assert cos.shape[1] == 2 * half
    ctab = np.zeros((seq, LANES)); ctab[:, :MLA_NOPE] = 1.0; ctab[:, MLA_NOPE:MLA_NOPE + MLA_ROPE] = cos
    stab = np.zeros((seq, LANES)); stab[:, MLA_NOPE:MLA_NOPE + MLA_ROPE] = sin
    ttab = np.zeros((seq, LANES)); ttab[:, MISC_KR:MISC_KR + MLA_ROPE] = cos
    ttab[:, MISC_KR_ROT:MISC_KR_ROT + MLA_ROPE] = sin
    return (jnp.asarray(ctab, F32), jnp.asarray(stab, F32), jnp.asarray(ttab, F32))


def _rot_cols(w):
    half = MLA_ROPE // 2
    return jnp.concatenate([-w[..., half:], w[..., :half]], axis=-1)


def _odd_proj(x, w_in, q_norm_g, kv_norm_g, w_uq, w_ukv, f_bias, *, batch, seq):
    n, d = x.shape
    q_rank, kv_rank = q_norm_g.shape[0], kv_norm_g.shape[0]
    fox_dim = FOX_HEADS * FOX_HEAD_DIM
    tm = min(ROW_TILE, seq)
    o = 0
    w_cq = w_in[:, o:o + q_rank]; o += q_rank
    w_ckv = w_in[:, o:o + kv_rank]; o += kv_rank
    w_kr = w_in[:, o:o + MLA_ROPE]; o += MLA_ROPE
    w_f = w_in[:, o:o + 3 * fox_dim]; o += 3 * fox_dim
    w_fz = w_in[:, o:o + FOX_HEADS]
    w_misc = jnp.zeros((d, LANES), F32)
    w_misc = w_misc.at[:, MISC_KR:MISC_KR + MLA_ROPE].set(w_kr)
    w_misc = w_misc.at[:, MISC_KR_ROT:MISC_KR_ROT + MLA_ROPE].set(_rot_cols(w_kr))
    w_misc = w_misc.at[:, MISC_FZ:MISC_FZ + FOX_HEADS].set(w_fz)
    w_in_p = jnp.concatenate([w_cq, w_ckv, w_misc, w_f], axis=1).astype(BF16)
    dq = MLA_NOPE + MLA_ROPE
    wq3 = w_uq.reshape(q_rank, MLA_HEADS, dq)
    wq = jnp.zeros((q_rank, MLA_HEADS, LANES), F32).at[:, :, :dq].set(wq3)
    wqr = jnp.zeros((q_rank, MLA_HEADS, LANES), F32).at[:, :, MLA_NOPE:dq].set(_rot_cols(wq3[:, :, MLA_NOPE:]))
    wkv3 = w_ukv.reshape(kv_rank, MLA_HEADS, MLA_NOPE + MLA_V)
    wk = jnp.zeros((kv_rank, MLA_HEADS, LANES), F32).at[:, :, :MLA_NOPE].set(wkv3[:, :, :MLA_NOPE])
    wv = wkv3[:, :, MLA_NOPE:].reshape(kv_rank, MLA_HEADS * MLA_V)
    place = np.zeros((LANES, MLA_HEADS, LANES), np.float32)
    for j in range(MLA_ROPE):
        place[MISC_KR + j, :, MLA_NOPE + j] = 1.0
        place[MISC_KR_ROT + j, :, MLA_NOPE + j] = 1.0
    wkr = jnp.asarray(place.reshape(LANES, MLA_HEADS * LANES), BF16)
    fb = jnp.zeros((1, LANES), F32).at[0, MISC_FZ:MISC_FZ + FOX_HEADS].set(f_bias)
    ctab, stab, ttab = _rope_tables(seq)
    hl = MLA_HEADS * LANES
    nseq = seq // tm
    row = lambda i: (i, 0)
    tab = lambda i: (i % nseq, 0)
    kern = functools.partial(_odd_proj_kernel, tm=tm, seq=seq, q_rank=q_rank, kv_rank=kv_rank, fox_dim=fox_dim,
                             q_scale=LOG2E / math.sqrt(MLA_NOPE + MLA_ROPE), fq_scale=LOG2E / math.sqrt(FOX_HEAD_DIM))
    outs = pl.pallas_call(
        kern,
        grid=(n // tm,),
        in_specs=[pl.BlockSpec((tm, d), row), _full(w_in_p.shape), _full((1, q_rank)), _full((1, kv_rank)),
                  _full((q_rank, hl)), _full((q_rank, hl)), _full((kv_rank, hl)), _full((LANES, hl)),
                  _full((kv_rank, MLA_HEADS * MLA_V)), _full((1, LANES)),
                  pl.BlockSpec((tm, LANES), tab), pl.BlockSpec((tm, LANES), tab), pl.BlockSpec((tm, LANES), tab)],
        out_specs=[pl.BlockSpec((tm, hl), row), pl.BlockSpec((tm, hl), row),
                   pl.BlockSpec((tm, MLA_HEADS * MLA_V), row),
                   pl.BlockSpec((tm, fox_dim), row), pl.BlockSpec((tm, fox_dim), row), pl.BlockSpec((tm, fox_dim), row),
                   pl.BlockSpec((tm, LANES), row),
                   pl.BlockSpec((FOX_HEADS, tm), lambda i: (i // nseq, i % nseq))],
        out_shape=[jax.ShapeDtypeStruct((n, hl), BF16), jax.ShapeDtypeStruct((n, hl), BF16),
                   jax.ShapeDtypeStruct((n, MLA_HEADS * MLA_V), BF16),
                   jax.ShapeDtypeStruct((n, fox_dim), BF16), jax.ShapeDtypeStruct((n, fox_dim), BF16),
                   jax.ShapeDtypeStruct((n, fox_dim), BF16),
                   jax.ShapeDtypeStruct((n, LANES), F32),
                   jax.ShapeDtypeStruct((batch * FOX_HEADS, seq), F32)],
        scratch_shapes=[pltpu.VMEM((SUBLANES, LANES), F32)],
        compiler_params=_cparams(),
        name="odd_proj",
    )(x, w_in_p, q_norm_g.reshape(1, q_rank), kv_norm_g.reshape(1, kv_rank),
      wq.reshape(q_rank, hl).astype(BF16), wqr.reshape(q_rank, hl).astype(BF16),
      wk.reshape(kv_rank, hl).astype(BF16), wkr, wv.astype(BF16), fb, ctab, stab, ttab)
    return outs


def _attn_kernel(*refs, tq, fox, head_lanes):
    if fox:
        q_ref, k_ref, v_ref, fcol_ref, frow_ref, o_ref = refs
    else:
        q_ref, k_ref, v_ref, o_ref = refs
    hp = pl.program_id(1)
    qi = pl.program_id(2)
    lane = lax.broadcasted_iota(I32, (tq, LANES), 1)
    qpos = qi * tq + lax.broadcasted_iota(I32, (tq, tq), 0)
    kpos = qi * tq + lax.broadcasted_iota(I32, (tq, tq), 1)
    if fox:
        allowed = kpos <= qpos
    else:
        allowed = (kpos // CHUNK) <= (qpos // CHUNK)
    qs, fqs = [], []
    for hh in range(2):
        if head_lanes == LANES:
            qs.append(q_ref[0, :, hh * LANES:(hh + 1) * LANES])
        else:
            in_head = (lane >= hh * head_lanes) & (lane < (hh + 1) * head_lanes)
            qs.append(jnp.where(in_head, q_ref[0], jnp.zeros((), BF16)))
        if fox:
            fqs.append(jnp.sum(jnp.where(lane == MISC_FZ + hp * 2 + hh, fcol_ref[0], 0.0),
                               axis=-1, keepdims=True))

    def step(j, carry, masked):
        start = pl.multiple_of(j * tq, tq)
        vt = v_ref[0, pl.ds(start, tq), :]
        new = []
        for hh in range(2):
            m, l, acc = carry[hh]
            if head_lanes == LANES:
                kt = k_ref[0, pl.ds(start, tq), hh * LANES:(hh + 1) * LANES]
            else:
                kt = k_ref[0, pl.ds(start, tq), :]
            s = lax.dot_general(qs[hh], kt, (((1,), (1,)), ((), ())), preferred_element_type=F32)
            if fox:
                s = s + (fqs[hh] - frow_ref[pl.ds(hp * 2 + hh, 1), pl.ds(start, tq)])
            if masked:
                s = jnp.where(allowed, s, NEG_INF)
            m_new = jnp.maximum(m, jnp.max(s, axis=-1, keepdims=True))
            a = jnp.exp2(m - m_new)
            p = jnp.exp2(s - m_new)
            l = a * l + jnp.sum(p, axis=-1, keepdims=True)
            acc = a * acc + _dot(p.astype(BF16), vt)
            new.append((m_new, l, acc))
        return tuple(new)

    one = (jnp.full((tq, 1), NEG_INF, F32), jnp.zeros((tq, 1), F32), jnp.zeros((tq, LANES), F32))
    carry = lax.fori_loop(0, qi, lambda j, cr: step(j, cr, False), (one, one))
    (_, l0, acc0), (_, l1, acc1) = step(qi, carry, True)
    o_ref[0] = jnp.where(lane < LANES // 2, acc0 / l0, acc1 / l1).astype(o_ref.dtype)


def _attention(q, k, v, fcol, frow, *, batch, seq, fox, head_lanes):
    tq = min(ATTN_TILE, seq)
    n_heads = v.shape[2] // MLA_V
    qk_w = 2 * head_lanes
    in_specs = [pl.BlockSpec((1, tq, qk_w), lambda b, h, i: (b, i, h)),
                pl.BlockSpec((1, seq, qk_w), lambda b, h, i: (b, 0, h)),
                pl.BlockSpec((1, seq, LANES), lambda b, h, i: (b, 0, h))]
    args = [q, k, v]
    if fox:
        in_specs += [pl.BlockSpec((1, tq, LANES), lambda b, h, i: (b, i, 0)),
                     pl.BlockSpec((FOX_HEADS, seq), lambda b, h, i: (b, 0))]
        args += [fcol, frow]
    kern = functools.partial(_attn_kernel, tq=tq, fox=fox, head_lanes=head_lanes)
    return pl.pallas_call(
        kern,
        grid=(batch, n_heads // 2, seq // tq),
        in_specs=in_specs,
        out_specs=pl.BlockSpec((1, tq, LANES), lambda b, h, i: (b, i, h)),
        out_shape=jax.ShapeDtypeStruct((batch, seq, n_heads * MLA_V), BF16),
        compiler_params=_cparams(3),
        name="fox_attention" if fox else "mla_attention",
    )(*args)


def _outproj_kernel(x_ref, ya_ref, yb_ref, w_ref, g_ref, b_ref, xo_ref, *, alpha):
    mix = jnp.concatenate([ya_ref[...], yb_ref[...]], axis=-1)
    out = _dot(mix, w_ref[...])
    xn = _layernorm(alpha * x_ref[...] + out, g_ref[...], b_ref[...])
    xo_ref[...] = xn


def _outproj_ln(x, ya, yb, w_out, ln_g, ln_b, *, alpha):
    n, d = x.shape
    tm = min(ROW_TILE, n)
    row = lambda i: (i, 0)
    return pl.pallas_call(
        functools.partial(_outproj_kernel, alpha=alpha),
        grid=(n // tm,),
        in_specs=[pl.BlockSpec((tm, d), row), pl.BlockSpec((tm, ya.shape[1]), row),
                  pl.BlockSpec((tm, yb.shape[1]), row), _full(w_out.shape), _full((1, d)), _full((1, d))],
        out_specs=pl.BlockSpec((tm, d), row),
        out_shape=jax.ShapeDtypeStruct((n, d), F32),
        compiler_params=_cparams(),
        name="odd_outproj",
    )(x, ya, yb, w_out.astype(BF16), ln_g.reshape(1, d), ln_b.reshape(1, d))


def _odd_layer(x, w_in, q_norm_g, kv_norm_g, w_uq, w_ukv, f_bias, w_out, ln_g, ln_b, *, batch, seq, alpha):
    n, d = x.shape
    q, k, v, fq, fk, fv, fcol, frow = _odd_proj(x, w_in, q_norm_g, kv_norm_g, w_uq, w_ukv, f_bias,
                                                batch=batch, seq=seq)
    r3 = lambda t: t.reshape(batch, seq, t.shape[1])
    y_c = _attention(r3(q), r3(k), r3(v), None, None, batch=batch, seq=seq, fox=False, head_lanes=LANES)
    y_d = _attention(r3(fq), r3(fk), r3(fv), r3(fcol), frow, batch=batch, seq=seq, fox=True,
                     head_lanes=FOX_HEAD_DIM)
    return _outproj_ln(x, y_c.reshape(n, -1), y_d.reshape(n, -1), w_out, ln_g, ln_b, alpha=alpha)


META_IDX = 0
META_RANK = TOP_K
META_GATE = 2 * TOP_K


def _router_kernel(x_ref, whi_ref, wlo_ref, b_ref, meta_ref, cnt_ref, run_ref, *, tm):
    i = pl.program_id(0)

    @pl.when(i == 0)
    def _():
        run_ref[...] = jnp.zeros((SUBLANES, LANES), F32)

    x = x_ref[...]
    xh, xl = _split2(x)
    whi = whi_ref[...]
    logits = _dot(xh, whi) + _dot(xl, whi) + _dot(xh, wlo_ref[...]) + b_ref[...]
    lane = lax.broadcasted_iota(I32, (tm, LANES), 1)
    work = logits
    vals, sels, idxs = [], [], []
    for _ in range(TOP_K):
        m = jnp.max(work, axis=-1, keepdims=True)
        idx = jnp.min(jnp.where(work == m, lane, LANES), axis=-1, keepdims=True)
        sel = lane == idx
        vals.append(m); sels.append(sel); idxs.append(idx)
        work = jnp.where(sel, -jnp.inf, work)
    es = [jnp.exp(vk - vals[0]) for vk in vals]
    den = es[0] + es[1] + es[2] + es[3]
    chosen = jnp.where(sels[0] | sels[1] | sels[2] | sels[3], 1.0, 0.0)
    r = lax.broadcasted_iota(I32, (tm, tm), 0)
    c = lax.broadcasted_iota(I32, (tm, tm), 1)
    tri = jnp.where(c < r, 1.0, 0.0).astype(BF16)
    before = _dot(tri, chosen.astype(BF16)) + run_ref[0:1, :]
    meta = jnp.zeros((tm, LANES), F32)
    for kk in range(TOP_K):
        rank = jnp.sum(jnp.where(sels[kk], before, 0.0), axis=-1, keepdims=True)
        meta = jnp.where(lane == META_IDX + kk, idxs[kk].astype(F32), meta)
        meta = jnp.where(lane == META_RANK + kk, rank, meta)
        meta = jnp.where(lane == META_GATE + kk, es[kk] / den, meta)
    meta_ref[...] = meta
    run = run_ref[0:1, :] + jnp.sum(chosen, axis=0, keepdims=True)
    run_ref[...] = jnp.broadcast_to(run, (SUBLANES, LANES))
    cnt_ref[...] = jnp.broadcast_to(run, (SUBLANES, LANES))


def _plan_kernel(cnt_ref, offs_ref, tile_ref, last_ref, *, te, max_tiles):
    lane = lax.broadcasted_iota(I32, (SUBLANES, LANES), 1)
    cnt = cnt_ref[...]
    padded = jnp.floor((cnt + (te - 1)) / te) * te
    incl = padded
    for s in (1, 2, 4, 8, 16):
        incl = incl + jnp.where(lane >= s, pltpu.roll(incl, s, axis=1), 0.0)
    offs_ref[...] = incl - padded
    last_ref[...] = jnp.where(cnt > 0.0, (incl - te) * ROW_SUBTILES, -1.0).astype(I32)
    ends = incl[0:1, :]
    lane_t = lax.broadcasted_iota(I32, (max_tiles, LANES), 1)
    start = (lax.broadcasted_iota(I32, (max_tiles, LANES), 0) * te).astype(F32)
    done = jnp.where((ends <= start) & (lane_t < N_EXPERTS), 1.0, 0.0)
    expert = jnp.minimum(jnp.sum(done, axis=-1, keepdims=True), N_EXPERTS - 1.0)
    total = jnp.sum(jnp.where(lane_t == N_EXPERTS - 1, ends, 0.0), axis=-1, keepdims=True) / te
    tile_ref[...] = jnp.where(lane_t == 0, expert, jnp.where(lane_t == 1, total, 0.0)).astype(I32)


def _dest_kernel(meta_ref, offs_ref, dest_ref, *, tm):
    lane = lax.broadcasted_iota(I32, (tm, LANES), 1)
    meta = meta_ref[...]
    offs = offs_ref[0:1, :]
    dest = jnp.zeros((tm, LANES), F32)
    for kk in range(TOP_K):
        idx = jnp.sum(jnp.where(lane == META_IDX + kk, meta, 0.0), axis=-1, keepdims=True).astype(I32)
        rank = jnp.sum(jnp.where(lane == META_RANK + kk, meta, 0.0), axis=-1, keepdims=True)
        base = jnp.sum(jnp.where(lane == idx, offs, 0.0), axis=-1, keepdims=True)
        dest = jnp.where(lane == kk, (base + rank) * ROW_SUBTILES, dest)
    dest_ref[...] = dest.astype(I32)


def _to_token_tiles(ref, x, n_rows):
    for c in range(ROW_SUBTILES):
        ref[pl.ds(c, n_rows, stride=ROW_SUBTILES), :] = x[:, c * LANES:(c + 1) * LANES]


def _from_token_tiles(ref, n_rows, lead=()):
    return jnp.concatenate([ref[lead + (pl.ds(c, n_rows, stride=ROW_SUBTILES), slice(None))]
                            for c in range(ROW_SUBTILES)], axis=-1)


def _dispatch_kernel(last_ref, dest_hbm, x_ref, xs_hbm, idx_smem, xt_ref, zero_ref, isem, sem, zsem, *, tm, te):
    i = pl.program_id(0)
    n_rows = tm * TOP_K
    icp = pltpu.make_async_copy(dest_hbm.at[pl.ds(i * n_rows, n_rows)], idx_smem, isem)
    icp.start()

    @pl.when(i == 0)
    def _():
        zero_ref[...] = jnp.zeros(zero_ref.shape, F32)

        def zero_copy(e):
            row = pl.multiple_of(last_ref[e], ROW_SUBTILES)
            return pltpu.make_async_copy(zero_ref, xs_hbm.at[pl.ds(row, te * ROW_SUBTILES), :], zsem)

        for e in range(N_EXPERTS):
            @pl.when(last_ref[e] >= 0)
            def _():
                zero_copy(e).start()
        for e in range(N_EXPERTS):
            @pl.when(last_ref[e] >= 0)
            def _():
                zero_copy(e).wait()

    _to_token_tiles(xt_ref, x_ref[...], tm)
    icp.wait()

    def row_copy(r, kk):
        d = pl.multiple_of(idx_smem[r * TOP_K + kk], ROW_SUBTILES)
        src = pl.multiple_of(r * ROW_SUBTILES, ROW_SUBTILES)
        return pltpu.make_async_copy(xt_ref.at[pl.ds(src, ROW_SUBTILES), :],
                                     xs_hbm.at[pl.ds(d, ROW_SUBTILES), :], sem)

    def issue(r, carry):
        for kk in range(TOP_K):
            row_copy(r, kk).start(priority=kk % 2)
        return carry

    lax.fori_loop(0, tm, issue, 0, unroll=ISSUE_UNROLL)

    def drain(r, carry):
        for kk in range(TOP_K):
            row_copy(r, kk).wait()
        return carry

    lax.fori_loop(0, tm, drain, 0)


def _expert_kernel(te_ref, nt_ref, xs_ref, w1_ref, b1_ref, w2_ref, b2_ref, perm_ref, ys_ref, w1b_ref, w2b_ref,
                   *, te, d_ff):
    j = pl.program_id(0)
    e = te_ref[j]
    n_tiles = nt_ref[0]
    prev = te_ref[jnp.maximum(j - 1, 0)]

    @pl.when((j == 0) | (e != prev))
    def _():
        perm = perm_ref[...]
        for cb in range(2 * d_ff // (2 * LANES)):
            sl = slice(cb * 2 * LANES, (cb + 1) * 2 * LANES)
            w1b_ref[:, sl] = _dot(w1_ref[0, 0, :, sl].astype(BF16), perm).astype(BF16)
        w2b_ref[...] = w2_ref[0, 0].astype(BF16)

    @pl.when(j < n_tiles)
    def _():
        xb = _from_token_tiles(xs_ref, te).astype(BF16)
        h = _dot(xb, w1b_ref[...]) + b1_ref[0, 0]
        acts = []
        for cb in range(d_ff // LANES):
            glu = jnp.minimum(h[:, cb * 2 * LANES:cb * 2 * LANES + LANES], SWIGLU_LIMIT)
            lin = jnp.clip(h[:, cb * 2 * LANES + LANES:(cb + 1) * 2 * LANES], -SWIGLU_LIMIT, SWIGLU_LIMIT)
            sig = 1.0 / (1.0 + jnp.exp(-SWIGLU_ALPHA * glu))
            acts.append((glu * sig * (lin + 1.0)).astype(BF16))
        act = jnp.concatenate(acts, axis=-1)
        _to_token_tiles(ys_ref, _dot(act, w2b_ref[...]) + b2_ref[0, 0], te)

    @pl.when(j >= n_tiles)
    def _():
        ys_ref[...] = jnp.zeros(ys_ref.shape, F32)


def _combine_kernel(dest_hbm, meta_ref, x_ref, g_ref, b_ref, ys_hbm, xo_ref, idx_smem, buf_ref, isem, sem,
                    *, tm, alpha):
    i = pl.program_id(0)
    n_rows = tm * TOP_K
    icp = pltpu.make_async_copy(dest_hbm.at[pl.ds(i * n_rows, n_rows)], idx_smem, isem)
    icp.start()
    icp.wait()

    def row_copy(r, kk):
        d = pl.multiple_of(idx_smem[r * TOP_K + kk], ROW_SUBTILES)
        dst = pl.multiple_of(r * ROW_SUBTILES, ROW_SUBTILES)
        return pltpu.make_async_copy(ys_hbm.at[pl.ds(d, ROW_SUBTILES), :],
                                     buf_ref.at[kk, pl.ds(dst, ROW_SUBTILES), :], sem)

    def issue(r, carry):
        for kk in range(TOP_K):
            row_copy(r, kk).start()
        return carry

    lax.fori_loop(0, tm, issue, 0, unroll=ISSUE_UNROLL)

    def drain(r, carry):
        for kk in range(TOP_K):
            row_copy(r, kk).wait()
        return carry

    lax.fori_loop(0, tm, drain, 0)

    lane = lax.broadcasted_iota(I32, (tm, LANES), 1)
    meta = meta_ref[...]
    ffn = jnp.zeros(x_ref.shape, F32)
    for kk in range(TOP_K):
        gate = jnp.sum(jnp.where(lane == META_GATE + kk, meta, 0.0), axis=-1, keepdims=True)
        ffn = ffn + gate * _from_token_tiles(buf_ref, tm, lead=(kk,))
    xn = _layernorm(alpha * x_ref[...] + ffn, g_ref[...], b_ref[...])
    xo_ref[...] = xn


def _deinterleave_perm():
    p = np.zeros((2 * LANES, 2 * LANES), np.float32)
    j = np.arange(LANES)
    p[2 * j, j] = 1.0
    p[2 * j + 1, LANES + j] = 1.0
    return jnp.asarray(p, BF16)


def _regroup_expert_biases(b1, b2):
    n_l, n_exp, two_ff = b1.shape
    b1_r = b1.reshape(n_l, n_exp, two_ff // (2 * LANES), LANES, 2).transpose(0, 1, 2, 4, 3)
    return b1_r.reshape(n_l, n_exp, 1, two_ff), b2.reshape(n_l, n_exp, 1, b2.shape[-1])


def _moe_layer(x, w_router, b_router, w1, b1_r, w2, b2, ln_g, ln_b, *, layer, alpha):
    n, d = x.shape
    assert d == ROW_SUBTILES * LANES
    _, n_exp, _, two_ff = w1.shape
    d_ff = two_ff // 2
    te = EXPERT_TILE
    max_tiles = (n * TOP_K) // te + n_exp
    max_tiles_p = -(-max_tiles // SUBLANES) * SUBLANES
    tm = min(ROW_TILE, n)
    row = lambda i: (i, 0)

    wr = jnp.zeros((d, LANES), F32).at[:, :n_exp].set(w_router)
    wr_hi = wr.astype(BF16)
    wr_lo = (wr - wr_hi.astype(F32)).astype(BF16)
    br = jnp.full((1, LANES), NEG_INF, F32).at[0, :n_exp].set(b_router)
    meta, counts = pl.pallas_call(
        functools.partial(_router_kernel, tm=tm),
        grid=(n // tm,),
        in_specs=[pl.BlockSpec((tm, d), row), _full((d, LANES)), _full((d, LANES)), _full((1, LANES))],
        out_specs=[pl.BlockSpec((tm, LANES), row), _full((SUBLANES, LANES))],
        out_shape=[jax.ShapeDtypeStruct((n, LANES), F32), jax.ShapeDtypeStruct((SUBLANES, LANES), F32)],
        scratch_shapes=[pltpu.VMEM((SUBLANES, LANES), F32)],
        compiler_params=_cparams(),
        name="moe_router",
    )(x, wr_hi, wr_lo, br)

    offs, tiles, last_rows = pl.pallas_call(
        functools.partial(_plan_kernel, te=te, max_tiles=max_tiles_p),
        out_shape=[jax.ShapeDtypeStruct((SUBLANES, LANES), F32), jax.ShapeDtypeStruct((max_tiles_p, LANES), I32),
                   jax.ShapeDtypeStruct((SUBLANES, LANES), I32)],
        name="moe_plan",
    )(counts)

    dest = pl.pallas_call(
        functools.partial(_dest_kernel, tm=tm),
        grid=(n // tm,),
        in_specs=[pl.BlockSpec((tm, LANES), row), _full((SUBLANES, LANES))],
        out_specs=pl.BlockSpec((tm, LANES), row),
        out_shape=jax.ShapeDtypeStruct((n, LANES), I32),
        compiler_params=_cparams(),
        name="moe_dest",
    )(meta, offs)
    dest_flat = dest[:, :TOP_K].reshape(n * TOP_K)
    tile_expert = tiles[:, 0]
    n_tiles = tiles[0:1, 1]

    tmv = min(MOVE_TILE, n)
    n_slots = max_tiles * te
    n_steps = n // tmv
    xs = pl.pallas_call(
        functools.partial(_dispatch_kernel, tm=tmv, te=te),
        grid_spec=pltpu.PrefetchScalarGridSpec(
            num_scalar_prefetch=1,
            grid=(n_steps,),
            in_specs=[pl.BlockSpec(memory_space=pl.ANY), pl.BlockSpec((tmv, d), lambda i, last: (i, 0))],
            out_specs=pl.BlockSpec(memory_space=pl.ANY),
            scratch_shapes=[pltpu.SMEM((tmv * TOP_K,), I32), pltpu.VMEM((tmv * ROW_SUBTILES, LANES), F32),
                            pltpu.VMEM((te * ROW_SUBTILES, LANES), F32),
                            pltpu.SemaphoreType.DMA, pltpu.SemaphoreType.DMA, pltpu.SemaphoreType.DMA]),
        out_shape=jax.ShapeDtypeStruct((n_slots * ROW_SUBTILES, LANES), F32),
        compiler_params=_cparams(),
        name="moe_dispatch",
    )(last_rows[0, :n_exp], dest_flat, x)

    ys = pl.pallas_call(
        functools.partial(_expert_kernel, te=te, d_ff=d_ff),
        grid_spec=pltpu.PrefetchScalarGridSpec(
            num_scalar_prefetch=2,
            grid=(max_tiles,),
            in_specs=[pl.BlockSpec((te * ROW_SUBTILES, LANES), lambda j, t, nt: (jnp.minimum(j, nt[0] - 1), 0)),
                      pl.BlockSpec((1, 1, d, two_ff), lambda j, t, nt: (layer, t[j], 0, 0)),
                      pl.BlockSpec((1, 1, 1, two_ff), lambda j, t, nt: (layer, t[j], 0, 0)),
                      pl.BlockSpec((1, 1, d_ff, d), lambda j, t, nt: (layer, t[j], 0, 0)),
                      pl.BlockSpec((1, 1, 1, d), lambda j, t, nt: (layer, t[j], 0, 0)),
                      pl.BlockSpec((2 * LANES, 2 * LANES), lambda j, t, nt: (0, 0))],
            out_specs=pl.BlockSpec((te * ROW_SUBTILES, LANES), lambda j, t, nt: (j, 0)),
            scratch_shapes=[pltpu.VMEM((d, two_ff), BF16), pltpu.VMEM((d_ff, d), BF16)]),
        out_shape=jax.ShapeDtypeStruct((n_slots * ROW_SUBTILES, LANES), F32),
        compiler_params=_cparams(),
        name="moe_experts",
    )(tile_expert, n_tiles, xs, w1, b1_r, w2, b2, _deinterleave_perm())

    return pl.pallas_call(
        functools.partial(_combine_kernel, tm=tmv, alpha=alpha),
        grid=(n_steps,),
        in_specs=[pl.BlockSpec(memory_space=pl.ANY), pl.BlockSpec((tmv, LANES), row), pl.BlockSpec((tmv, d), row),
                  _full((1, d)), _full((1, d)), pl.BlockSpec(memory_space=pl.ANY)],
        out_specs=pl.BlockSpec((tmv, d), row),
        out_shape=jax.ShapeDtypeStruct((n, d), F32),
        scratch_shapes=[pltpu.SMEM((tmv * TOP_K,), I32), pltpu.VMEM((TOP_K, tmv * ROW_SUBTILES, LANES), F32),
                        pltpu.SemaphoreType.DMA, pltpu.SemaphoreType.DMA],
        compiler_params=_cparams(),
        name="moe_combine",
    )(dest_flat, meta, x, ln_g.reshape(1, d), ln_b.reshape(1, d), ys)


def kernel(x, ev_w_in, ev_conv_w, ev_sgu_ln_g, ev_sgu_ln_b, ev_sgu_w, ev_sgu_b, ev_w_out, od_w_in, od_q_norm_g, od_kv_norm_g, od_w_uq, od_w_ukv, od_f_bias, od_w_out, ln_mix_g, ln_mix_b, ln_ffn_g, ln_ffn_b, moe_w_router, moe_b_router, moe_w1, moe_b1, moe_w2, moe_b2):
    batch, seq, d = x.shape
    depth = ln_mix_g.shape[0]
    alpha = (2 * depth) ** 0.25
    xf = x.reshape(batch * seq, d)
    b1_r, b2_r = _regroup_expert_biases(moe_b1, moe_b2)
    for layer in range(depth):
        i = layer // 2
        if layer % 2 == 0:
            xf = _even_layer(xf, ev_w_in[i], ev_conv_w[i], ev_sgu_ln_g[i], ev_sgu_ln_b[i], ev_sgu_w[i],
                                ev_sgu_b[i], ev_w_out[i], ln_mix_g[layer], ln_mix_b[layer], seq=seq, alpha=alpha)
        else:
            xf = _odd_layer(xf, od_w_in[i], od_q_norm_g[i], od_kv_norm_g[i], od_w_uq[i], od_w_ukv[i],
                               od_f_bias[i], od_w_out[i], ln_mix_g[layer], ln_mix_b[layer],
                               batch=batch, seq=seq, alpha=alpha)
        xf = _moe_layer(xf, moe_w_router[layer], moe_b_router[layer], moe_w1, b1_r, moe_w2, b2_r,
                           ln_ffn_g[layer], ln_ffn_b[layer], layer=layer, alpha=alpha)
    return xf.reshape(batch, seq, d)
```

```python
import functools
import math

import numpy as np
import jax
import jax.numpy as jnp
from jax import lax
from jax.experimental import pallas as pl
from jax.experimental.pallas import tpu as pltpu

F32 = jnp.float32
BF16 = jnp.bfloat16
I32 = jnp.int32

CHUNK = 64
CONV_WIDTH = 3
SGU_BLOCK = 128
GROUP_DIM = 64
MLA_HEADS = 8
MLA_NOPE = 64
MLA_ROPE = 32
MLA_V = 64
ROPE_THETA = 10000.0
FOX_HEADS = 8
FOX_HEAD_DIM = 64
N_EXPERTS = 32
TOP_K = 4
SWIGLU_ALPHA = 1.702
SWIGLU_LIMIT = 7.0
NORM_EPS = 1e-5
NEG_INF = -1e30

LANES = 128
SUBLANES = 8
ROW_SUBTILES = 8
VMEM_LIMIT = 56 * 1024 * 1024

ROW_TILE = 512
ATTN_TILE = 1024
LOG2E = math.log2(math.e)
EXPERT_TILE = 512
MOVE_TILE = 512
ISSUE_UNROLL = 4


def _cparams(n_axes=1):
    return pltpu.CompilerParams(dimension_semantics=("arbitrary",) * n_axes,
                                vmem_limit_bytes=VMEM_LIMIT)


def _dot(a, b):
    return jnp.dot(a, b, preferred_element_type=F32)


def _split2(v):
    hi = v.astype(BF16)
    lo = (v - hi.astype(F32)).astype(BF16)
    return hi, lo


def _split3(v):
    hi = v.astype(BF16)
    r1 = v - hi.astype(F32)
    mid = r1.astype(BF16)
    lo = (r1 - mid.astype(F32)).astype(BF16)
    return hi, mid, lo


def _layernorm(x, g, b):
    mu = jnp.mean(x, axis=-1, keepdims=True)
    xc = x - mu
    var = jnp.mean(xc * xc, axis=-1, keepdims=True)
    return xc * lax.rsqrt(var + NORM_EPS) * g + b


def _gelu_tanh(x):
    c = math.sqrt(2.0 / math.pi)
    return x * (0.5 * (1.0 + jnp.tanh(c * (x + 0.044715 * (x * x * x)))))


def _full(shape):
    nd = len(shape)
    return pl.BlockSpec(shape, lambda *_: (0,) * nd)


def _even_kernel(x_ref, win_ref, convw_ref, gavg_ref, lng_ref, lnb_ref, ws_ref, sb_ref, wout_ref,
                 g_ref, b_ref, xo_ref, hs_ref, mix_ref, *, tm, seq, alpha, a_dim):
    i = pl.program_id(0)
    x = x_ref[...]
    proj = _dot(x.astype(BF16), win_ref[...])
    a_c = proj[:, 0:a_dim]
    a_b = proj[:, a_dim:2 * a_dim]
    a_v = proj[:, 2 * a_dim:3 * a_dim]
    b_u = proj[:, 3 * a_dim:4 * a_dim]
    b_v = proj[:, 4 * a_dim:5 * a_dim]

    h = a_c * a_v

    @pl.when((i * tm) % seq == 0)
    def _():
        hs_ref[0:SUBLANES, :] = jnp.zeros((SUBLANES, a_dim), F32)

    hs_ref[SUBLANES:SUBLANES + tm, :] = h
    conv = (hs_ref[SUBLANES - 2:SUBLANES - 2 + tm, :] * convw_ref[0:1, :]
            + hs_ref[SUBLANES - 1:SUBLANES - 1 + tm, :] * convw_ref[1:2, :]
            + h * convw_ref[2:3, :])
    hs_ref[0:SUBLANES, :] = h[tm - SUBLANES:tm, :]
    mix_ref[:, 0:a_dim] = (a_b * conv).astype(BF16)

    u = _gelu_tanh(b_u)
    v = _gelu_tanh(b_v)
    gavg = gavg_ref[...]
    lane = lax.broadcasted_iota(I32, (SGU_BLOCK, LANES), 1)
    low = lane < GROUP_DIM
    for c in range(a_dim // LANES):
        vc = v[:, c * LANES:(c + 1) * LANES]
        hi, lo = _split2(vc)
        mean = _dot(hi, gavg) + _dot(lo, gavg)
        d = vc - mean
        hi, lo = _split2(d * d)
        var = _dot(hi, gavg) + _dot(lo, gavg)
        vn = d * lax.rsqrt(var + NORM_EPS) * lng_ref[:, c * LANES:(c + 1) * LANES] \
            + lnb_ref[:, c * LANES:(c + 1) * LANES]
        for blk in range(tm // SGU_BLOCK):
            vb = vn[blk * SGU_BLOCK:(blk + 1) * SGU_BLOCK, :]
            rhs = jnp.concatenate([jnp.where(low, vb, 0.0), jnp.where(low, 0.0, vb)], axis=0).astype(BF16)
            sg = _dot(ws_ref[c], rhs) + sb_ref[:, c * LANES:(c + 1) * LANES]
            ub = u[blk * SGU_BLOCK:(blk + 1) * SGU_BLOCK, c * LANES:(c + 1) * LANES]
            mix_ref[blk * SGU_BLOCK:(blk + 1) * SGU_BLOCK, a_dim + c * LANES:a_dim + (c + 1) * LANES] = \
                (ub * sg).astype(BF16)

    out = _dot(mix_ref[...], wout_ref[...])
    xn = _layernorm(alpha * x + out, g_ref[...], b_ref[...])
    xo_ref[...] = xn


def _even_layer(x, w_in, conv_w, sgu_ln_g, sgu_ln_b, sgu_w, sgu_b, w_out, ln_g, ln_b, *, seq, alpha):
    n, d = x.shape
    a_dim = conv_w.shape[1]
    tm = min(ROW_TILE, seq)
    n_groups = sgu_w.shape[0]
    pos = np.arange(SGU_BLOCK)
    mask = (pos[None, :] // CHUNK) <= (pos[:, None] // CHUNK)
    w_s = jnp.where(mask, sgu_w, 0.0)
    ws_pairs = jnp.concatenate([w_s[0::2], w_s[1::2]], axis=2).astype(BF16)
    sb_full = jnp.repeat(sgu_b.T, GROUP_DIM, axis=1)
    convw = jnp.zeros((SUBLANES, a_dim), F32).at[0:CONV_WIDTH].set(conv_w)
    g_idx = np.arange(LANES) // GROUP_DIM
    gavg = jnp.asarray((g_idx[:, None] == g_idx[None, :]).astype(np.float32) / GROUP_DIM, BF16)
    assert n_groups * GROUP_DIM == a_dim and w_in.shape[1] == 5 * a_dim
    kern = functools.partial(_even_kernel, tm=tm, seq=seq, alpha=alpha, a_dim=a_dim)
    return pl.pallas_call(
        kern,
        grid=(n // tm,),
        in_specs=[pl.BlockSpec((tm, d), lambda i: (i, 0)),
                  _full(w_in.shape), _full(convw.shape), _full(gavg.shape),
                  _full((1, a_dim)), _full((1, a_dim)), _full(ws_pairs.shape), _full(sb_full.shape),
                  _full(w_out.shape), _full((1, d)), _full((1, d))],
        out_specs=pl.BlockSpec((tm, d), lambda i: (i, 0)),
        out_shape=jax.ShapeDtypeStruct((n, d), F32),
        scratch_shapes=[pltpu.VMEM((tm + SUBLANES, a_dim), F32), pltpu.VMEM((tm, 2 * a_dim), BF16)],
        compiler_params=_cparams(),
        name="even_mixer",
    )(x, w_in.astype(BF16), convw, gavg, sgu_ln_g.reshape(1, a_dim), sgu_ln_b.reshape(1, a_dim),
      ws_pairs, sb_full, w_out.astype(BF16), ln_g.reshape(1, d), ln_b.reshape(1, d))


MISC_KR = 0
MISC_KR_ROT = MLA_ROPE
MISC_FZ = 2 * MLA_ROPE


def _odd_proj_kernel(x_ref, win_ref, qg_ref, kvg_ref, wq_ref, wqr_ref, wk_ref, wkr_ref, wv_ref,
                     fb_ref, ctab_ref, stab_ref, ttab_ref,
                     q_ref, k_ref, v_ref, fq_ref, fk_ref, fv_ref, fcol_ref, frow_ref, carry_ref,
                     *, tm, seq, q_rank, kv_rank, fox_dim, q_scale, fq_scale):
    i = pl.program_id(0)
    x = x_ref[...]
    proj = _dot(x.astype(BF16), win_ref[...])
    o = 0
    c_q = proj[:, o:o + q_rank]; o += q_rank
    c_kv = proj[:, o:o + kv_rank]; o += kv_rank
    misc = proj[:, o:o + LANES]; o += LANES
    fq_ref[...] = (proj[:, o:o + fox_dim] * fq_scale).astype(BF16); o += fox_dim
    fk_ref[...] = proj[:, o:o + fox_dim].astype(BF16); o += fox_dim
    fv_ref[...] = proj[:, o:o + fox_dim].astype(BF16)

    cqn = (c_q * lax.rsqrt(jnp.mean(c_q * c_q, axis=-1, keepdims=True) + NORM_EPS) * qg_ref[...]).astype(BF16)
    ckn = (c_kv * lax.rsqrt(jnp.mean(c_kv * c_kv, axis=-1, keepdims=True) + NORM_EPS) * kvg_ref[...]).astype(BF16)

    q1 = _dot(cqn, wq_ref[...])
    q2 = _dot(cqn, wqr_ref[...])
    ctab = ctab_ref[...]
    stab = stab_ref[...]
    for hh in range(MLA_HEADS):
        sl = slice(hh * LANES, (hh + 1) * LANES)
        q_ref[:, sl] = ((q1[:, sl] * ctab + q2[:, sl] * stab) * q_scale).astype(BF16)

    km_hi, km_lo = _split2(misc * ttab_ref[...])
    k_ref[...] = (_dot(ckn, wk_ref[...]) + _dot(km_hi, wkr_ref[...]) + _dot(km_lo, wkr_ref[...])).astype(BF16)
    v_ref[...] = _dot(ckn, wv_ref[...]).astype(BF16)

    lane = lax.broadcasted_iota(I32, (tm, LANES), 1)
    z = misc + fb_ref[...]
    lf = -(jnp.maximum(-z, 0.0) + jnp.log1p(jnp.exp(-jnp.abs(z))))
    lf = jnp.where((lane >= MISC_FZ) & (lane < MISC_FZ + FOX_HEADS), lf, 0.0)
    r = lax.broadcasted_iota(I32, (tm, tm), 0)
    c = lax.broadcasted_iota(I32, (tm, tm), 1)
    tri = jnp.where(c <= r, 1.0, 0.0).astype(BF16)
    hi, mid, lo = _split3(lf)
    incl = _dot(tri, hi) + _dot(tri, mid) + _dot(tri, lo)

    @pl.when((i * tm) % seq == 0)
    def _():
        carry_ref[...] = jnp.zeros((SUBLANES, LANES), F32)

    fcum = incl + carry_ref[0:1, :]
    carry_ref[...] = jnp.broadcast_to(fcum[tm - 1:tm, :], (SUBLANES, LANES))
    fsc = fcum * LOG2E
    fcol_ref[...] = fsc
    frow_ref[...] = fsc.T[MISC_FZ:MISC_FZ + FOX_HEADS, :]


def _rope_tables(seq):
    half = MLA_ROPE // 2
    inv_freq = ROPE_THETA ** (-np.arange(0, MLA_ROPE, 2, dtype=np.float64) / MLA_ROPE)
    ang = np.arange(seq, dtype=np.float64)[:, None] * inv_freq[None, :]
    cos = np.concatenate([np.cos(ang), np.cos(ang)], axis=1)
    sin = np.concatenate([np.sin(ang), np.sin(ang)], axis=1)
    assert cos.shape[1] == 2 * half
    ctab = np.zeros((seq, LANES)); ctab[:, :MLA_NOPE] = 1.0; ctab[:, MLA_NOPE:MLA_NOPE + MLA_ROPE] = cos
    stab = np.zeros((seq, LANES)); stab[:, MLA_NOPE:MLA_NOPE + MLA_ROPE] = sin
    ttab = np.zeros((seq, LANES)); ttab[:, MISC_KR:MISC_KR + MLA_ROPE] = cos
    ttab[:, MISC_KR_ROT:MISC_KR_ROT + MLA_ROPE] = sin
    return (jnp.asarray(ctab, F32), jnp.asarray(stab, F32), jnp.asarray(ttab, F32))


def _rot_cols(w):
    half = MLA_ROPE // 2
    return jnp.concatenate([-w[..., half:], w[..., :half]], axis=-1)


def _odd_proj(x, w_in, q_norm_g, kv_norm_g, w_uq, w_ukv, f_bias, *, batch, seq):
    n, d = x.shape
    q_rank, kv_rank = q_norm_g.shape[0], kv_norm_g.shape[0]
    fox_dim = FOX_HEADS * FOX_HEAD_DIM
    tm = min(ROW_TILE, seq)
    o = 0
    w_cq = w_in[:, o:o + q_rank]; o += q_rank
    w_ckv = w_in[:, o:o + kv_rank]; o += kv_rank
    w_kr = w_in[:, o:o + MLA_ROPE]; o += MLA_ROPE
    w_f = w_in[:, o:o + 3 * fox_dim]; o += 3 * fox_dim
    w_fz = w_in[:, o:o + FOX_HEADS]
    w_misc = jnp.zeros((d, LANES), F32)
    w_misc = w_misc.at[:, MISC_KR:MISC_KR + MLA_ROPE].set(w_kr)
    w_misc = w_misc.at[:, MISC_KR_ROT:MISC_KR_ROT + MLA_ROPE].set(_rot_cols(w_kr))
    w_misc = w_misc.at[:, MISC_FZ:MISC_FZ + FOX_HEADS].set(w_fz)
    w_in_p = jnp.concatenate([w_cq, w_ckv, w_misc, w_f], axis=1).astype(BF16)
    dq = MLA_NOPE + MLA_ROPE
    wq3 = w_uq.reshape(q_rank, MLA_HEADS, dq)
    wq = jnp.zeros((q_rank, MLA_HEADS, LANES), F32).at[:, :, :dq].set(wq3)
    wqr = jnp.zeros((q_rank, MLA_HEADS, LANES), F32).at[:, :, MLA_NOPE:dq].set(_rot_cols(wq3[:, :, MLA_NOPE:]))
    wkv3 = w_ukv.reshape(kv_rank, MLA_HEADS, MLA_NOPE + MLA_V)
    wk = jnp.zeros((kv_rank, MLA_HEADS, LANES), F32).at[:, :, :MLA_NOPE].set(wkv3[:, :, :MLA_NOPE])
    wv = wkv3[:, :, MLA_NOPE:].reshape(kv_rank, MLA_HEADS * MLA_V)
    place = np.zeros((LANES, MLA_HEADS, LANES), np.float32)
    for j in range(MLA_ROPE):
        place[MISC_KR + j, :, MLA_NOPE + j] = 1.0
        place[MISC_KR_ROT + j, :, MLA_NOPE + j] = 1.0
    wkr = jnp.asarray(place.reshape(LANES, MLA_HEADS * LANES), BF16)
    fb = jnp.zeros((1, LANES), F32).at[0, MISC_FZ:MISC_FZ + FOX_HEADS].set(f_bias)
    ctab, stab, ttab = _rope_tables(seq)
    hl = MLA_HEADS * LANES
    nseq = seq // tm
    row = lambda i: (i, 0)
    tab = lambda i: (i % nseq, 0)
    kern = functools.partial(_odd_proj_kernel, tm=tm, seq=seq, q_rank=q_rank, kv_rank=kv_rank, fox_dim=fox_dim,
                             q_scale=LOG2E / math.sqrt(MLA_NOPE + MLA_ROPE), fq_scale=LOG2E / math.sqrt(FOX_HEAD_DIM))
    outs = pl.pallas_call(
        kern,
        grid=(n // tm,),
        in_specs=[pl.BlockSpec((tm, d), row), _full(w_in_p.shape), _full((1, q_rank)), _full((1, kv_rank)),
                  _full((q_rank, hl)), _full((q_rank, hl)), _full((kv_rank, hl)), _full((LANES, hl)),
                  _full((kv_rank, MLA_HEADS * MLA_V)), _full((1, LANES)),
                  pl.BlockSpec((tm, LANES), tab), pl.BlockSpec((tm, LANES), tab), pl.BlockSpec((tm, LANES), tab)],
        out_specs=[pl.BlockSpec((tm, hl), row), pl.BlockSpec((tm, hl), row),
                   pl.BlockSpec((tm, MLA_HEADS * MLA_V), row),
                   pl.BlockSpec((tm, fox_dim), row), pl.BlockSpec((tm, fox_dim), row), pl.BlockSpec((tm, fox_dim), row),
                   pl.BlockSpec((tm, LANES), row),
                   pl.BlockSpec((FOX_HEADS, tm), lambda i: (i // nseq, i % nseq))],
        out_shape=[jax.ShapeDtypeStruct((n, hl), BF16), jax.ShapeDtypeStruct((n, hl), BF16),
                   jax.ShapeDtypeStruct((n, MLA_HEADS * MLA_V), BF16),
                   jax.ShapeDtypeStruct((n, fox_dim), BF16), jax.ShapeDtypeStruct((n, fox_dim), BF16),
                   jax.ShapeDtypeStruct((n, fox_dim), BF16),
                   jax.ShapeDtypeStruct((n, LANES), F32),
                   jax.ShapeDtypeStruct((batch * FOX_HEADS, seq), F32)],
        scratch_shapes=[pltpu.VMEM((SUBLANES, LANES), F32)],
        compiler_params=_cparams(),
        name="odd_proj",
    )(x, w_in_p, q_norm_g.reshape(1, q_rank), kv_norm_g.reshape(1, kv_rank),
      wq.reshape(q_rank, hl).astype(BF16), wqr.reshape(q_rank, hl).astype(BF16),
      wk.reshape(kv_rank, hl).astype(BF16), wkr, wv.astype(BF16), fb, ctab, stab, ttab)
    return outs


def _attn_kernel(*refs, tq, fox, head_lanes):
    if fox:
        q_ref, k_ref, v_ref, fcol_ref, frow_ref, o_ref = refs
    else:
        q_ref, k_ref, v_ref, o_ref = refs
    hp = pl.program_id(1)
    qi = pl.program_id(2)
    lane = lax.broadcasted_iota(I32, (tq, LANES), 1)
    half = tq // 2
    qrel = lax.broadcasted_iota(I32, (half, 1), 0)
    krel = lax.broadcasted_iota(I32, (1, half), 1)
    if fox:
        allowed = krel <= qrel
    else:
        allowed = (krel // CHUNK) <= (qrel // CHUNK)
    qs, fqs = [], []
    for hh in range(2):
        if head_lanes == LANES:
            qs.append(q_ref[0, :, hh * LANES:(hh + 1) * LANES])
        else:
            in_head = (lane >= hh * head_lanes) & (lane < (hh + 1) * head_lanes)
            qs.append(jnp.where(in_head, q_ref[0], jnp.zeros((), BF16)))
        if fox:
            fqs.append(jnp.sum(jnp.where(lane == MISC_FZ + hp * 2 + hh, fcol_ref[0], 0.0),
                               axis=-1, keepdims=True))

    def update(carry, hh, r0, nr, start, nk, masked):
        m, l, acc = carry
        if head_lanes == LANES:
            kt = k_ref[0, pl.ds(start, nk), hh * LANES:(hh + 1) * LANES]
        else:
            kt = k_ref[0, pl.ds(start, nk), :]
        s = lax.dot_general(qs[hh][r0:r0 + nr], kt, (((1,), (1,)), ((), ())), preferred_element_type=F32)
        if fox:
            s = s + (fqs[hh][r0:r0 + nr] - frow_ref[pl.ds(hp * 2 + hh, 1), pl.ds(start, nk)])
        if masked:
            s = jnp.where(allowed, s, NEG_INF)
        m_new = jnp.maximum(m, jnp.max(s, axis=-1, keepdims=True))
        a = jnp.exp2(m - m_new)
        p = jnp.exp2(s - m_new)
        l = a * l + jnp.sum(p, axis=-1, keepdims=True)
        acc = a * acc + _dot(p.astype(BF16), v_ref[0, pl.ds(start, nk), :])
        return m_new, l, acc

    def full_step(j, carry):
        start = pl.multiple_of(j * tq, tq)
        return tuple(update(carry[hh], hh, 0, tq, start, tq, False) for hh in range(2))

    one = (jnp.full((tq, 1), NEG_INF, F32), jnp.zeros((tq, 1), F32), jnp.zeros((tq, LANES), F32))
    carry = lax.fori_loop(0, qi, full_step, (one, one))

    d0 = pl.multiple_of(qi * tq, tq)
    d1 = pl.multiple_of(qi * tq + half, half)
    outs = []
    for hh in range(2):
        top = tuple(t[:half] for t in carry[hh])
        bot = tuple(t[half:] for t in carry[hh])
        top = update(top, hh, 0, half, d0, half, True)
        bot = update(bot, hh, half, half, d0, half, False)
        bot = update(bot, hh, half, half, d1, half, True)
        outs.append(jnp.concatenate([top[2] / top[1], bot[2] / bot[1]], axis=0))
    o_ref[0] = jnp.where(lane < LANES // 2, outs[0], outs[1]).astype(o_ref.dtype)


def _attention(q, k, v, fcol, frow, *, batch, seq, fox, head_lanes):
    tq = min(ATTN_TILE, seq)
    n_heads = v.shape[2] // MLA_V
    qk_w = 2 * head_lanes
    in_specs = [pl.BlockSpec((1, tq, qk_w), lambda b, h, i: (b, i, h)),
                pl.BlockSpec((1, seq, qk_w), lambda b, h, i: (b, 0, h)),
                pl.BlockSpec((1, seq, LANES), lambda b, h, i: (b, 0, h))]
    args = [q, k, v]
    if fox:
        in_specs += [pl.BlockSpec((1, tq, LANES), lambda b, h, i: (b, i, 0)),
                     pl.BlockSpec((FOX_HEADS, seq), lambda b, h, i: (b, 0))]
        args += [fcol, frow]
    kern = functools.partial(_attn_kernel, tq=tq, fox=fox, head_lanes=head_lanes)
    return pl.pallas_call(
        kern,
        grid=(batch, n_heads // 2, seq // tq),
        in_specs=in_specs,
        out_specs=pl.BlockSpec((1, tq, LANES), lambda b, h, i: (b, i, h)),
        out_shape=jax.ShapeDtypeStruct((batch, seq, n_heads * MLA_V), BF16),
        compiler_params=_cparams(3),
        name="fox_attention" if fox else "mla_attention",
    )(*args)


def _outproj_kernel(x_ref, ya_ref, yb_ref, w_ref, g_ref, b_ref, xo_ref, *, alpha):
    mix = jnp.concatenate([ya_ref[...], yb_ref[...]], axis=-1)
    out = _dot(mix, w_ref[...])
    xn = _layernorm(alpha * x_ref[...] + out, g_ref[...], b_ref[...])
    xo_ref[...] = xn


def _outproj_ln(x, ya, yb, w_out, ln_g, ln_b, *, alpha):
    n, d = x.shape
    tm = min(ROW_TILE, n)
    row = lambda i: (i, 0)
    return pl.pallas_call(
        functools.partial(_outproj_kernel, alpha=alpha),
        grid=(n // tm,),
        in_specs=[pl.BlockSpec((tm, d), row), pl.BlockSpec((tm, ya.shape[1]), row),
                  pl.BlockSpec((tm, yb.shape[1]), row), _full(w_out.shape), _full((1, d)), _full((1, d))],
        out_specs=pl.BlockSpec((tm, d), row),
        out_shape=jax.ShapeDtypeStruct((n, d), F32),
        compiler_params=_cparams(),
        name="odd_outproj",
    )(x, ya, yb, w_out.astype(BF16), ln_g.reshape(1, d), ln_b.reshape(1, d))


def _odd_layer(x, w_in, q_norm_g, kv_norm_g, w_uq, w_ukv, f_bias, w_out, ln_g, ln_b, *, batch, seq, alpha):
    n, d = x.shape
    q, k, v, fq, fk, fv, fcol, frow = _odd_proj(x, w_in, q_norm_g, kv_norm_g, w_uq, w_ukv, f_bias,
                                                batch=batch, seq=seq)
    r3 = lambda t: t.reshape(batch, seq, t.shape[1])
    y_c = _attention(r3(q), r3(k), r3(v), None, None, batch=batch, seq=seq, fox=False, head_lanes=LANES)
    y_d = _attention(r3(fq), r3(fk), r3(fv), r3(fcol), frow, batch=batch, seq=seq, fox=True,
                     head_lanes=FOX_HEAD_DIM)
    return _outproj_ln(x, y_c.reshape(n, -1), y_d.reshape(n, -1), w_out, ln_g, ln_b, alpha=alpha)


META_IDX = 0
META_RANK = TOP_K
META_GATE = 2 * TOP_K


def _router_kernel(x_ref, whi_ref, wlo_ref, b_ref, meta_ref, cnt_ref, run_ref, *, tm):
    i = pl.program_id(0)

    @pl.when(i == 0)
    def _():
        run_ref[...] = jnp.zeros((SUBLANES, LANES), F32)

    x = x_ref[...]
    xh, xl = _split2(x)
    whi = whi_ref[...]
    logits = _dot(xh, whi) + _dot(xl, whi) + _dot(xh, wlo_ref[...]) + b_ref[...]
    lane = lax.broadcasted_iota(I32, (tm, LANES), 1)
    work = logits
    vals, sels, idxs = [], [], []
    for _ in range(TOP_K):
        m = jnp.max(work, axis=-1, keepdims=True)
        idx = jnp.min(jnp.where(work == m, lane, LANES), axis=-1, keepdims=True)
        sel = lane == idx
        vals.append(m); sels.append(sel); idxs.append(idx)
        work = jnp.where(sel, -jnp.inf, work)
    es = [jnp.exp(vk - vals[0]) for vk in vals]
    den = es[0] + es[1] + es[2] + es[3]
    chosen = jnp.where(sels[0] | sels[1] | sels[2] | sels[3], 1.0, 0.0)
    r = lax.broadcasted_iota(I32, (tm, tm), 0)
    c = lax.broadcasted_iota(I32, (tm, tm), 1)
    tri = jnp.where(c < r, 1.0, 0.0).astype(BF16)
    before = _dot(tri, chosen.astype(BF16)) + run_ref[0:1, :]
    meta = jnp.zeros((tm, LANES), F32)
    for kk in range(TOP_K):
        rank = jnp.sum(jnp.where(sels[kk], before, 0.0), axis=-1, keepdims=True)
        meta = jnp.where(lane == META_IDX + kk, idxs[kk].astype(F32), meta)
        meta = jnp.where(lane == META_RANK + kk, rank, meta)
        meta = jnp.where(lane == META_GATE + kk, es[kk] / den, meta)
    meta_ref[...] = meta
    run = run_ref[0:1, :] + jnp.sum(chosen, axis=0, keepdims=True)
    run_ref[...] = jnp.broadcast_to(run, (SUBLANES, LANES))
    cnt_ref[...] = jnp.broadcast_to(run, (SUBLANES, LANES))


def _plan_kernel(cnt_ref, offs_ref, tile_ref, last_ref, *, te, max_tiles):
    lane = lax.broadcasted_iota(I32, (SUBLANES, LANES), 1)
    cnt = cnt_ref[...]
    padded = jnp.floor((cnt + (te - 1)) / te) * te
    incl = padded
    for s in (1, 2, 4, 8, 16):
        incl = incl + jnp.where(lane >= s, pltpu.roll(incl, s, axis=1), 0.0)
    offs_ref[...] = incl - padded
    last_ref[...] = jnp.where(cnt > 0.0, (incl - te) * ROW_SUBTILES, -1.0).astype(I32)
    ends = incl[0:1, :]
    lane_t = lax.broadcasted_iota(I32, (max_tiles, LANES), 1)
    start = (lax.broadcasted_iota(I32, (max_tiles, LANES), 0) * te).astype(F32)
    done = jnp.where((ends <= start) & (lane_t < N_EXPERTS), 1.0, 0.0)
    expert = jnp.minimum(jnp.sum(done, axis=-1, keepdims=True), N_EXPERTS - 1.0)
    total = jnp.sum(jnp.where(lane_t == N_EXPERTS - 1, ends, 0.0), axis=-1, keepdims=True) / te
    tile_ref[...] = jnp.where(lane_t == 0, expert, jnp.where(lane_t == 1, total, 0.0)).astype(I32)


def _dest_kernel(meta_ref, offs_ref, dest_ref, *, tm):
    lane = lax.broadcasted_iota(I32, (tm, LANES), 1)
    meta = meta_ref[...]
    offs = offs_ref[0:1, :]
    dest = jnp.zeros((tm, LANES), F32)
    for kk in range(TOP_K):
        idx = jnp.sum(jnp.where(lane == META_IDX + kk, meta, 0.0), axis=-1, keepdims=True).astype(I32)
        rank = jnp.sum(jnp.where(lane == META_RANK + kk, meta, 0.0), axis=-1, keepdims=True)
        base = jnp.sum(jnp.where(lane == idx, offs, 0.0), axis=-1, keepdims=True)
        dest = jnp.where(lane == kk, (base + rank) * ROW_SUBTILES, dest)
    dest_ref[...] = dest.astype(I32)


def _to_token_tiles(ref, x, n_rows):
    for c in range(ROW_SUBTILES):
        ref[pl.ds(c, n_rows, stride=ROW_SUBTILES), :] = x[:, c * LANES:(c + 1) * LANES]


def _from_token_tiles(ref, n_rows, lead=()):
    return jnp.concatenate([ref[lead + (pl.ds(c, n_rows, stride=ROW_SUBTILES), slice(None))]
                            for c in range(ROW_SUBTILES)], axis=-1)


def _dispatch_kernel(last_ref, dest_hbm, x_ref, xs_hbm, idx_smem, xt_ref, zero_ref, isem, sem, zsem, *, tm, te):
    i = pl.program_id(0)
    n_rows = tm * TOP_K
    icp = pltpu.make_async_copy(dest_hbm.at[pl.ds(i * n_rows, n_rows)], idx_smem, isem)
    icp.start()

    @pl.when(i == 0)
    def _():
        zero_ref[...] = jnp.zeros(zero_ref.shape, F32)

        def zero_copy(e):
            row = pl.multiple_of(last_ref[e], ROW_SUBTILES)
            return pltpu.make_async_copy(zero_ref, xs_hbm.at[pl.ds(row, te * ROW_SUBTILES), :], zsem)

        for e in range(N_EXPERTS):
            @pl.when(last_ref[e] >= 0)
            def _():
                zero_copy(e).start()
        for e in range(N_EXPERTS):
            @pl.when(last_ref[e] >= 0)
            def _():
                zero_copy(e).wait()

    _to_token_tiles(xt_ref, x_ref[...], tm)
    icp.wait()

    def row_copy(r, kk):
        d = pl.multiple_of(idx_smem[r * TOP_K + kk], ROW_SUBTILES)
        src = pl.multiple_of(r * ROW_SUBTILES, ROW_SUBTILES)
        return pltpu.make_async_copy(xt_ref.at[pl.ds(src, ROW_SUBTILES), :],
                                     xs_hbm.at[pl.ds(d, ROW_SUBTILES), :], sem)

    def issue(r, carry):
        for kk in range(TOP_K):
            row_copy(r, kk).start(priority=kk % 2)
        return carry

    lax.fori_loop(0, tm, issue, 0, unroll=ISSUE_UNROLL)

    def drain(r, carry):
        for kk in range(TOP_K):
            row_copy(r, kk).wait()
        return carry

    lax.fori_loop(0, tm, drain, 0)


def _expert_kernel(te_ref, nt_ref, xs_ref, w1_ref, b1_ref, w2_ref, b2_ref, perm_ref, ys_ref, w1b_ref, w2b_ref,
                   *, te, d_ff):
    j = pl.program_id(0)
    e = te_ref[j]
    n_tiles = nt_ref[0]
    prev = te_ref[jnp.maximum(j - 1, 0)]

    @pl.when((j == 0) | (e != prev))
    def _():
        perm = perm_ref[...]
        for cb in range(2 * d_ff // (2 * LANES)):
            sl = slice(cb * 2 * LANES, (cb + 1) * 2 * LANES)
            w1b_ref[:, sl] = _dot(w1_ref[0, 0, :, sl].astype(BF16), perm).astype(BF16)
        w2b_ref[...] = w2_ref[0, 0].astype(BF16)

    @pl.when(j < n_tiles)
    def _():
        xb = _from_token_tiles(xs_ref, te).astype(BF16)
        h = _dot(xb, w1b_ref[...]) + b1_ref[0, 0]
        acts = []
        for cb in range(d_ff // LANES):
            glu = jnp.minimum(h[:, cb * 2 * LANES:cb * 2 * LANES + LANES], SWIGLU_LIMIT)
            lin = jnp.clip(h[:, cb * 2 * LANES + LANES:(cb + 1) * 2 * LANES], -SWIGLU_LIMIT, SWIGLU_LIMIT)
            sig = 1.0 / (1.0 + jnp.exp(-SWIGLU_ALPHA * glu))
            acts.append((glu * sig * (lin + 1.0)).astype(BF16))
        act = jnp.concatenate(acts, axis=-1)
        _to_token_tiles(ys_ref, _dot(act, w2b_ref[...]) + b2_ref[0, 0], te)

    @pl.when(j >= n_tiles)
    def _():
        ys_ref[...] = jnp.zeros(ys_ref.shape, F32)


def _combine_kernel(dest_hbm, meta_ref, x_ref, g_ref, b_ref, ys_hbm, xo_ref, idx_smem, buf_ref, isem, sem,
                    *, tm, alpha):
    i = pl.program_id(0)
    n_rows = tm * TOP_K
    icp = pltpu.make_async_copy(dest_hbm.at[pl.ds(i * n_rows, n_rows)], idx_smem, isem)
    icp.start()
    icp.wait()

    def row_copy(r, kk):
        d = pl.multiple_of(idx_smem[r * TOP_K + kk], ROW_SUBTILES)
        dst = pl.multiple_of(r * ROW_SUBTILES, ROW_SUBTILES)
        return pltpu.make_async_copy(ys_hbm.at[pl.ds(d, ROW_SUBTILES), :],
                                     buf_ref.at[kk, pl.ds(dst, ROW_SUBTILES), :], sem)

    def issue(r, carry):
        for kk in range(TOP_K):
            row_copy(r, kk).start(priority=kk % 2)
        return carry

    lax.fori_loop(0, tm, issue, 0, unroll=ISSUE_UNROLL)

    def drain(r, carry):
        for kk in range(TOP_K):
            row_copy(r, kk).wait()
        return carry

    lax.fori_loop(0, tm, drain, 0)

    lane = lax.broadcasted_iota(I32, (tm, LANES), 1)
    meta = meta_ref[...]
    ffn = jnp.zeros(x_ref.shape, F32)
    for kk in range(TOP_K):
        gate = jnp.sum(jnp.where(lane == META_GATE + kk, meta, 0.0), axis=-1, keepdims=True)
        ffn = ffn + gate * _from_token_tiles(buf_ref, tm, lead=(kk,))
    xn = _layernorm(alpha * x_ref[...] + ffn, g_ref[...], b_ref[...])
    xo_ref[...] = xn


def _deinterleave_perm():
    p = np.zeros((2 * LANES, 2 * LANES), np.float32)
    j = np.arange(LANES)
    p[2 * j, j] = 1.0
    p[2 * j + 1, LANES + j] = 1.0
    return jnp.asarray(p, BF16)


def _regroup_expert_biases(b1, b2):
    n_l, n_exp, two_ff = b1.shape
    b1_r = b1.reshape(n_l, n_exp, two_ff // (2 * LANES), LANES, 2).transpose(0, 1, 2, 4, 3)
    return b1_r.reshape(n_l, n_exp, 1, two_ff), b2.reshape(n_l, n_exp, 1, b2.shape[-1])


def _moe_layer(x, w_router, b_router, w1, b1_r, w2, b2, ln_g, ln_b, *, layer, alpha):
    n, d = x.shape
    assert d == ROW_SUBTILES * LANES
    _, n_exp, _, two_ff = w1.shape
    d_ff = two_ff // 2
    te = EXPERT_TILE
    max_tiles = (n * TOP_K) // te + n_exp
    max_tiles_p = -(-max_tiles // SUBLANES) * SUBLANES
    tm = min(ROW_TILE, n)
    row = lambda i: (i, 0)

    wr = jnp.zeros((d, LANES), F32).at[:, :n_exp].set(w_router)
    wr_hi = wr.astype(BF16)
    wr_lo = (wr - wr_hi.astype(F32)).astype(BF16)
    br = jnp.full((1, LANES), NEG_INF, F32).at[0, :n_exp].set(b_router)
    meta, counts = pl.pallas_call(
        functools.partial(_router_kernel, tm=tm),
        grid=(n // tm,),
        in_specs=[pl.BlockSpec((tm, d), row), _full((d, LANES)), _full((d, LANES)), _full((1, LANES))],
        out_specs=[pl.BlockSpec((tm, LANES), row), _full((SUBLANES, LANES))],
        out_shape=[jax.ShapeDtypeStruct((n, LANES), F32), jax.ShapeDtypeStruct((SUBLANES, LANES), F32)],
        scratch_shapes=[pltpu.VMEM((SUBLANES, LANES), F32)],
        compiler_params=_cparams(),
        name="moe_router",
    )(x, wr_hi, wr_lo, br)

    offs, tiles, last_rows = pl.pallas_call(
        functools.partial(_plan_kernel, te=te, max_tiles=max_tiles_p),
        out_shape=[jax.ShapeDtypeStruct((SUBLANES, LANES), F32), jax.ShapeDtypeStruct((max_tiles_p, LANES), I32),
                   jax.ShapeDtypeStruct((SUBLANES, LANES), I32)],
        name="moe_plan",
    )(counts)

    dest = pl.pallas_call(
        functools.partial(_dest_kernel, tm=tm),
        grid=(n // tm,),
        in_specs=[pl.BlockSpec((tm, LANES), row), _full((SUBLANES, LANES))],
        out_specs=pl.BlockSpec((tm, LANES), row),
        out_shape=jax.ShapeDtypeStruct((n, LANES), I32),
        compiler_params=_cparams(),
        name="moe_dest",
    )(meta, offs)
    dest_flat = dest[:, :TOP_K].reshape(n * TOP_K)
    tile_expert = tiles[:, 0]
    n_tiles = tiles[0:1, 1]

    tmv = min(MOVE_TILE, n)
    n_slots = max_tiles * te
    n_steps = n // tmv
    xs = pl.pallas_call(
        functools.partial(_dispatch_kernel, tm=tmv, te=te),
        grid_spec=pltpu.PrefetchScalarGridSpec(
            num_scalar_prefetch=1,
            grid=(n_steps,),
            in_specs=[pl.BlockSpec(memory_space=pl.ANY), pl.BlockSpec((tmv, d), lambda i, last: (i, 0))],
            out_specs=pl.BlockSpec(memory_space=pl.ANY),
            scratch_shapes=[pltpu.SMEM((tmv * TOP_K,), I32), pltpu.VMEM((tmv * ROW_SUBTILES, LANES), F32),
                            pltpu.VMEM((te * ROW_SUBTILES, LANES), F32),
                            pltpu.SemaphoreType.DMA, pltpu.SemaphoreType.DMA, pltpu.SemaphoreType.DMA]),
        out_shape=jax.ShapeDtypeStruct((n_slots * ROW_SUBTILES, LANES), F32),
        compiler_params=_cparams(),
        name="moe_dispatch",
    )(last_rows[0, :n_exp], dest_flat, x)

    ys = pl.pallas_call(
        functools.partial(_expert_kernel, te=te, d_ff=d_ff),
        grid_spec=pltpu.PrefetchScalarGridSpec(
            num_scalar_prefetch=2,
            grid=(max_tiles,),
            in_specs=[pl.BlockSpec((te * ROW_SUBTILES, LANES), lambda j, t, nt: (jnp.minimum(j, nt[0] - 1), 0)),
                      pl.BlockSpec((1, 1, d, two_ff), lambda j, t, nt: (layer, t[j], 0, 0)),
                      pl.BlockSpec((1, 1, 1, two_ff), lambda j, t, nt: (layer, t[j], 0, 0)),
                      pl.BlockSpec((1, 1, d_ff, d), lambda j, t, nt: (layer, t[j], 0, 0)),
                      pl.BlockSpec((1, 1, 1, d), lambda j, t, nt: (layer, t[j], 0, 0)),
                      pl.BlockSpec((2 * LANES, 2 * LANES), lambda j, t, nt: (0, 0))],
            out_specs=pl.BlockSpec((te * ROW_SUBTILES, LANES), lambda j, t, nt: (j, 0)),
            scratch_shapes=[pltpu.VMEM((d, two_ff), BF16), pltpu.VMEM((d_ff, d), BF16)]),
        out_shape=jax.ShapeDtypeStruct((n_slots * ROW_SUBTILES, LANES), F32),
        compiler_params=_cparams(),
        name="moe_experts",
    )(tile_expert, n_tiles, xs, w1, b1_r, w2, b2, _deinterleave_perm())

    return pl.pallas_call(
        functools.partial(_combine_kernel, tm=tmv, alpha=alpha),
        grid=(n_steps,),
        in_specs=[pl.BlockSpec(memory_space=pl.ANY), pl.BlockSpec((tmv, LANES), row), pl.BlockSpec((tmv, d), row),
                  _full((1, d)), _full((1, d)), pl.BlockSpec(memory_space=pl.ANY)],
        out_specs=pl.BlockSpec((tmv, d), row),
        out_shape=jax.ShapeDtypeStruct((n, d), F32),
        scratch_shapes=[pltpu.SMEM((tmv * TOP_K,), I32), pltpu.VMEM((TOP_K, tmv * ROW_SUBTILES, LANES), F32),
                        pltpu.SemaphoreType.DMA, pltpu.SemaphoreType.DMA],
        compiler_params=_cparams(),
        name="moe_combine",
    )(dest_flat, meta, x, ln_g.reshape(1, d), ln_b.reshape(1, d), ys)


def kernel(x, ev_w_in, ev_conv_w, ev_sgu_ln_g, ev_sgu_ln_b, ev_sgu_w, ev_sgu_b, ev_w_out, od_w_in, od_q_norm_g, od_kv_norm_g, od_w_uq, od_w_ukv, od_f_bias, od_w_out, ln_mix_g, ln_mix_b, ln_ffn_g, ln_ffn_b, moe_w_router, moe_b_router, moe_w1, moe_b1, moe_w2, moe_b2):
    batch, seq, d = x.shape
    depth = ln_mix_g.shape[0]
    alpha = (2 * depth) ** 0.25
    xf = x.reshape(batch * seq, d)
    b1_r, b2_r = _regroup_expert_biases(moe_b1, moe_b2)
    for layer in range(depth):
        i = layer // 2
        if layer % 2 == 0:
            xf = _even_layer(xf, ev_w_in[i], ev_conv_w[i], ev_sgu_ln_g[i], ev_sgu_ln_b[i], ev_sgu_w[i],
                                ev_sgu_b[i], ev_w_out[i], ln_mix_g[layer], ln_mix_b[layer], seq=seq, alpha=alpha)
        else:
            xf = _odd_layer(xf, od_w_in[i], od_q_norm_g[i], od_kv_norm_g[i], od_w_uq[i], od_w_ukv[i],
                               od_f_bias[i], od_w_out[i], ln_mix_g[layer], ln_mix_b[layer],
                               batch=batch, seq=seq, alpha=alpha)
        xf = _moe_layer(xf, moe_w_router[layer], moe_b_router[layer], moe_w1, b1_r, moe_w2, b2_r,
                           ln_ffn_g[layer], ln_ffn_b[layer], layer=layer, alpha=alpha)
    return xf.reshape(batch, seq, d)
```

```python
import functools
import math

import numpy as np
import jax
import jax.numpy as jnp
from jax import lax
from jax.experimental import pallas as pl
from jax.experimental.pallas import tpu as pltpu

F32 = jnp.float32
BF16 = jnp.bfloat16
I32 = jnp.int32

CHUNK = 64
CONV_WIDTH = 3
SGU_BLOCK = 128
GROUP_DIM = 64
MLA_HEADS = 8
MLA_NOPE = 64
MLA_ROPE = 32
MLA_V = 64
ROPE_THETA = 10000.0
FOX_HEADS = 8
FOX_HEAD_DIM = 64
N_EXPERTS = 32
TOP_K = 4
SWIGLU_ALPHA = 1.702
SWIGLU_LIMIT = 7.0
NORM_EPS = 1e-5
NEG_INF = -1e30

LANES = 128
SUBLANES = 8
ROW_SUBTILES = 8
VMEM_LIMIT = 56 * 1024 * 1024

ROW_TILE = 512
ATTN_TILE = 1024
LOG2E = math.log2(math.e)
EXPERT_TILE = 512
MOVE_TILE = 512
ISSUE_UNROLL = 4


def _cparams(n_axes=1):
    return pltpu.CompilerParams(dimension_semantics=("arbitrary",) * n_axes,
                                vmem_limit_bytes=VMEM_LIMIT)


def _dot(a, b):
    return jnp.dot(a, b, preferred_element_type=F32)


def _split2(v):
    hi = v.astype(BF16)
    lo = (v - hi.astype(F32)).astype(BF16)
    return hi, lo


def _split3(v):
    hi = v.astype(BF16)
    r1 = v - hi.astype(F32)
    mid = r1.astype(BF16)
    lo = (r1 - mid.astype(F32)).astype(BF16)
    return hi, mid, lo


def _layernorm(x, g, b):
    mu = jnp.mean(x, axis=-1, keepdims=True)
    xc = x - mu
    var = jnp.mean(xc * xc, axis=-1, keepdims=True)
    return xc * lax.rsqrt(var + NORM_EPS) * g + b


def _gelu_tanh(x):
    c = math.sqrt(2.0 / math.pi)
    return x * (0.5 * (1.0 + jnp.tanh(c * (x + 0.044715 * (x * x * x)))))


def _full(shape):
    nd = len(shape)
    return pl.BlockSpec(shape, lambda *_: (0,) * nd)


def _even_kernel(x_ref, win_ref, convw_ref, gavg_ref, lng_ref, lnb_ref, ws_ref, sb_ref, wout_ref,
                 g_ref, b_ref, xo_ref, hs_ref, mix_ref, *, tm, seq, alpha, a_dim):
    i = pl.program_id(0)
    x = x_ref[...]
    proj = _dot(x.astype(BF16), win_ref[...])
    a_c = proj[:, 0:a_dim]
    a_b = proj[:, a_dim:2 * a_dim]
    a_v = proj[:, 2 * a_dim:3 * a_dim]
    b_u = proj[:, 3 * a_dim:4 * a_dim]
    b_v = proj[:, 4 * a_dim:5 * a_dim]

    h = a_c * a_v

    @pl.when((i * tm) % seq == 0)
    def _():
        hs_ref[0:SUBLANES, :] = jnp.zeros((SUBLANES, a_dim), F32)

    hs_ref[SUBLANES:SUBLANES + tm, :] = h
    conv = (hs_ref[SUBLANES - 2:SUBLANES - 2 + tm, :] * convw_ref[0:1, :]
            + hs_ref[SUBLANES - 1:SUBLANES - 1 + tm, :] * convw_ref[1:2, :]
            + h * convw_ref[2:3, :])
    hs_ref[0:SUBLANES, :] = h[tm - SUBLANES:tm, :]
    mix_ref[:, 0:a_dim] = (a_b * conv).astype(BF16)

    u = _gelu_tanh(b_u)
    v = _gelu_tanh(b_v)
    gavg = gavg_ref[...]
    lane = lax.broadcasted_iota(I32, (SGU_BLOCK, LANES), 1)
    low = lane < GROUP_DIM
    for c in range(a_dim // LANES):
        vc = v[:, c * LANES:(c + 1) * LANES]
        hi, lo = _split2(vc)
        mean = _dot(hi, gavg) + _dot(lo, gavg)
        d = vc - mean
        hi, lo = _split2(d * d)
        var = _dot(hi, gavg) + _dot(lo, gavg)
        vn = d * lax.rsqrt(var + NORM_EPS) * lng_ref[:, c * LANES:(c + 1) * LANES] \
            + lnb_ref[:, c * LANES:(c + 1) * LANES]
        for blk in range(tm // SGU_BLOCK):
            vb = vn[blk * SGU_BLOCK:(blk + 1) * SGU_BLOCK, :]
            rhs = jnp.concatenate([jnp.where(low, vb, 0.0), jnp.where(low, 0.0, vb)], axis=0).astype(BF16)
            sg = _dot(ws_ref[c], rhs) + sb_ref[:, c * LANES:(c + 1) * LANES]
            ub = u[blk * SGU_BLOCK:(blk + 1) * SGU_BLOCK, c * LANES:(c + 1) * LANES]
            mix_ref[blk * SGU_BLOCK:(blk + 1) * SGU_BLOCK, a_dim + c * LANES:a_dim + (c + 1) * LANES] = \
                (ub * sg).astype(BF16)

    out = _dot(mix_ref[...], wout_ref[...])
    xn = _layernorm(alpha * x + out, g_ref[...], b_ref[...])
    xo_ref[...] = xn


def _even_layer(x, w_in, conv_w, sgu_ln_g, sgu_ln_b, sgu_w, sgu_b, w_out, ln_g, ln_b, *, seq, alpha):
    n, d = x.shape
    a_dim = conv_w.shape[1]
    tm = min(ROW_TILE, seq)
    n_groups = sgu_w.shape[0]
    pos = np.arange(SGU_BLOCK)
    mask = (pos[None, :] // CHUNK) <= (pos[:, None] // CHUNK)
    w_s = jnp.where(mask, sgu_w, 0.0)
    ws_pairs = jnp.concatenate([w_s[0::2], w_s[1::2]], axis=2).astype(BF16)
    sb_full = jnp.repeat(sgu_b.T, GROUP_DIM, axis=1)
    convw = jnp.zeros((SUBLANES, a_dim), F32).at[0:CONV_WIDTH].set(conv_w)
    g_idx = np.arange(LANES) // GROUP_DIM
    gavg = jnp.asarray((g_idx[:, None] == g_idx[None, :]).astype(np.float32) / GROUP_DIM, BF16)
    assert n_groups * GROUP_DIM == a_dim and w_in.shape[1] == 5 * a_dim
    kern = functools.partial(_even_kernel, tm=tm, seq=seq, alpha=alpha, a_dim=a_dim)
    return pl.pallas_call(
        kern,
        grid=(n // tm,),
        in_specs=[pl.BlockSpec((tm, d), lambda i: (i, 0)),
                  _full(w_in.shape), _full(convw.shape), _full(gavg.shape),
                  _full((1, a_dim)), _full((1, a_dim)), _full(ws_pairs.shape), _full(sb_full.shape),
                  _full(w_out.shape), _full((1, d)), _full((1, d))],
        out_specs=pl.BlockSpec((tm, d), lambda i: (i, 0)),
        out_shape=jax.ShapeDtypeStruct((n, d), F32),
        scratch_shapes=[pltpu.VMEM((tm + SUBLANES, a_dim), F32), pltpu.VMEM((tm, 2 * a_dim), BF16)],
        compiler_params=_cparams(),
        name="even_mixer",
    )(x, w_in.astype(BF16), convw, gavg, sgu_ln_g.reshape(1, a_dim), sgu_ln_b.reshape(1, a_dim),
      ws_pairs, sb_full, w_out.astype(BF16), ln_g.reshape(1, d), ln_b.reshape(1, d))


MISC_KR = 0
MISC_KR_ROT = MLA_ROPE
MISC_FZ = 2 * MLA_ROPE


def _odd_proj_kernel(x_ref, win_ref, qg_ref, kvg_ref, wq_ref, wqr_ref, wk_ref, wkr_ref, wv_ref,
                     fb_ref, ctab_ref, stab_ref, ttab_ref,
                     q_ref, k_ref, v_ref, fq_ref, fk_ref, fv_ref, fcol_ref, frow_ref, carry_ref,
                     *, tm, seq, q_rank, kv_rank, fox_dim, q_scale, fq_scale):
    i = pl.program_id(0)
    x = x_ref[...]
    proj = _dot(x.astype(BF16), win_ref[...])
    o = 0
    c_q = proj[:, o:o + q_rank]; o += q_rank
    c_kv = proj[:, o:o + kv_rank]; o += kv_rank
    misc = proj[:, o:o + LANES]; o += LANES
    fq_ref[...] = (proj[:, o:o + fox_dim] * fq_scale).astype(BF16); o += fox_dim
    fk_ref[...] = proj[:, o:o + fox_dim].astype(BF16); o += fox_dim
    fv_ref[...] = proj[:, o:o + fox_dim].astype(BF16)

    cqn = (c_q * lax.rsqrt(jnp.mean(c_q * c_q, axis=-1, keepdims=True) + NORM_EPS) * qg_ref[...]).astype(BF16)
    ckn = (c_kv * lax.rsqrt(jnp.mean(c_kv * c_kv, axis=-1, keepdims=True) + NORM_EPS) * kvg_ref[...]).astype(BF16)

    q1 = _dot(cqn, wq_ref[...])
    q2 = _dot(cqn, wqr_ref[...])
    ctab = ctab_ref[...]
    stab = stab_ref[...]
    for hh in range(MLA_HEADS):
        sl = slice(hh * LANES, (hh + 1) * LANES)
        q_ref[:, sl] = ((q1[:, sl] * ctab + q2[:, sl] * stab) * q_scale).astype(BF16)

    km_hi, km_lo = _split2(misc * ttab_ref[...])
    k_ref[...] = (_dot(ckn, wk_ref[...]) + _dot(km_hi, wkr_ref[...]) + _dot(km_lo, wkr_ref[...])).astype(BF16)
    v_ref[...] = _dot(ckn, wv_ref[...]).astype(BF16)

    lane = lax.broadcasted_iota(I32, (tm, LANES), 1)
    z = misc + fb_ref[...]
    lf = -(jnp.maximum(-z, 0.0) + jnp.log1p(jnp.exp(-jnp.abs(z))))
    lf = jnp.where((lane >= MISC_FZ) & (lane < MISC_FZ + FOX_HEADS), lf, 0.0)
    r = lax.broadcasted_iota(I32, (tm, tm), 0)
    c = lax.broadcasted_iota(I32, (tm, tm), 1)
    tri = jnp.where(c <= r, 1.0, 0.0).astype(BF16)
    hi, mid, lo = _split3(lf)
    incl = _dot(tri, hi) + _dot(tri, mid) + _dot(tri, lo)

    @pl.when((i * tm) % seq == 0)
    def _():
        carry_ref[...] = jnp.zeros((SUBLANES, LANES), F32)

    fcum = incl + carry_ref[0:1, :]
    carry_ref[...] = jnp.broadcast_to(fcum[tm - 1:tm, :], (SUBLANES, LANES))
    fsc = fcum * LOG2E
    fcol_ref[...] = fsc
    frow_ref[...] = fsc.T[MISC_FZ:MISC_FZ + FOX_HEADS, :]


def _rope_tables(seq):
    half = MLA_ROPE // 2
    inv_freq = ROPE_THETA ** (-np.arange(0, MLA_ROPE, 2, dtype=np.float64) / MLA_ROPE)
    ang = np.arange(seq, dtype=np.float64)[:, None] * inv_freq[None, :]
    cos = np.concatenate([np.cos(ang), np.cos(ang)], axis=1)
    sin = np.concatenate([np.sin(ang), np.sin(ang)], axis=1)
    assert cos.shape[1] == 2 * half
    ctab = np.zeros((seq, LANES)); ctab[:, :MLA_NOPE] = 1.0; ctab[:, MLA_NOPE:MLA_NOPE + MLA_ROPE] = cos
    stab = np.zeros((seq, LANES)); stab[:, MLA_NOPE:MLA_NOPE + MLA_ROPE] = sin
    ttab = np.zeros((seq, LANES)); ttab[:, MISC_KR:MISC_KR + MLA_ROPE] = cos
    ttab[:, MISC_KR_ROT:MISC_KR_ROT + MLA_ROPE] = sin
    return (jnp.asarray(ctab, F32), jnp.asarray(stab, F32), jnp.asarray(ttab, F32))


def _rot_cols(w):
    half = MLA_ROPE // 2
    return jnp.concatenate([-w[..., half:], w[..., :half]], axis=-1)


def _odd_proj(x, w_in, q_norm_g, kv_norm_g, w_uq, w_ukv, f_bias, *, batch, seq):
    n, d = x.shape
    q_rank, kv_rank = q_norm_g.shape[0], kv_norm_g.shape[0]
    fox_dim = FOX_HEADS * FOX_HEAD_DIM
    tm = min(ROW_TILE, seq)
    o = 0
    w_cq = w_in[:, o:o + q_rank]; o += q_rank
    w_ckv = w_in[:, o:o + kv_rank]; o += kv_rank
    w_kr = w_in[:, o:o + MLA_ROPE]; o += MLA_ROPE
    w_f = w_in[:, o:o + 3 * fox_dim]; o += 3 * fox_dim
    w_fz = w_in[:, o:o + FOX_HEADS]
    w_misc = jnp.zeros((d, LANES), F32)
    w_misc = w_misc.at[:, MISC_KR:MISC_KR + MLA_ROPE].set(w_kr)
    w_misc = w_misc.at[:, MISC_KR_ROT:MISC_KR_ROT + MLA_ROPE].set(_rot_cols(w_kr))
    w_misc = w_misc.at[:, MISC_FZ:MISC_FZ + FOX_HEADS].set(w_fz)
    w_in_p = jnp.concatenate([w_cq, w_ckv, w_misc, w_f], axis=1).astype(BF16)
    dq = MLA_NOPE + MLA_ROPE
    wq3 = w_uq.reshape(q_rank, MLA_HEADS, dq)
    wq = jnp.zeros((q_rank, MLA_HEADS, LANES), F32).at[:, :, :dq].set(wq3)
    wqr = jnp.zeros((q_rank, MLA_HEADS, LANES), F32).at[:, :, MLA_NOPE:dq].set(_rot_cols(wq3[:, :, MLA_NOPE:]))
    wkv3 = w_ukv.reshape(kv_rank, MLA_HEADS, MLA_NOPE + MLA_V)
    wk = jnp.zeros((kv_rank, MLA_HEADS, LANES), F32).at[:, :, :MLA_NOPE].set(wkv3[:, :, :MLA_NOPE])
    wv = wkv3[:, :, MLA_NOPE:].reshape(kv_rank, MLA_HEADS * MLA_V)
    place = np.zeros((LANES, MLA_HEADS, LANES), np.float32)
    for j in range(MLA_ROPE):
        place[MISC_KR + j, :, MLA_NOPE + j] = 1.0
        place[MISC_KR_ROT + j, :, MLA_NOPE + j] = 1.0
    wkr = jnp.asarray(place.reshape(LANES, MLA_HEADS * LANES), BF16)
    fb = jnp.zeros((1, LANES), F32).at[0, MISC_FZ:MISC_FZ + FOX_HEADS].set(f_bias)
    ctab, stab, ttab = _rope_tables(seq)
    hl = MLA_HEADS * LANES
    nseq = seq // tm
    row = lambda i: (i, 0)
    tab = lambda i: (i % nseq, 0)
    kern = functools.partial(_odd_proj_kernel, tm=tm, seq=seq, q_rank=q_rank, kv_rank=kv_rank, fox_dim=fox_dim,
                             q_scale=LOG2E / math.sqrt(MLA_NOPE + MLA_ROPE), fq_scale=LOG2E / math.sqrt(FOX_HEAD_DIM))
    outs = pl.pallas_call(
        kern,
        grid=(n // tm,),
        in_specs=[pl.BlockSpec((tm, d), row), _full(w_in_p.shape), _full((1, q_rank)), _full((1, kv_rank)),
                  _full((q_rank, hl)), _full((q_rank, hl)), _full((kv_rank, hl)), _full((LANES, hl)),
                  _full((kv_rank, MLA_HEADS * MLA_V)), _full((1, LANES)),
                  pl.BlockSpec((tm, LANES), tab), pl.BlockSpec((tm, LANES), tab), pl.BlockSpec((tm, LANES), tab)],
        out_specs=[pl.BlockSpec((tm, hl), row), pl.BlockSpec((tm, hl), row),
                   pl.BlockSpec((tm, MLA_HEADS * MLA_V), row),
                   pl.BlockSpec((tm, fox_dim), row), pl.BlockSpec((tm, fox_dim), row), pl.BlockSpec((tm, fox_dim), row),
                   pl.BlockSpec((tm, LANES), row),
                   pl.BlockSpec((FOX_HEADS, tm), lambda i: (i // nseq, i % nseq))],
        out_shape=[jax.ShapeDtypeStruct((n, hl), BF16), jax.ShapeDtypeStruct((n, hl), BF16),
                   jax.ShapeDtypeStruct((n, MLA_HEADS * MLA_V), BF16),
                   jax.ShapeDtypeStruct((n, fox_dim), BF16), jax.ShapeDtypeStruct((n, fox_dim), BF16),
                   jax.ShapeDtypeStruct((n, fox_dim), BF16),
                   jax.ShapeDtypeStruct((n, LANES), F32),
                   jax.ShapeDtypeStruct((batch * FOX_HEADS, seq), F32)],
        scratch_shapes=[pltpu.VMEM((SUBLANES, LANES), F32)],
        compiler_params=_cparams(),
        name="odd_proj",
    )(x, w_in_p, q_norm_g.reshape(1, q_rank), kv_norm_g.reshape(1, kv_rank),
      wq.reshape(q_rank, hl).astype(BF16), wqr.reshape(q_rank, hl).astype(BF16),
      wk.reshape(kv_rank, hl).astype(BF16), wkr, wv.astype(BF16), fb, ctab, stab, ttab)
    return outs


def _attn_kernel(*refs, tq, fox, head_lanes):
    if fox:
        q_ref, k_ref, v_ref, fcol_ref, frow_ref, o_ref = refs
    else:
        q_ref, k_ref, v_ref, o_ref = refs
    hp = pl.program_id(1)
    qi = pl.program_id(2)
    lane = lax.broadcasted_iota(I32, (tq, LANES), 1)
    half = tq // 2
    qrel = lax.broadcasted_iota(I32, (half, 1), 0)
    krel = lax.broadcasted_iota(I32, (1, half), 1)
    if fox:
        allowed = krel <= qrel
    else:
        allowed = (krel // CHUNK) <= (qrel // CHUNK)
    qs, fqs = [], []
    for hh in range(2):
        if head_lanes == LANES:
            qs.append(q_ref[0, :, hh * LANES:(hh + 1) * LANES])
        else:
            in_head = (lane >= hh * head_lanes) & (lane < (hh + 1) * head_lanes)
            qs.append(jnp.where(in_head, q_ref[0], jnp.zeros((), BF16)))
        if fox:
            fqs.append(jnp.sum(jnp.where(lane == MISC_FZ + hp * 2 + hh, fcol_ref[0], 0.0),
                               axis=-1, keepdims=True))

    def update(carry, hh, r0, nr, start, nk, masked):
        m, l, acc = carry
        if head_lanes == LANES:
            kt = k_ref[0, pl.ds(start, nk), hh * LANES:(hh + 1) * LANES]
        else:
            kt = k_ref[0, pl.ds(start, nk), :]
        s = lax.dot_general(qs[hh][r0:r0 + nr], kt, (((1,), (1,)), ((), ())), preferred_element_type=F32)
        if fox:
            s = s + (fqs[hh][r0:r0 + nr] - frow_ref[pl.ds(hp * 2 + hh, 1), pl.ds(start, nk)])
        if masked:
            s = jnp.where(allowed, s, NEG_INF)
        m_new = jnp.maximum(m, jnp.max(s, axis=-1, keepdims=True))
        a = jnp.exp2(m - m_new)
        p = jnp.exp2(s - m_new)
        l = a * l + jnp.sum(p, axis=-1, keepdims=True)
        acc = a * acc + _dot(p.astype(BF16), v_ref[0, pl.ds(start, nk), :])
        return m_new, l, acc

    def full_step(j, carry):
        start = pl.multiple_of(j * tq, tq)
        return tuple(update(carry[hh], hh, 0, tq, start, tq, False) for hh in range(2))

    one = (jnp.full((tq, 1), NEG_INF, F32), jnp.zeros((tq, 1), F32), jnp.zeros((tq, LANES), F32))
    carry = lax.fori_loop(0, qi, full_step, (one, one))

    d0 = pl.multiple_of(qi * tq, tq)
    d1 = pl.multiple_of(qi * tq + half, half)
    outs = []
    for hh in range(2):
        top = tuple(t[:half] for t in carry[hh])
        bot = tuple(t[half:] for t in carry[hh])
        top = update(top, hh, 0, half, d0, half, True)
        bot = update(bot, hh, half, half, d0, half, False)
        bot = update(bot, hh, half, half, d1, half, True)
        outs.append(jnp.concatenate([top[2] / top[1], bot[2] / bot[1]], axis=0))
    o_ref[0] = jnp.where(lane < LANES // 2, outs[0], outs[1]).astype(o_ref.dtype)


def _attention(q, k, v, fcol, frow, *, batch, seq, fox, head_lanes):
    tq = min(ATTN_TILE, seq)
    n_heads = v.shape[2] // MLA_V
    qk_w = 2 * head_lanes
    in_specs = [pl.BlockSpec((1, tq, qk_w), lambda b, h, i: (b, i, h)),
                pl.BlockSpec((1, seq, qk_w), lambda b, h, i: (b, 0, h)),
                pl.BlockSpec((1, seq, LANES), lambda b, h, i: (b, 0, h))]
    args = [q, k, v]
    if fox:
        in_specs += [pl.BlockSpec((1, tq, LANES), lambda b, h, i: (b, i, 0)),
                     pl.BlockSpec((FOX_HEADS, seq), lambda b, h, i: (b, 0))]
        args += [fcol, frow]
    kern = functools.partial(_attn_kernel, tq=tq, fox=fox, head_lanes=head_lanes)
    return pl.pallas_call(
        kern,
        grid=(batch, n_heads // 2, seq // tq),
        in_specs=in_specs,
        out_specs=pl.BlockSpec((1, tq, LANES), lambda b, h, i: (b, i, h)),
        out_shape=jax.ShapeDtypeStruct((batch, seq, n_heads * MLA_V), BF16),
        compiler_params=_cparams(3),
        name="fox_attention" if fox else "mla_attention",
    )(*args)


def _outproj_kernel(x_ref, ya_ref, yb_ref, w_ref, g_ref, b_ref, xo_ref, *, alpha):
    mix = jnp.concatenate([ya_ref[...], yb_ref[...]], axis=-1)
    out = _dot(mix, w_ref[...])
    xn = _layernorm(alpha * x_ref[...] + out, g_ref[...], b_ref[...])
    xo_ref[...] = xn


def _outproj_ln(x, ya, yb, w_out, ln_g, ln_b, *, alpha):
    n, d = x.shape
    tm = min(ROW_TILE, n)
    row = lambda i: (i, 0)
    return pl.pallas_call(
        functools.partial(_outproj_kernel, alpha=alpha),
        grid=(n // tm,),
        in_specs=[pl.BlockSpec((tm, d), row), pl.BlockSpec((tm, ya.shape[1]), row),
                  pl.BlockSpec((tm, yb.shape[1]), row), _full(w_out.shape), _full((1, d)), _full((1, d))],
        out_specs=pl.BlockSpec((tm, d), row),
        out_shape=jax.ShapeDtypeStruct((n, d), F32),
        compiler_params=_cparams(),
        name="odd_outproj",
    )(x, ya, yb, w_out.astype(BF16), ln_g.reshape(1, d), ln_b.reshape(1, d))


def _odd_layer(x, w_in, q_norm_g, kv_norm_g, w_uq, w_ukv, f_bias, w_out, ln_g, ln_b, *, batch, seq, alpha):
    n, d = x.shape
    q, k, v, fq, fk, fv, fcol, frow = _odd_proj(x, w_in, q_norm_g, kv_norm_g, w_uq, w_ukv, f_bias,
                                                batch=batch, seq=seq)
    r3 = lambda t: t.reshape(batch, seq, t.shape[1])
    y_c = _attention(r3(q), r3(k), r3(v), None, None, batch=batch, seq=seq, fox=False, head_lanes=LANES)
    y_d = _attention(r3(fq), r3(fk), r3(fv), r3(fcol), frow, batch=batch, seq=seq, fox=True,
                     head_lanes=FOX_HEAD_DIM)
    return _outproj_ln(x, y_c.reshape(n, -1), y_d.reshape(n, -1), w_out, ln_g, ln_b, alpha=alpha)


META_IDX = 0
META_RANK = TOP_K
META_GATE = 2 * TOP_K


def _router_kernel(x_ref, whl_ref, b_ref, meta_ref, cnt_ref, run_ref, *, tm):
    i = pl.program_id(0)

    @pl.when(i == 0)
    def _():
        run_ref[...] = jnp.zeros((SUBLANES, LANES), F32)

    x = x_ref[...]
    xh, xl = _split2(x)
    both = _dot(xh, whl_ref[...])
    logits = both[:, :LANES] + _dot(xl, whl_ref[:, :LANES]) + both[:, LANES:] + b_ref[...]
    lane = lax.broadcasted_iota(I32, (tm, LANES), 1)
    work = logits
    vals, sels, idxs = [], [], []
    for _ in range(TOP_K):
        m = jnp.max(work, axis=-1, keepdims=True)
        idx = jnp.min(jnp.where(work == m, lane, LANES), axis=-1, keepdims=True)
        sel = lane == idx
        vals.append(m); sels.append(sel); idxs.append(idx)
        work = jnp.where(sel, -jnp.inf, work)
    es = [jnp.exp(vk - vals[0]) for vk in vals]
    den = es[0] + es[1] + es[2] + es[3]
    chosen = jnp.where(sels[0] | sels[1] | sels[2] | sels[3], 1.0, 0.0)
    r = lax.broadcasted_iota(I32, (tm, tm), 0)
    c = lax.broadcasted_iota(I32, (tm, tm), 1)
    tri = jnp.where(c < r, 1.0, 0.0).astype(BF16)
    before = _dot(tri, chosen.astype(BF16)) + run_ref[0:1, :]
    meta = jnp.zeros((tm, LANES), F32)
    for kk in range(TOP_K):
        rank = jnp.sum(jnp.where(sels[kk], before, 0.0), axis=-1, keepdims=True)
        meta = jnp.where(lane == META_IDX + kk, idxs[kk].astype(F32), meta)
        meta = jnp.where(lane == META_RANK + kk, rank, meta)
        meta = jnp.where(lane == META_GATE + kk, es[kk] / den, meta)
    meta_ref[...] = meta
    run = run_ref[0:1, :] + jnp.sum(chosen, axis=0, keepdims=True)
    run_ref[...] = jnp.broadcast_to(run, (SUBLANES, LANES))
    cnt_ref[...] = jnp.broadcast_to(run, (SUBLANES, LANES))


def _plan_kernel(cnt_ref, offs_ref, tile_ref, last_ref, *, te, max_tiles):
    lane = lax.broadcasted_iota(I32, (SUBLANES, LANES), 1)
    cnt = cnt_ref[...]
    padded = jnp.floor((cnt + (te - 1)) / te) * te
    incl = padded
    for s in (1, 2, 4, 8, 16):
        incl = incl + jnp.where(lane >= s, pltpu.roll(incl, s, axis=1), 0.0)
    offs_ref[...] = incl - padded
    last_ref[...] = jnp.where(cnt > 0.0, (incl - te) * ROW_SUBTILES, -1.0).astype(I32)
    ends = incl[0:1, :]
    lane_t = lax.broadcasted_iota(I32, (max_tiles, LANES), 1)
    start = (lax.broadcasted_iota(I32, (max_tiles, LANES), 0) * te).astype(F32)
    done = jnp.where((ends <= start) & (lane_t < N_EXPERTS), 1.0, 0.0)
    expert = jnp.minimum(jnp.sum(done, axis=-1, keepdims=True), N_EXPERTS - 1.0)
    total = jnp.sum(jnp.where(lane_t == N_EXPERTS - 1, ends, 0.0), axis=-1, keepdims=True) / te
    tile_ref[...] = jnp.where(lane_t == 0, expert, jnp.where(lane_t == 1, total, 0.0)).astype(I32)


def _dest_kernel(meta_ref, offs_ref, dest_ref, *, tm):
    lane = lax.broadcasted_iota(I32, (tm, LANES), 1)
    meta = meta_ref[...]
    offs = offs_ref[0:1, :]
    dest = jnp.zeros((tm, LANES), F32)
    for kk in range(TOP_K):
        idx = jnp.sum(jnp.where(lane == META_IDX + kk, meta, 0.0), axis=-1, keepdims=True).astype(I32)
        rank = jnp.sum(jnp.where(lane == META_RANK + kk, meta, 0.0), axis=-1, keepdims=True)
        base = jnp.sum(jnp.where(lane == idx, offs, 0.0), axis=-1, keepdims=True)
        dest = jnp.where(lane == kk, (base + rank) * ROW_SUBTILES, dest)
    dest_ref[...] = dest.astype(I32)


def _to_token_tiles(ref, x, n_rows):
    for c in range(ROW_SUBTILES):
        ref[pl.ds(c, n_rows, stride=ROW_SUBTILES), :] = x[:, c * LANES:(c + 1) * LANES]


def _from_token_tiles(ref, n_rows, lead=(), first=0):
    return jnp.concatenate([ref[lead + (pl.ds(first * ROW_SUBTILES + c, n_rows, stride=ROW_SUBTILES), slice(None))]
                            for c in range(ROW_SUBTILES)], axis=-1)


def _dispatch_kernel(last_ref, dest_hbm, x_ref, xs_hbm, idx_smem, xt_ref, zero_ref, isem, sem, zsem, *, tm, te):
    i = pl.program_id(0)
    n_rows = tm * TOP_K
    icp = pltpu.make_async_copy(dest_hbm.at[pl.ds(i * n_rows, n_rows)], idx_smem, isem)
    icp.start()

    @pl.when(i == 0)
    def _():
        zero_ref[...] = jnp.zeros(zero_ref.shape, F32)

        def zero_copy(e):
            row = pl.multiple_of(last_ref[e], ROW_SUBTILES)
            return pltpu.make_async_copy(zero_ref, xs_hbm.at[pl.ds(row, te * ROW_SUBTILES), :], zsem)

        for e in range(N_EXPERTS):
            @pl.when(last_ref[e] >= 0)
            def _():
                zero_copy(e).start()
        for e in range(N_EXPERTS):
            @pl.when(last_ref[e] >= 0)
            def _():
                zero_copy(e).wait()

    _to_token_tiles(xt_ref, x_ref[...], tm)
    icp.wait()

    def row_copy(r, kk):
        d = pl.multiple_of(idx_smem[r * TOP_K + kk], ROW_SUBTILES)
        src = pl.multiple_of(r * ROW_SUBTILES, ROW_SUBTILES)
        return pltpu.make_async_copy(xt_ref.at[pl.ds(src, ROW_SUBTILES), :],
                                     xs_hbm.at[pl.ds(d, ROW_SUBTILES), :], sem)

    def issue(r, carry):
        for kk in range(TOP_K):
            row_copy(r, kk).start(priority=kk % 2)
        return carry

    lax.fori_loop(0, tm, issue, 0, unroll=ISSUE_UNROLL)

    def drain(r, carry):
        for kk in range(TOP_K):
            row_copy(r, kk).wait()
        return carry

    lax.fori_loop(0, tm, drain, 0)


def _expert_kernel(te_ref, nt_ref, xs_ref, w1_ref, b1_ref, w2_ref, b2_ref, perm_ref, ys_ref, w1b_ref, w2b_ref,
                   *, te, d_ff):
    j = pl.program_id(0)
    e = te_ref[j]
    n_tiles = nt_ref[0]
    prev = te_ref[jnp.maximum(j - 1, 0)]

    @pl.when((j == 0) | (e != prev))
    def _():
        perm = perm_ref[...]
        for cb in range(2 * d_ff // (2 * LANES)):
            sl = slice(cb * 2 * LANES, (cb + 1) * 2 * LANES)
            w1b_ref[:, sl] = _dot(w1_ref[0, 0, :, sl].astype(BF16), perm).astype(BF16)
        w2b_ref[...] = w2_ref[0, 0].astype(BF16)

    @pl.when(j < n_tiles)
    def _():
        xb = _from_token_tiles(xs_ref, te).astype(BF16)
        h = _dot(xb, w1b_ref[...]) + b1_ref[0, 0]
        acts = []
        for cb in range(d_ff // LANES):
            glu = jnp.minimum(h[:, cb * 2 * LANES:cb * 2 * LANES + LANES], SWIGLU_LIMIT)
            lin = jnp.clip(h[:, cb * 2 * LANES + LANES:(cb + 1) * 2 * LANES], -SWIGLU_LIMIT, SWIGLU_LIMIT)
            sig = 1.0 / (1.0 + jnp.exp(-SWIGLU_ALPHA * glu))
            acts.append((glu * sig * (lin + 1.0)).astype(BF16))
        act = jnp.concatenate(acts, axis=-1)
        _to_token_tiles(ys_ref, _dot(act, w2b_ref[...]) + b2_ref[0, 0], te)

    @pl.when(j >= n_tiles)
    def _():
        ys_ref[...] = jnp.zeros(ys_ref.shape, F32)


def _combine_kernel(dest_hbm, meta_ref, x_ref, g_ref, b_ref, ys_hbm, xo_ref, idx_smem, buf_ref, isem, sem,
                    *, tm, alpha):
    i = pl.program_id(0)
    n_rows = tm * TOP_K
    hm = tm // 2
    icp = pltpu.make_async_copy(dest_hbm.at[pl.ds(i * n_rows, n_rows)], idx_smem, isem)
    icp.start()
    icp.wait()

    def row_copy(r, kk, part):
        d = pl.multiple_of(idx_smem[r * TOP_K + kk], ROW_SUBTILES)
        dst = pl.multiple_of(r * ROW_SUBTILES, ROW_SUBTILES)
        return pltpu.make_async_copy(ys_hbm.at[pl.ds(d, ROW_SUBTILES), :],
                                     buf_ref.at[kk, pl.ds(dst, ROW_SUBTILES), :], sem.at[part])

    for part in range(2):
        def issue(r, carry, part=part):
            for kk in range(TOP_K):
                row_copy(r, kk, part).start(priority=kk % 2)
            return carry
        lax.fori_loop(part * hm, (part + 1) * hm, issue, 0, unroll=ISSUE_UNROLL)

    lane = lax.broadcasted_iota(I32, (hm, LANES), 1)
    for part in range(2):
        def drain(r, carry, part=part):
            for kk in range(TOP_K):
                row_copy(r, kk, part).wait()
            return carry
        lax.fori_loop(part * hm, (part + 1) * hm, drain, 0)

        rows = slice(part * hm, (part + 1) * hm)
        meta = meta_ref[rows, :]
        ffn = jnp.zeros((hm, x_ref.shape[1]), F32)
        for kk in range(TOP_K):
            gate = jnp.sum(jnp.where(lane == META_GATE + kk, meta, 0.0), axis=-1, keepdims=True)
            ffn = ffn + gate * _from_token_tiles(buf_ref, hm, lead=(kk,), first=part * hm)
        xo_ref[rows, :] = _layernorm(alpha * x_ref[rows, :] + ffn, g_ref[...], b_ref[...])


def _deinterleave_perm():
    p = np.zeros((2 * LANES, 2 * LANES), np.float32)
    j = np.arange(LANES)
    p[2 * j, j] = 1.0
    p[2 * j + 1, LANES + j] = 1.0
    return jnp.asarray(p, BF16)


def _regroup_expert_biases(b1, b2):
    n_l, n_exp, two_ff = b1.shape
    b1_r = b1.reshape(n_l, n_exp, two_ff // (2 * LANES), LANES, 2).transpose(0, 1, 2, 4, 3)
    return b1_r.reshape(n_l, n_exp, 1, two_ff), b2.reshape(n_l, n_exp, 1, b2.shape[-1])


def _moe_layer(x, w_router, b_router, w1, b1_r, w2, b2, ln_g, ln_b, *, layer, alpha):
    n, d = x.shape
    assert d == ROW_SUBTILES * LANES
    _, n_exp, _, two_ff = w1.shape
    d_ff = two_ff // 2
    te = EXPERT_TILE
    max_tiles = (n * TOP_K) // te + n_exp
    max_tiles_p = -(-max_tiles // SUBLANES) * SUBLANES
    tm = min(ROW_TILE, n)
    row = lambda i: (i, 0)

    wr = jnp.zeros((d, LANES), F32).at[:, :n_exp].set(w_router)
    wr_hi = wr.astype(BF16)
    wr_lo = (wr - wr_hi.astype(F32)).astype(BF16)
    wr_hl = jnp.concatenate([wr_hi, wr_lo], axis=1)
    br = jnp.full((1, LANES), NEG_INF, F32).at[0, :n_exp].set(b_router)
    meta, counts = pl.pallas_call(
        functools.partial(_router_kernel, tm=tm),
        grid=(n // tm,),
        in_specs=[pl.BlockSpec((tm, d), row), _full((d, 2 * LANES)), _full((1, LANES))],
        out_specs=[pl.BlockSpec((tm, LANES), row), _full((SUBLANES, LANES))],
        out_shape=[jax.ShapeDtypeStruct((n, LANES), F32), jax.ShapeDtypeStruct((SUBLANES, LANES), F32)],
        scratch_shapes=[pltpu.VMEM((SUBLANES, LANES), F32)],
        compiler_params=_cparams(),
        name="moe_router",
    )(x, wr_hl, br)

    offs, tiles, last_rows = pl.pallas_call(
        functools.partial(_plan_kernel, te=te, max_tiles=max_tiles_p),
        out_shape=[jax.ShapeDtypeStruct((SUBLANES, LANES), F32), jax.ShapeDtypeStruct((max_tiles_p, LANES), I32),
                   jax.ShapeDtypeStruct((SUBLANES, LANES), I32)],
        name="moe_plan",
    )(counts)

    dest = pl.pallas_call(
        functools.partial(_dest_kernel, tm=tm),
        grid=(n // tm,),
        in_specs=[pl.BlockSpec((tm, LANES), row), _full((SUBLANES, LANES))],
        out_specs=pl.BlockSpec((tm, LANES), row),
        out_shape=jax.ShapeDtypeStruct((n, LANES), I32),
        compiler_params=_cparams(),
        name="moe_dest",
    )(meta, offs)
    dest_flat = dest[:, :TOP_K].reshape(n * TOP_K)
    tile_expert = tiles[:, 0]
    n_tiles = tiles[0:1, 1]

    tmv = min(MOVE_TILE, n)
    n_slots = max_tiles * te
    n_steps = n // tmv
    xs = pl.pallas_call(
        functools.partial(_dispatch_kernel, tm=tmv, te=te),
        grid_spec=pltpu.PrefetchScalarGridSpec(
            num_scalar_prefetch=1,
            grid=(n_steps,),
            in_specs=[pl.BlockSpec(memory_space=pl.ANY), pl.BlockSpec((tmv, d), lambda i, last: (i, 0))],
            out_specs=pl.BlockSpec(memory_space=pl.ANY),
            scratch_shapes=[pltpu.SMEM((tmv * TOP_K,), I32), pltpu.VMEM((tmv * ROW_SUBTILES, LANES), F32),
                            pltpu.VMEM((te * ROW_SUBTILES, LANES), F32),
                            pltpu.SemaphoreType.DMA, pltpu.SemaphoreType.DMA, pltpu.SemaphoreType.DMA]),
        out_shape=jax.ShapeDtypeStruct((n_slots * ROW_SUBTILES, LANES), F32),
        compiler_params=_cparams(),
        name="moe_dispatch",
    )(last_rows[0, :n_exp], dest_flat, x)

    ys = pl.pallas_call(
        functools.partial(_expert_kernel, te=te, d_ff=d_ff),
        grid_spec=pltpu.PrefetchScalarGridSpec(
            num_scalar_prefetch=2,
            grid=(max_tiles,),
            in_specs=[pl.BlockSpec((te * ROW_SUBTILES, LANES), lambda j, t, nt: (jnp.minimum(j, nt[0] - 1), 0)),
                      pl.BlockSpec((1, 1, d, two_ff), lambda j, t, nt: (layer, t[j], 0, 0)),
                      pl.BlockSpec((1, 1, 1, two_ff), lambda j, t, nt: (layer, t[j], 0, 0)),
                      pl.BlockSpec((1, 1, d_ff, d), lambda j, t, nt: (layer, t[j], 0, 0)),
                      pl.BlockSpec((1, 1, 1, d), lambda j, t, nt: (layer, t[j], 0, 0)),
                      pl.BlockSpec((2 * LANES, 2 * LANES), lambda j, t, nt: (0, 0))],
            out_specs=pl.BlockSpec((te * ROW_SUBTILES, LANES), lambda j, t, nt: (j, 0)),
            scratch_shapes=[pltpu.VMEM((d, two_ff), BF16), pltpu.VMEM((d_ff, d), BF16)]),
        out_shape=jax.ShapeDtypeStruct((n_slots * ROW_SUBTILES, LANES), F32),
        compiler_params=_cparams(),
        name="moe_experts",
    )(tile_expert, n_tiles, xs, w1, b1_r, w2, b2, _deinterleave_perm())

    return pl.pallas_call(
        functools.partial(_combine_kernel, tm=tmv, alpha=alpha),
        grid=(n_steps,),
        in_specs=[pl.BlockSpec(memory_space=pl.ANY), pl.BlockSpec((tmv, LANES), row), pl.BlockSpec((tmv, d), row),
                  _full((1, d)), _full((1, d)), pl.BlockSpec(memory_space=pl.ANY)],
        out_specs=pl.BlockSpec((tmv, d), row),
        out_shape=jax.ShapeDtypeStruct((n, d), F32),
        scratch_shapes=[pltpu.SMEM((tmv * TOP_K,), I32), pltpu.VMEM((TOP_K, tmv * ROW_SUBTILES, LANES), F32),
                        pltpu.SemaphoreType.DMA, pltpu.SemaphoreType.DMA((2,))],
        compiler_params=_cparams(),
        name="moe_combine",
    )(dest_flat, meta, x, ln_g.reshape(1, d), ln_b.reshape(1, d), ys)


def kernel(x, ev_w_in, ev_conv_w, ev_sgu_ln_g, ev_sgu_ln_b, ev_sgu_w, ev_sgu_b, ev_w_out, od_w_in, od_q_norm_g, od_kv_norm_g, od_w_uq, od_w_ukv, od_f_bias, od_w_out, ln_mix_g, ln_mix_b, ln_ffn_g, ln_ffn_b, moe_w_router, moe_b_router, moe_w1, moe_b1, moe_w2, moe_b2):
    batch, seq, d = x.shape
    depth = ln_mix_g.shape[0]
    alpha = (2 * depth) ** 0.25
    xf = x.reshape(batch * seq, d)
    b1_r, b2_r = _regroup_expert_biases(moe_b1, moe_b2)
    for layer in range(depth):
        i = layer // 2
        if layer % 2 == 0:
            xf = _even_layer(xf, ev_w_in[i], ev_conv_w[i], ev_sgu_ln_g[i], ev_sgu_ln_b[i], ev_sgu_w[i],
                                ev_sgu_b[i], ev_w_out[i], ln_mix_g[layer], ln_mix_b[layer], seq=seq, alpha=alpha)
        else:
            xf = _odd_layer(xf, od_w_in[i], od_q_norm_g[i], od_kv_norm_g[i], od_w_uq[i], od_w_ukv[i],
                               od_f_bias[i], od_w_out[i], ln_mix_g[layer], ln_mix_b[layer],
                               batch=batch, seq=seq, alpha=alpha)
        xf = _moe_layer(xf, moe_w_router[layer], moe_b_router[layer], moe_w1, b1_r, moe_w2, b2_r,
                           ln_ffn_g[layer], ln_ffn_b[layer], layer=layer, alpha=alpha)
    return xf.reshape(batch, seq, d)
```

```python
import functools
import math

import numpy as np
import jax
import jax.numpy as jnp
from jax import lax
from jax.experimental import pallas as pl
from jax.experimental.pallas import tpu as pltpu

F32 = jnp.float32
BF16 = jnp.bfloat16
I32 = jnp.int32

CHUNK = 64
CONV_WIDTH = 3
SGU_BLOCK = 128
GROUP_DIM = 64
MLA_HEADS = 8
MLA_NOPE = 64
MLA_ROPE = 32
MLA_V = 64
ROPE_THETA = 10000.0
FOX_HEADS = 8
FOX_HEAD_DIM = 64
N_EXPERTS = 32
TOP_K = 4
SWIGLU_ALPHA = 1.702
SWIGLU_LIMIT = 7.0
NORM_EPS = 1e-5
NEG_INF = -1e30

LANES = 128
SUBLANES = 8
ROW_SUBTILES = 8
VMEM_LIMIT = 56 * 1024 * 1024

ROW_TILE = 512
ATTN_TILE = 1024
LOG2E = math.log2(math.e)
EXPERT_TILE = 512
MOVE_TILE = 512
ISSUE_UNROLL = 4


def _cparams(n_axes=1):
    return pltpu.CompilerParams(dimension_semantics=("arbitrary",) * n_axes,
                                vmem_limit_bytes=VMEM_LIMIT)


def _dot(a, b):
    return jnp.dot(a, b, preferred_element_type=F32)


def _split2(v):
    hi = v.astype(BF16)
    lo = (v - hi.astype(F32)).astype(BF16)
    return hi, lo


def _split3(v):
    hi = v.astype(BF16)
    r1 = v - hi.astype(F32)
    mid = r1.astype(BF16)
    lo = (r1 - mid.astype(F32)).astype(BF16)
    return hi, mid, lo


def _layernorm(x, g, b):
    mu = jnp.mean(x, axis=-1, keepdims=True)
    xc = x - mu
    var = jnp.mean(xc * xc, axis=-1, keepdims=True)
    return xc * lax.rsqrt(var + NORM_EPS) * g + b


def _gelu_tanh(x):
    c = math.sqrt(2.0 / math.pi)
    return x * (0.5 * (1.0 + jnp.tanh(c * (x + 0.044715 * (x * x * x)))))


def _full(shape):
    nd = len(shape)
    return pl.BlockSpec(shape, lambda *_: (0,) * nd)


def _even_kernel(x_ref, win_ref, convw_ref, gavg_ref, lng_ref, lnb_ref, ws_ref, sb_ref, wout_ref,
                 g_ref, b_ref, xo_ref, hs_ref, mix_ref, *, tm, seq, alpha, a_dim):
    i = pl.program_id(0)
    x = x_ref[...]
    proj = _dot(x.astype(BF16), win_ref[...])
    a_c = proj[:, 0:a_dim]
    a_b = proj[:, a_dim:2 * a_dim]
    a_v = proj[:, 2 * a_dim:3 * a_dim]
    b_u = proj[:, 3 * a_dim:4 * a_dim]
    b_v = proj[:, 4 * a_dim:5 * a_dim]

    h = a_c * a_v

    @pl.when((i * tm) % seq == 0)
    def _():
        hs_ref[0:SUBLANES, :] = jnp.zeros((SUBLANES, a_dim), F32)

    hs_ref[SUBLANES:SUBLANES + tm, :] = h
    conv = (hs_ref[SUBLANES - 2:SUBLANES - 2 + tm, :] * convw_ref[0:1, :]
            + hs_ref[SUBLANES - 1:SUBLANES - 1 + tm, :] * convw_ref[1:2, :]
            + h * convw_ref[2:3, :])
    hs_ref[0:SUBLANES, :] = h[tm - SUBLANES:tm, :]
    mix_ref[:, 0:a_dim] = (a_b * conv).astype(BF16)

    u = _gelu_tanh(b_u)
    v = _gelu_tanh(b_v)
    gavg = gavg_ref[...]
    lane = lax.broadcasted_iota(I32, (SGU_BLOCK, LANES), 1)
    low = lane < GROUP_DIM
    for c in range(a_dim // LANES):
        vc = v[:, c * LANES:(c + 1) * LANES]
        hi, lo = _split2(vc)
        mean = _dot(hi, gavg) + _dot(lo, gavg)
        d = vc - mean
        hi, lo = _split2(d * d)
        var = _dot(hi, gavg) + _dot(lo, gavg)
        vn = d * lax.rsqrt(var + NORM_EPS) * lng_ref[:, c * LANES:(c + 1) * LANES] \
            + lnb_ref[:, c * LANES:(c + 1) * LANES]
        for blk in range(tm // SGU_BLOCK):
            vb = vn[blk * SGU_BLOCK:(blk + 1) * SGU_BLOCK, :]
            rhs = jnp.concatenate([jnp.where(low, vb, 0.0), jnp.where(low, 0.0, vb)], axis=0).astype(BF16)
            sg = _dot(ws_ref[c], rhs) + sb_ref[:, c * LANES:(c + 1) * LANES]
            ub = u[blk * SGU_BLOCK:(blk + 1) * SGU_BLOCK, c * LANES:(c + 1) * LANES]
            mix_ref[blk * SGU_BLOCK:(blk + 1) * SGU_BLOCK, a_dim + c * LANES:a_dim + (c + 1) * LANES] = \
                (ub * sg).astype(BF16)

    out = _dot(mix_ref[...], wout_ref[...])
    xn = _layernorm(alpha * x + out, g_ref[...], b_ref[...])
    xo_ref[...] = xn


def _even_layer(x, w_in, conv_w, sgu_ln_g, sgu_ln_b, sgu_w, sgu_b, w_out, ln_g, ln_b, *, seq, alpha):
    n, d = x.shape
    a_dim = conv_w.shape[1]
    tm = min(ROW_TILE, seq)
    n_groups = sgu_w.shape[0]
    pos = np.arange(SGU_BLOCK)
    mask = (pos[None, :] // CHUNK) <= (pos[:, None] // CHUNK)
    w_s = jnp.where(mask, sgu_w, 0.0)
    ws_pairs = jnp.concatenate([w_s[0::2], w_s[1::2]], axis=2).astype(BF16)
    sb_full = jnp.repeat(sgu_b.T, GROUP_DIM, axis=1)
    convw = jnp.zeros((SUBLANES, a_dim), F32).at[0:CONV_WIDTH].set(conv_w)
    g_idx = np.arange(LANES) // GROUP_DIM
    gavg = jnp.asarray((g_idx[:, None] == g_idx[None, :]).astype(np.float32) / GROUP_DIM, BF16)
    assert n_groups * GROUP_DIM == a_dim and w_in.shape[1] == 5 * a_dim
    kern = functools.partial(_even_kernel, tm=tm, seq=seq, alpha=alpha, a_dim=a_dim)
    return pl.pallas_call(
        kern,
        grid=(n // tm,),
        in_specs=[pl.BlockSpec((tm, d), lambda i: (i, 0)),
                  _full(w_in.shape), _full(convw.shape), _full(gavg.shape),
                  _full((1, a_dim)), _full((1, a_dim)), _full(ws_pairs.shape), _full(sb_full.shape),
                  _full(w_out.shape), _full((1, d)), _full((1, d))],
        out_specs=pl.BlockSpec((tm, d), lambda i: (i, 0)),
        out_shape=jax.ShapeDtypeStruct((n, d), F32),
        scratch_shapes=[pltpu.VMEM((tm + SUBLANES, a_dim), F32), pltpu.VMEM((tm, 2 * a_dim), BF16)],
        compiler_params=_cparams(),
        name="even_mixer",
    )(x, w_in.astype(BF16), convw, gavg, sgu_ln_g.reshape(1, a_dim), sgu_ln_b.reshape(1, a_dim),
      ws_pairs, sb_full, w_out.astype(BF16), ln_g.reshape(1, d), ln_b.reshape(1, d))


MISC_KR = 0
MISC_KR_ROT = MLA_ROPE
MISC_FZ = 2 * MLA_ROPE


def _odd_proj_kernel(x_ref, win_ref, qg_ref, kvg_ref, wq_ref, wqr_ref, wk_ref, wkr_ref, wv_ref,
                     fb_ref, ctab_ref, stab_ref, ttab_ref,
                     q_ref, k_ref, v_ref, fq_ref, fk_ref, fv_ref, fcol_ref, frow_ref, carry_ref,
                     *, tm, seq, q_rank, kv_rank, fox_dim, q_scale, fq_scale):
    i = pl.program_id(0)
    x = x_ref[...]
    proj = _dot(x.astype(BF16), win_ref[...])
    o = 0
    c_q = proj[:, o:o + q_rank]; o += q_rank
    c_kv = proj[:, o:o + kv_rank]; o += kv_rank
    misc = proj[:, o:o + LANES]; o += LANES
    fq_ref[...] = (proj[:, o:o + fox_dim] * fq_scale).astype(BF16); o += fox_dim
    fk_ref[...] = proj[:, o:o + fox_dim].astype(BF16); o += fox_dim
    fv_ref[...] = proj[:, o:o + fox_dim].astype(BF16)

    cqn = (c_q * lax.rsqrt(jnp.mean(c_q * c_q, axis=-1, keepdims=True) + NORM_EPS) * qg_ref[...]).astype(BF16)
    ckn = (c_kv * lax.rsqrt(jnp.mean(c_kv * c_kv, axis=-1, keepdims=True) + NORM_EPS) * kvg_ref[...]).astype(BF16)

    q1 = _dot(cqn, wq_ref[...])
    q2 = _dot(cqn, wqr_ref[...])
    ctab = ctab_ref[...]
    stab = stab_ref[...]
    for hh in range(MLA_HEADS):
        sl = slice(hh * LANES, (hh + 1) * LANES)
        q_ref[:, sl] = ((q1[:, sl] * ctab + q2[:, sl] * stab) * q_scale).astype(BF16)

    km_hi, km_lo = _split2(misc * ttab_ref[...])
    k_ref[...] = (_dot(ckn, wk_ref[...]) + _dot(km_hi, wkr_ref[...]) + _dot(km_lo, wkr_ref[...])).astype(BF16)
    v_ref[...] = _dot(ckn, wv_ref[...]).astype(BF16)

    lane = lax.broadcasted_iota(I32, (tm, LANES), 1)
    z = misc + fb_ref[...]
    lf = -(jnp.maximum(-z, 0.0) + jnp.log1p(jnp.exp(-jnp.abs(z))))
    lf = jnp.where((lane >= MISC_FZ) & (lane < MISC_FZ + FOX_HEADS), lf, 0.0)
    r = lax.broadcasted_iota(I32, (tm, tm), 0)
    c = lax.broadcasted_iota(I32, (tm, tm), 1)
    tri = jnp.where(c <= r, 1.0, 0.0).astype(BF16)
    hi, mid, lo = _split3(lf)
    incl = _dot(tri, hi) + _dot(tri, mid) + _dot(tri, lo)

    @pl.when((i * tm) % seq == 0)
    def _():
        carry_ref[...] = jnp.zeros((SUBLANES, LANES), F32)

    fcum = incl + carry_ref[0:1, :]
    carry_ref[...] = jnp.broadcast_to(fcum[tm - 1:tm, :], (SUBLANES, LANES))
    fsc = fcum * LOG2E
    fcol_ref[...] = fsc
    frow_ref[...] = fsc.T[MISC_FZ:MISC_FZ + FOX_HEADS, :]


def _rope_tables(seq):
    half = MLA_ROPE // 2
    inv_freq = ROPE_THETA ** (-np.arange(0, MLA_ROPE, 2, dtype=np.float64) / MLA_ROPE)
    ang = np.arange(seq, dtype=np.float64)[:, None] * inv_freq[None, :]
    cos = np.concatenate([np.cos(ang), np.cos(ang)], axis=1)
    sin = np.concatenate([np.sin(ang), np.sin(ang)], axis=1)
    assert cos.shape[1] == 2 * half
    ctab = np.zeros((seq, LANES)); ctab[:, :MLA_NOPE] = 1.0; ctab[:, MLA_NOPE:MLA_NOPE + MLA_ROPE] = cos
    stab = np.zeros((seq, LANES)); stab[:, MLA_NOPE:MLA_NOPE + MLA_ROPE] = sin
    ttab = np.zeros((seq, LANES)); ttab[:, MISC_KR:MISC_KR + MLA_ROPE] = cos
    ttab[:, MISC_KR_ROT:MISC_KR_ROT + MLA_ROPE] = sin
    return (jnp.asarray(ctab, F32), jnp.asarray(stab, F32), jnp.asarray(ttab, F32))


def _rot_cols(w):
    half = MLA_ROPE // 2
    return jnp.concatenate([-w[..., half:], w[..., :half]], axis=-1)


def _odd_proj(x, w_in, q_norm_g, kv_norm_g, w_uq, w_ukv, f_bias, *, batch, seq):
    n, d = x.shape
    q_rank, kv_rank = q_norm_g.shape[0], kv_norm_g.shape[0]
    fox_dim = FOX_HEADS * FOX_HEAD_DIM
    tm = min(ROW_TILE, seq)
    o = 0
    w_cq = w_in[:, o:o + q_rank]; o += q_rank
    w_ckv = w_in[:, o:o + kv_rank]; o += kv_rank
    w_kr = w_in[:, o:o + MLA_ROPE]; o += MLA_ROPE
    w_f = w_in[:, o:o + 3 * fox_dim]; o += 3 * fox_dim
    w_fz = w_in[:, o:o + FOX_HEADS]
    w_misc = jnp.zeros((d, LANES), F32)
    w_misc = w_misc.at[:, MISC_KR:MISC_KR + MLA_ROPE].set(w_kr)
    w_misc = w_misc.at[:, MISC_KR_ROT:MISC_KR_ROT + MLA_ROPE].set(_rot_cols(w_kr))
    w_misc = w_misc.at[:, MISC_FZ:MISC_FZ + FOX_HEADS].set(w_fz)
    w_in_p = jnp.concatenate([w_cq, w_ckv, w_misc, w_f], axis=1).astype(BF16)
    dq = MLA_NOPE + MLA_ROPE
    wq3 = w_uq.reshape(q_rank, MLA_HEADS, dq)
    wq = jnp.zeros((q_rank, MLA_HEADS, LANES), F32).at[:, :, :dq].set(wq3)
    wqr = jnp.zeros((q_rank, MLA_HEADS, LANES), F32).at[:, :, MLA_NOPE:dq].set(_rot_cols(wq3[:, :, MLA_NOPE:]))
    wkv3 = w_ukv.reshape(kv_rank, MLA_HEADS, MLA_NOPE + MLA_V)
    wk = jnp.zeros((kv_rank, MLA_HEADS, LANES), F32).at[:, :, :MLA_NOPE].set(wkv3[:, :, :MLA_NOPE])
    wv = wkv3[:, :, MLA_NOPE:].reshape(kv_rank, MLA_HEADS * MLA_V)
    place = np.zeros((LANES, MLA_HEADS, LANES), np.float32)
    for j in range(MLA_ROPE):
        place[MISC_KR + j, :, MLA_NOPE + j] = 1.0
        place[MISC_KR_ROT + j, :, MLA_NOPE + j] = 1.0
    wkr = jnp.asarray(place.reshape(LANES, MLA_HEADS * LANES), BF16)
    fb = jnp.zeros((1, LANES), F32).at[0, MISC_FZ:MISC_FZ + FOX_HEADS].set(f_bias)
    ctab, stab, ttab = _rope_tables(seq)
    hl = MLA_HEADS * LANES
    nseq = seq // tm
    row = lambda i: (i, 0)
    tab = lambda i: (i % nseq, 0)
    kern = functools.partial(_odd_proj_kernel, tm=tm, seq=seq, q_rank=q_rank, kv_rank=kv_rank, fox_dim=fox_dim,
                             q_scale=LOG2E / math.sqrt(MLA_NOPE + MLA_ROPE), fq_scale=LOG2E / math.sqrt(FOX_HEAD_DIM))
    outs = pl.pallas_call(
        kern,
        grid=(n // tm,),
        in_specs=[pl.BlockSpec((tm, d), row), _full(w_in_p.shape), _full((1, q_rank)), _full((1, kv_rank)),
                  _full((q_rank, hl)), _full((q_rank, hl)), _full((kv_rank, hl)), _full((LANES, hl)),
                  _full((kv_rank, MLA_HEADS * MLA_V)), _full((1, LANES)),
                  pl.BlockSpec((tm, LANES), tab), pl.BlockSpec((tm, LANES), tab), pl.BlockSpec((tm, LANES), tab)],
        out_specs=[pl.BlockSpec((tm, hl), row), pl.BlockSpec((tm, hl), row),
                   pl.BlockSpec((tm, MLA_HEADS * MLA_V), row),
                   pl.BlockSpec((tm, fox_dim), row), pl.BlockSpec((tm, fox_dim), row), pl.BlockSpec((tm, fox_dim), row),
                   pl.BlockSpec((tm, LANES), row),
                   pl.BlockSpec((FOX_HEADS, tm), lambda i: (i // nseq, i % nseq))],
        out_shape=[jax.ShapeDtypeStruct((n, hl), BF16), jax.ShapeDtypeStruct((n, hl), BF16),
                   jax.ShapeDtypeStruct((n, MLA_HEADS * MLA_V), BF16),
                   jax.ShapeDtypeStruct((n, fox_dim), BF16), jax.ShapeDtypeStruct((n, fox_dim), BF16),
                   jax.ShapeDtypeStruct((n, fox_dim), BF16),
                   jax.ShapeDtypeStruct((n, LANES), F32),
                   jax.ShapeDtypeStruct((batch * FOX_HEADS, seq), F32)],
        scratch_shapes=[pltpu.VMEM((SUBLANES, LANES), F32)],
        compiler_params=_cparams(),
        name="odd_proj",
    )(x, w_in_p, q_norm_g.reshape(1, q_rank), kv_norm_g.reshape(1, kv_rank),
      wq.reshape(q_rank, hl).astype(BF16), wqr.reshape(q_rank, hl).astype(BF16),
      wk.reshape(kv_rank, hl).astype(BF16), wkr, wv.astype(BF16), fb, ctab, stab, ttab)
    return outs


def _attn_kernel(*refs, tq, fox, head_lanes):
    if fox:
        q_ref, k_ref, v_ref, fcol_ref, frow_ref, o_ref = refs
    else:
        q_ref, k_ref, v_ref, o_ref = refs
    hp = pl.program_id(1)
    qi = pl.program_id(2)
    lane = lax.broadcasted_iota(I32, (tq, LANES), 1)
    half = tq // 2
    qrel = lax.broadcasted_iota(I32, (half, 1), 0)
    krel = lax.broadcasted_iota(I32, (1, half), 1)
    if fox:
        allowed = krel <= qrel
    else:
        allowed = (krel // CHUNK) <= (qrel // CHUNK)
    qs, fqs = [], []
    for hh in range(2):
        if head_lanes == LANES:
            qs.append(q_ref[0, :, hh * LANES:(hh + 1) * LANES])
        else:
            in_head = (lane >= hh * head_lanes) & (lane < (hh + 1) * head_lanes)
            qs.append(jnp.where(in_head, q_ref[0], jnp.zeros((), BF16)))
        if fox:
            fqs.append(jnp.sum(jnp.where(lane == MISC_FZ + hp * 2 + hh, fcol_ref[0], 0.0),
                               axis=-1, keepdims=True))

    def update(carry, hh, r0, nr, start, nk, masked):
        m, l, acc = carry
        if head_lanes == LANES:
            kt = k_ref[0, pl.ds(start, nk), hh * LANES:(hh + 1) * LANES]
        else:
            kt = k_ref[0, pl.ds(start, nk), :]
        s = lax.dot_general(qs[hh][r0:r0 + nr], kt, (((1,), (1,)), ((), ())), preferred_element_type=F32)
        if fox:
            s = s + (fqs[hh][r0:r0 + nr] - frow_ref[pl.ds(hp * 2 + hh, 1), pl.ds(start, nk)])
        if masked:
            s = jnp.where(allowed, s, NEG_INF)
        m_new = jnp.maximum(m, jnp.max(s, axis=-1, keepdims=True))
        a = jnp.exp2(m - m_new)
        p = jnp.exp2(s - m_new)
        l = a * l + jnp.sum(p, axis=-1, keepdims=True)
        acc = a * acc + _dot(p.astype(BF16), v_ref[0, pl.ds(start, nk), :])
        return m_new, l, acc

    def full_step(j, carry):
        start = pl.multiple_of(j * tq, tq)
        return tuple(update(carry[hh], hh, 0, tq, start, tq, False) for hh in range(2))

    one = (jnp.full((tq, 1), NEG_INF, F32), jnp.zeros((tq, 1), F32), jnp.zeros((tq, LANES), F32))
    carry = lax.fori_loop(0, qi, full_step, (one, one))

    d0 = pl.multiple_of(qi * tq, tq)
    d1 = pl.multiple_of(qi * tq + half, half)
    outs = []
    for hh in range(2):
        top = tuple(t[:half] for t in carry[hh])
        bot = tuple(t[half:] for t in carry[hh])
        top = update(top, hh, 0, half, d0, half, True)
        bot = update(bot, hh, half, half, d0, half, False)
        bot = update(bot, hh, half, half, d1, half, True)
        outs.append(jnp.concatenate([top[2] / top[1], bot[2] / bot[1]], axis=0))
    o_ref[0] = jnp.where(lane < LANES // 2, outs[0], outs[1]).astype(o_ref.dtype)


def _attention(q, k, v, fcol, frow, *, batch, seq, fox, head_lanes):
    tq = min(ATTN_TILE, seq)
    n_heads = v.shape[2] // MLA_V
    qk_w = 2 * head_lanes
    in_specs = [pl.BlockSpec((1, tq, qk_w), lambda b, h, i: (b, i, h)),
                pl.BlockSpec((1, seq, qk_w), lambda b, h, i: (b, 0, h)),
                pl.BlockSpec((1, seq, LANES), lambda b, h, i: (b, 0, h))]
    args = [q, k, v]
    if fox:
        in_specs += [pl.BlockSpec((1, tq, LANES), lambda b, h, i: (b, i, 0)),
                     pl.BlockSpec((FOX_HEADS, seq), lambda b, h, i: (b, 0))]
        args += [fcol, frow]
    kern = functools.partial(_attn_kernel, tq=tq, fox=fox, head_lanes=head_lanes)
    return pl.pallas_call(
        kern,
        grid=(batch, n_heads // 2, seq // tq),
        in_specs=in_specs,
        out_specs=pl.BlockSpec((1, tq, LANES), lambda b, h, i: (b, i, h)),
        out_shape=jax.ShapeDtypeStruct((batch, seq, n_heads * MLA_V), BF16),
        compiler_params=_cparams(3),
        name="fox_attention" if fox else "mla_attention",
    )(*args)


def _outproj_kernel(x_ref, ya_ref, yb_ref, w_ref, g_ref, b_ref, xo_ref, *, alpha):
    mix = jnp.concatenate([ya_ref[...], yb_ref[...]], axis=-1)
    out = _dot(mix, w_ref[...])
    xn = _layernorm(alpha * x_ref[...] + out, g_ref[...], b_ref[...])
    xo_ref[...] = xn


def _outproj_ln(x, ya, yb, w_out, ln_g, ln_b, *, alpha):
    n, d = x.shape
    tm = min(ROW_TILE, n)
    row = lambda i: (i, 0)
    return pl.pallas_call(
        functools.partial(_outproj_kernel, alpha=alpha),
        grid=(n // tm,),
        in_specs=[pl.BlockSpec((tm, d), row), pl.BlockSpec((tm, ya.shape[1]), row),
                  pl.BlockSpec((tm, yb.shape[1]), row), _full(w_out.shape), _full((1, d)), _full((1, d))],
        out_specs=pl.BlockSpec((tm, d), row),
        out_shape=jax.ShapeDtypeStruct((n, d), F32),
        compiler_params=_cparams(),
        name="odd_outproj",
    )(x, ya, yb, w_out.astype(BF16), ln_g.reshape(1, d), ln_b.reshape(1, d))


def _odd_layer(x, w_in, q_norm_g, kv_norm_g, w_uq, w_ukv, f_bias, w_out, ln_g, ln_b, *, batch, seq, alpha):
    n, d = x.shape
    q, k, v, fq, fk, fv, fcol, frow = _odd_proj(x, w_in, q_norm_g, kv_norm_g, w_uq, w_ukv, f_bias,
                                                batch=batch, seq=seq)
    r3 = lambda t: t.reshape(batch, seq, t.shape[1])
    y_c = _attention(r3(q), r3(k), r3(v), None, None, batch=batch, seq=seq, fox=False, head_lanes=LANES)
    y_d = _attention(r3(fq), r3(fk), r3(fv), r3(fcol), frow, batch=batch, seq=seq, fox=True,
                     head_lanes=FOX_HEAD_DIM)
    return _outproj_ln(x, y_c.reshape(n, -1), y_d.reshape(n, -1), w_out, ln_g, ln_b, alpha=alpha)


META_IDX = 0
META_RANK = TOP_K
META_GATE = 2 * TOP_K


def _router_kernel(x_ref, whl_ref, b_ref, meta_ref, cnt_ref, run_ref, *, tm):
    i = pl.program_id(0)

    @pl.when(i == 0)
    def _():
        run_ref[...] = jnp.zeros((SUBLANES, LANES), F32)

    x = x_ref[...]
    xh, xl = _split2(x)
    both = _dot(xh, whl_ref[...])
    logits = both[:, :LANES] + _dot(xl, whl_ref[:, :LANES]) + both[:, LANES:] + b_ref[...]
    lane = lax.broadcasted_iota(I32, (tm, LANES), 1)
    work = logits
    vals, sels, idxs = [], [], []
    for _ in range(TOP_K):
        m = jnp.max(work, axis=-1, keepdims=True)
        idx = jnp.min(jnp.where(work == m, lane, LANES), axis=-1, keepdims=True)
        sel = lane == idx
        vals.append(m); sels.append(sel); idxs.append(idx)
        work = jnp.where(sel, -jnp.inf, work)
    es = [jnp.exp(vk - vals[0]) for vk in vals]
    den = es[0] + es[1] + es[2] + es[3]
    chosen = jnp.where(sels[0] | sels[1] | sels[2] | sels[3], 1.0, 0.0)
    r = lax.broadcasted_iota(I32, (tm, tm), 0)
    c = lax.broadcasted_iota(I32, (tm, tm), 1)
    tri = jnp.where(c < r, 1.0, 0.0).astype(BF16)
    before = _dot(tri, chosen.astype(BF16)) + run_ref[0:1, :]
    meta = jnp.zeros((tm, LANES), F32)
    for kk in range(TOP_K):
        rank = jnp.sum(jnp.where(sels[kk], before, 0.0), axis=-1, keepdims=True)
        meta = jnp.where(lane == META_IDX + kk, idxs[kk].astype(F32), meta)
        meta = jnp.where(lane == META_RANK + kk, rank, meta)
        meta = jnp.where(lane == META_GATE + kk, es[kk] / den, meta)
    meta_ref[...] = meta
    run = run_ref[0:1, :] + jnp.sum(chosen, axis=0, keepdims=True)
    run_ref[...] = jnp.broadcast_to(run, (SUBLANES, LANES))
    cnt_ref[...] = jnp.broadcast_to(run, (SUBLANES, LANES))


def _plan_kernel(cnt_ref, offs_ref, tile_ref, last_ref, *, te, max_tiles):
    lane = lax.broadcasted_iota(I32, (SUBLANES, LANES), 1)
    cnt = cnt_ref[...]
    padded = jnp.floor((cnt + (te - 1)) / te) * te
    incl = padded
    for s in (1, 2, 4, 8, 16):
        incl = incl + jnp.where(lane >= s, pltpu.roll(incl, s, axis=1), 0.0)
    offs_ref[...] = incl - padded
    last_ref[...] = jnp.where(cnt > 0.0, (incl - te) * ROW_SUBTILES, -1.0).astype(I32)
    ends = incl[0:1, :]
    lane_t = lax.broadcasted_iota(I32, (max_tiles, LANES), 1)
    start = (lax.broadcasted_iota(I32, (max_tiles, LANES), 0) * te).astype(F32)
    done = jnp.where((ends <= start) & (lane_t < N_EXPERTS), 1.0, 0.0)
    expert = jnp.minimum(jnp.sum(done, axis=-1, keepdims=True), N_EXPERTS - 1.0)
    total = jnp.sum(jnp.where(lane_t == N_EXPERTS - 1, ends, 0.0), axis=-1, keepdims=True) / te
    tile_ref[...] = jnp.where(lane_t == 0, expert, jnp.where(lane_t == 1, total, 0.0)).astype(I32)


def _dest_kernel(meta_ref, offs_ref, dest_ref, *, tm):
    lane = lax.broadcasted_iota(I32, (tm, LANES), 1)
    meta = meta_ref[...]
    offs = offs_ref[0:1, :]
    dest = jnp.zeros((tm, LANES), F32)
    for kk in range(TOP_K):
        idx = jnp.sum(jnp.where(lane == META_IDX + kk, meta, 0.0), axis=-1, keepdims=True).astype(I32)
        rank = jnp.sum(jnp.where(lane == META_RANK + kk, meta, 0.0), axis=-1, keepdims=True)
        base = jnp.sum(jnp.where(lane == idx, offs, 0.0), axis=-1, keepdims=True)
        dest = jnp.where(lane == kk, (base + rank) * ROW_SUBTILES, dest)
    dest_ref[...] = dest.astype(I32)


def _to_token_tiles(ref, x, n_rows):
    for c in range(ROW_SUBTILES):
        ref[pl.ds(c, n_rows, stride=ROW_SUBTILES), :] = x[:, c * LANES:(c + 1) * LANES]


def _from_token_tiles(ref, n_rows, lead=(), first=0):
    return jnp.concatenate([ref[lead + (pl.ds(first * ROW_SUBTILES + c, n_rows, stride=ROW_SUBTILES), slice(None))]
                            for c in range(ROW_SUBTILES)], axis=-1)


def _idx_chunk_copy(dest_hbm, idx_smem, isem, step, n_rows):
    slot = step % 2
    off = pl.multiple_of(slot * n_rows, n_rows)
    return pltpu.make_async_copy(dest_hbm.at[pl.ds(step * n_rows, n_rows)], idx_smem.at[pl.ds(off, n_rows)],
                                 isem.at[slot])


def _prefetch_idx(dest_hbm, idx_smem, isem, i, n_steps, n_rows):
    @pl.when(i == 0)
    def _():
        _idx_chunk_copy(dest_hbm, idx_smem, isem, 0, n_rows).start()

    _idx_chunk_copy(dest_hbm, idx_smem, isem, i, n_rows).wait()

    @pl.when(i + 1 < n_steps)
    def _():
        _idx_chunk_copy(dest_hbm, idx_smem, isem, i + 1, n_rows).start()

    return (i % 2) * n_rows


def _dispatch_kernel(last_ref, nt_ref, dest_hbm, x_ref, xs_hbm, idx_smem, xt_ref, zero_ref, isem, sem, zsem,
                     *, tm, te, n_steps, max_tiles):
    i = pl.program_id(0)
    n_rows = tm * TOP_K
    tile_rows = te * ROW_SUBTILES

    @pl.when(i == 0)
    def _():
        zero_ref[...] = jnp.zeros(zero_ref.shape, F32)

        def zero_copy(row):
            return pltpu.make_async_copy(zero_ref, xs_hbm.at[pl.ds(pl.multiple_of(row, ROW_SUBTILES), tile_rows), :],
                                         zsem)

        for e in range(N_EXPERTS):
            @pl.when(last_ref[e] >= 0)
            def _():
                zero_copy(last_ref[e]).start()

        def tail_start(j, carry):
            zero_copy(j * tile_rows).start()
            return carry

        def tail_wait(j, carry):
            zero_copy(j * tile_rows).wait()
            return carry

        lax.fori_loop(nt_ref[0], max_tiles, tail_start, 0)
        for e in range(N_EXPERTS):
            @pl.when(last_ref[e] >= 0)
            def _():
                zero_copy(last_ref[e]).wait()
        lax.fori_loop(nt_ref[0], max_tiles, tail_wait, 0)

    _to_token_tiles(xt_ref, x_ref[...], tm)
    base = _prefetch_idx(dest_hbm, idx_smem, isem, i, n_steps, n_rows)

    def row_copy(r, kk):
        d = pl.multiple_of(idx_smem[base + r * TOP_K + kk], ROW_SUBTILES)
        src = pl.multiple_of(r * ROW_SUBTILES, ROW_SUBTILES)
        return pltpu.make_async_copy(xt_ref.at[pl.ds(src, ROW_SUBTILES), :],
                                     xs_hbm.at[pl.ds(d, ROW_SUBTILES), :], sem)

    def issue(r, carry):
        for kk in range(TOP_K):
            row_copy(r, kk).start(priority=kk % 2)
        return carry

    lax.fori_loop(0, tm, issue, 0, unroll=ISSUE_UNROLL)

    for _ in range(TOP_K):
        pltpu.make_async_copy(xt_ref, xs_hbm.at[pl.ds(0, tm * ROW_SUBTILES), :], sem).wait()


def _expert_kernel(te_ref, nt_ref, xs_ref, w1_ref, b1_ref, w2_ref, b2_ref, perm_ref, ys_ref, w1b_ref, w2b_ref,
                   *, te, d_ff):
    j = pl.program_id(0)
    e = te_ref[j]
    n_tiles = nt_ref[0]
    prev = te_ref[jnp.maximum(j - 1, 0)]

    @pl.when((j == 0) | (e != prev))
    def _():
        perm = perm_ref[...]
        for cb in range(2 * d_ff // (2 * LANES)):
            sl = slice(cb * 2 * LANES, (cb + 1) * 2 * LANES)
            w1b_ref[:, sl] = _dot(w1_ref[0, 0, :, sl].astype(BF16), perm).astype(BF16)
        w2b_ref[...] = w2_ref[0, 0].astype(BF16)

    @pl.when(j < n_tiles)
    def _():
        xb = _from_token_tiles(xs_ref, te).astype(BF16)
        h = _dot(xb, w1b_ref[...]) + b1_ref[0, 0]
        acts = []
        for cb in range(d_ff // LANES):
            glu = jnp.minimum(h[:, cb * 2 * LANES:cb * 2 * LANES + LANES], SWIGLU_LIMIT)
            lin = jnp.clip(h[:, cb * 2 * LANES + LANES:(cb + 1) * 2 * LANES], -SWIGLU_LIMIT, SWIGLU_LIMIT)
            sig = 1.0 / (1.0 + jnp.exp(-SWIGLU_ALPHA * glu))
            acts.append((glu * sig * (lin + 1.0)).astype(BF16))
        act = jnp.concatenate(acts, axis=-1)
        _to_token_tiles(ys_ref, _dot(act, w2b_ref[...]) + b2_ref[0, 0], te)

    @pl.when(j >= n_tiles)
    def _():
        ys_ref[...] = jnp.zeros(ys_ref.shape, F32)


def _combine_kernel(dest_hbm, meta_ref, x_ref, g_ref, b_ref, ys_hbm, xo_ref, idx_smem, buf_ref, isem, sem,
                    *, tm, alpha, n_steps):
    i = pl.program_id(0)
    n_rows = tm * TOP_K
    hm = tm // 2
    base = _prefetch_idx(dest_hbm, idx_smem, isem, i, n_steps, n_rows)

    def row_copy(r, kk, part):
        d = pl.multiple_of(idx_smem[base + r * TOP_K + kk], ROW_SUBTILES)
        dst = pl.multiple_of(r * ROW_SUBTILES, ROW_SUBTILES)
        return pltpu.make_async_copy(ys_hbm.at[pl.ds(d, ROW_SUBTILES), :],
                                     buf_ref.at[kk, pl.ds(dst, ROW_SUBTILES), :], sem.at[part])

    for part in range(2):
        def issue(r, carry, part=part):
            for kk in range(TOP_K):
                row_copy(r, kk, part).start(priority=kk % 2)
            return carry
        lax.fori_loop(part * hm, (part + 1) * hm, issue, 0, unroll=ISSUE_UNROLL)

    lane = lax.broadcasted_iota(I32, (hm, LANES), 1)
    for part in range(2):
        for kk in range(TOP_K):
            pltpu.make_async_copy(ys_hbm.at[pl.ds(0, hm * ROW_SUBTILES), :],
                                  buf_ref.at[kk, pl.ds(part * hm * ROW_SUBTILES, hm * ROW_SUBTILES), :],
                                  sem.at[part]).wait()

        rows = slice(part * hm, (part + 1) * hm)
        meta = meta_ref[rows, :]
        ffn = jnp.zeros((hm, x_ref.shape[1]), F32)
        for kk in range(TOP_K):
            gate = jnp.sum(jnp.where(lane == META_GATE + kk, meta, 0.0), axis=-1, keepdims=True)
            ffn = ffn + gate * _from_token_tiles(buf_ref, hm, lead=(kk,), first=part * hm)
        xo_ref[rows, :] = _layernorm(alpha * x_ref[rows, :] + ffn, g_ref[...], b_ref[...])


def _deinterleave_perm():
    p = np.zeros((2 * LANES, 2 * LANES), np.float32)
    j = np.arange(LANES)
    p[2 * j, j] = 1.0
    p[2 * j + 1, LANES + j] = 1.0
    return jnp.asarray(p, BF16)


def _regroup_expert_biases(b1, b2):
    n_l, n_exp, two_ff = b1.shape
    b1_r = b1.reshape(n_l, n_exp, two_ff // (2 * LANES), LANES, 2).transpose(0, 1, 2, 4, 3)
    return b1_r.reshape(n_l, n_exp, 1, two_ff), b2.reshape(n_l, n_exp, 1, b2.shape[-1])


def _moe_layer(x, w_router, b_router, w1, b1_r, w2, b2, ln_g, ln_b, *, layer, alpha):
    n, d = x.shape
    assert d == ROW_SUBTILES * LANES
    _, n_exp, _, two_ff = w1.shape
    d_ff = two_ff // 2
    te = EXPERT_TILE
    max_tiles = (n * TOP_K) // te + n_exp
    max_tiles_p = -(-max_tiles // SUBLANES) * SUBLANES
    tm = min(ROW_TILE, n)
    row = lambda i: (i, 0)

    wr = jnp.zeros((d, LANES), F32).at[:, :n_exp].set(w_router)
    wr_hi = wr.astype(BF16)
    wr_lo = (wr - wr_hi.astype(F32)).astype(BF16)
    wr_hl = jnp.concatenate([wr_hi, wr_lo], axis=1)
    br = jnp.full((1, LANES), NEG_INF, F32).at[0, :n_exp].set(b_router)
    meta, counts = pl.pallas_call(
        functools.partial(_router_kernel, tm=tm),
        grid=(n // tm,),
        in_specs=[pl.BlockSpec((tm, d), row), _full((d, 2 * LANES)), _full((1, LANES))],
        out_specs=[pl.BlockSpec((tm, LANES), row), _full((SUBLANES, LANES))],
        out_shape=[jax.ShapeDtypeStruct((n, LANES), F32), jax.ShapeDtypeStruct((SUBLANES, LANES), F32)],
        scratch_shapes=[pltpu.VMEM((SUBLANES, LANES), F32)],
        compiler_params=_cparams(),
        name="moe_router",
    )(x, wr_hl, br)

    offs, tiles, last_rows = pl.pallas_call(
        functools.partial(_plan_kernel, te=te, max_tiles=max_tiles_p),
        out_shape=[jax.ShapeDtypeStruct((SUBLANES, LANES), F32), jax.ShapeDtypeStruct((max_tiles_p, LANES), I32),
                   jax.ShapeDtypeStruct((SUBLANES, LANES), I32)],
        name="moe_plan",
    )(counts)

    dest = pl.pallas_call(
        functools.partial(_dest_kernel, tm=tm),
        grid=(n // tm,),
        in_specs=[pl.BlockSpec((tm, LANES), row), _full((SUBLANES, LANES))],
        out_specs=pl.BlockSpec((tm, LANES), row),
        out_shape=jax.ShapeDtypeStruct((n, LANES), I32),
        compiler_params=_cparams(),
        name="moe_dest",
    )(meta, offs)
    dest_flat = dest[:, :TOP_K].reshape(n * TOP_K)
    tile_expert = tiles[:, 0]
    n_tiles = tiles[0:1, 1]

    tmv = min(MOVE_TILE, n)
    n_slots = max_tiles * te
    n_steps = n // tmv
    xs = pl.pallas_call(
        functools.partial(_dispatch_kernel, tm=tmv, te=te, n_steps=n_steps, max_tiles=max_tiles),
        grid_spec=pltpu.PrefetchScalarGridSpec(
            num_scalar_prefetch=2,
            grid=(n_steps,),
            in_specs=[pl.BlockSpec(memory_space=pl.ANY), pl.BlockSpec((tmv, d), lambda i, last, nt: (i, 0))],
            out_specs=pl.BlockSpec(memory_space=pl.ANY),
            scratch_shapes=[pltpu.SMEM((2 * tmv * TOP_K,), I32), pltpu.VMEM((tmv * ROW_SUBTILES, LANES), F32),
                            pltpu.VMEM((te * ROW_SUBTILES, LANES), F32),
                            pltpu.SemaphoreType.DMA((2,)), pltpu.SemaphoreType.DMA, pltpu.SemaphoreType.DMA]),
        out_shape=jax.ShapeDtypeStruct((n_slots * ROW_SUBTILES, LANES), F32),
        compiler_params=_cparams(),
        name="moe_dispatch",
    )(last_rows[0, :n_exp], n_tiles, dest_flat, x)

    ys = pl.pallas_call(
        functools.partial(_expert_kernel, te=te, d_ff=d_ff),
        grid_spec=pltpu.PrefetchScalarGridSpec(
            num_scalar_prefetch=2,
            grid=(max_tiles,),
            in_specs=[pl.BlockSpec((te * ROW_SUBTILES, LANES), lambda j, t, nt: (jnp.minimum(j, nt[0] - 1), 0)),
                      pl.BlockSpec((1, 1, d, two_ff), lambda j, t, nt: (layer, t[j], 0, 0)),
                      pl.BlockSpec((1, 1, 1, two_ff), lambda j, t, nt: (layer, t[j], 0, 0)),
                      pl.BlockSpec((1, 1, d_ff, d), lambda j, t, nt: (layer, t[j], 0, 0)),
                      pl.BlockSpec((1, 1, 1, d), lambda j, t, nt: (layer, t[j], 0, 0)),
                      pl.BlockSpec((2 * LANES, 2 * LANES), lambda j, t, nt: (0, 0))],
            out_specs=pl.BlockSpec((te * ROW_SUBTILES, LANES), lambda j, t, nt: (j, 0)),
            scratch_shapes=[pltpu.VMEM((d, two_ff), BF16), pltpu.VMEM((d_ff, d), BF16)]),
        out_shape=jax.ShapeDtypeStruct((n_slots * ROW_SUBTILES, LANES), F32),
        compiler_params=_cparams(),
        name="moe_experts",
    )(tile_expert, n_tiles, xs, w1, b1_r, w2, b2, _deinterleave_perm())

    return pl.pallas_call(
        functools.partial(_combine_kernel, tm=tmv, alpha=alpha, n_steps=n_steps),
        grid=(n_steps,),
        in_specs=[pl.BlockSpec(memory_space=pl.ANY), pl.BlockSpec((tmv, LANES), row), pl.BlockSpec((tmv, d), row),
                  _full((1, d)), _full((1, d)), pl.BlockSpec(memory_space=pl.ANY)],
        out_specs=pl.BlockSpec((tmv, d), row),
        out_shape=jax.ShapeDtypeStruct((n, d), F32),
        scratch_shapes=[pltpu.SMEM((2 * tmv * TOP_K,), I32), pltpu.VMEM((TOP_K, tmv * ROW_SUBTILES, LANES), F32),
                        pltpu.SemaphoreType.DMA((2,)), pltpu.SemaphoreType.DMA((2,))],
        compiler_params=_cparams(),
        name="moe_combine",
    )(dest_flat, meta, x, ln_g.reshape(1, d), ln_b.reshape(1, d), ys)


def kernel(x, ev_w_in, ev_conv_w, ev_sgu_ln_g, ev_sgu_ln_b, ev_sgu_w, ev_sgu_b, ev_w_out, od_w_in, od_q_norm_g, od_kv_norm_g, od_w_uq, od_w_ukv, od_f_bias, od_w_out, ln_mix_g, ln_mix_b, ln_ffn_g, ln_ffn_b, moe_w_router, moe_b_router, moe_w1, moe_b1, moe_w2, moe_b2):
    batch, seq, d = x.shape
    depth = ln_mix_g.shape[0]
    alpha = (2 * depth) ** 0.25
    xf = x.reshape(batch * seq, d)
    b1_r, b2_r = _regroup_expert_biases(moe_b1, moe_b2)
    for layer in range(depth):
        i = layer // 2
        if layer % 2 == 0:
            xf = _even_layer(xf, ev_w_in[i], ev_conv_w[i], ev_sgu_ln_g[i], ev_sgu_ln_b[i], ev_sgu_w[i],
                                ev_sgu_b[i], ev_w_out[i], ln_mix_g[layer], ln_mix_b[layer], seq=seq, alpha=alpha)
        else:
            xf = _odd_layer(xf, od_w_in[i], od_q_norm_g[i], od_kv_norm_g[i], od_w_uq[i], od_w_ukv[i],
                               od_f_bias[i], od_w_out[i], ln_mix_g[layer], ln_mix_b[layer],
                               batch=batch, seq=seq, alpha=alpha)
        xf = _moe_layer(xf, moe_w_router[layer], moe_b_router[layer], moe_w1, b1_r, moe_w2, b2_r,
                           ln_ffn_g[layer], ln_ffn_b[layer], layer=layer, alpha=alpha)
    return xf.reshape(batch, seq, d)
```

```python
import functools
import math

import numpy as np
import jax
import jax.numpy as jnp
from jax import lax
from jax.experimental import pallas as pl
from jax.experimental.pallas import tpu as pltpu

F32 = jnp.float32
BF16 = jnp.bfloat16
I32 = jnp.int32

CHUNK = 64
CONV_WIDTH = 3
SGU_BLOCK = 128
GROUP_DIM = 64
MLA_HEADS = 8
MLA_NOPE = 64
MLA_ROPE = 32
MLA_V = 64
ROPE_THETA = 10000.0
FOX_HEADS = 8
FOX_HEAD_DIM = 64
N_EXPERTS = 32
TOP_K = 4
SWIGLU_ALPHA = 1.702
SWIGLU_LIMIT = 7.0
NORM_EPS = 1e-5
NEG_INF = -1e30

LANES = 128
SUBLANES = 8
ROW_SUBTILES = 8
VMEM_LIMIT = 56 * 1024 * 1024

ROW_TILE = 512
ATTN_TILE = 1024
LOG2E = math.log2(math.e)
EXPERT_TILE = 512
MOVE_TILE = 512
ISSUE_UNROLL = 4


def _cparams(n_axes=1):
    return pltpu.CompilerParams(dimension_semantics=("arbitrary",) * n_axes,
                                vmem_limit_bytes=VMEM_LIMIT)


def _dot(a, b):
    return jnp.dot(a, b, preferred_element_type=F32)


def _split2(v):
    hi = v.astype(BF16)
    lo = (v - hi.astype(F32)).astype(BF16)
    return hi, lo


def _split3(v):
    hi = v.astype(BF16)
    r1 = v - hi.astype(F32)
    mid = r1.astype(BF16)
    lo = (r1 - mid.astype(F32)).astype(BF16)
    return hi, mid, lo


def _layernorm(x, g, b):
    mu = jnp.mean(x, axis=-1, keepdims=True)
    xc = x - mu
    var = jnp.mean(xc * xc, axis=-1, keepdims=True)
    return xc * lax.rsqrt(var + NORM_EPS) * g + b


def _gelu_tanh(x):
    c = math.sqrt(2.0 / math.pi)
    return x * (0.5 * (1.0 + jnp.tanh(c * (x + 0.044715 * (x * x * x)))))


def _full(shape):
    nd = len(shape)
    return pl.BlockSpec(shape, lambda *_: (0,) * nd)


def _even_kernel(x_ref, win_ref, convw_ref, gavg_ref, lng_ref, lnb_ref, ws_ref, sb_ref, wout_ref,
                 g_ref, b_ref, whl_ref, br_ref, xo_ref, meta_ref, cnt_ref, hs_ref, mix_ref, run_ref,
                 *, tm, seq, alpha, a_dim):
    i = pl.program_id(0)
    x = x_ref[...]
    proj = _dot(x.astype(BF16), win_ref[...])
    a_c = proj[:, 0:a_dim]
    a_b = proj[:, a_dim:2 * a_dim]
    a_v = proj[:, 2 * a_dim:3 * a_dim]
    b_u = proj[:, 3 * a_dim:4 * a_dim]
    b_v = proj[:, 4 * a_dim:5 * a_dim]

    h = a_c * a_v

    @pl.when((i * tm) % seq == 0)
    def _():
        hs_ref[0:SUBLANES, :] = jnp.zeros((SUBLANES, a_dim), F32)

    hs_ref[SUBLANES:SUBLANES + tm, :] = h
    conv = (hs_ref[SUBLANES - 2:SUBLANES - 2 + tm, :] * convw_ref[0:1, :]
            + hs_ref[SUBLANES - 1:SUBLANES - 1 + tm, :] * convw_ref[1:2, :]
            + h * convw_ref[2:3, :])
    hs_ref[0:SUBLANES, :] = h[tm - SUBLANES:tm, :]
    mix_ref[:, 0:a_dim] = (a_b * conv).astype(BF16)

    u = _gelu_tanh(b_u)
    v = _gelu_tanh(b_v)
    gavg = gavg_ref[...]
    lane = lax.broadcasted_iota(I32, (SGU_BLOCK, LANES), 1)
    low = lane < GROUP_DIM
    for c in range(a_dim // LANES):
        vc = v[:, c * LANES:(c + 1) * LANES]
        hi, lo = _split2(vc)
        mean = _dot(hi, gavg) + _dot(lo, gavg)
        d = vc - mean
        hi, lo = _split2(d * d)
        var = _dot(hi, gavg) + _dot(lo, gavg)
        vn = d * lax.rsqrt(var + NORM_EPS) * lng_ref[:, c * LANES:(c + 1) * LANES] \
            + lnb_ref[:, c * LANES:(c + 1) * LANES]
        for blk in range(tm // SGU_BLOCK):
            vb = vn[blk * SGU_BLOCK:(blk + 1) * SGU_BLOCK, :]
            rhs = jnp.concatenate([jnp.where(low, vb, 0.0), jnp.where(low, 0.0, vb)], axis=0).astype(BF16)
            sg = _dot(ws_ref[c], rhs) + sb_ref[:, c * LANES:(c + 1) * LANES]
            ub = u[blk * SGU_BLOCK:(blk + 1) * SGU_BLOCK, c * LANES:(c + 1) * LANES]
            mix_ref[blk * SGU_BLOCK:(blk + 1) * SGU_BLOCK, a_dim + c * LANES:a_dim + (c + 1) * LANES] = \
                (ub * sg).astype(BF16)

    out = _dot(mix_ref[...], wout_ref[...])
    xn = _layernorm(alpha * x + out, g_ref[...], b_ref[...])
    xo_ref[...] = xn
    _route(xn, whl_ref, br_ref, meta_ref, cnt_ref, run_ref, tm)


def _router_operands(w_router, b_router):
    d, n_exp = w_router.shape
    wr = jnp.zeros((d, LANES), F32).at[:, :n_exp].set(w_router)
    wr_hi = wr.astype(BF16)
    wr_lo = (wr - wr_hi.astype(F32)).astype(BF16)
    br = jnp.full((1, LANES), NEG_INF, F32).at[0, :n_exp].set(b_router)
    return jnp.concatenate([wr_hi, wr_lo], axis=1), br


def _router_specs(n, d, tm):
    return ([_full((d, 2 * LANES)), _full((1, LANES))],
            [pl.BlockSpec((tm, LANES), lambda i: (i, 0)), _full((SUBLANES, LANES))],
            [jax.ShapeDtypeStruct((n, LANES), F32), jax.ShapeDtypeStruct((SUBLANES, LANES), F32)],
            [pltpu.VMEM((SUBLANES, LANES), F32)])


def _even_layer(x, w_in, conv_w, sgu_ln_g, sgu_ln_b, sgu_w, sgu_b, w_out, ln_g, ln_b, w_router, b_router,
                *, seq, alpha):
    n, d = x.shape
    a_dim = conv_w.shape[1]
    tm = min(ROW_TILE, seq)
    n_groups = sgu_w.shape[0]
    pos = np.arange(SGU_BLOCK)
    mask = (pos[None, :] // CHUNK) <= (pos[:, None] // CHUNK)
    w_s = jnp.where(mask, sgu_w, 0.0)
    ws_pairs = jnp.concatenate([w_s[0::2], w_s[1::2]], axis=2).astype(BF16)
    sb_full = jnp.repeat(sgu_b.T, GROUP_DIM, axis=1)
    convw = jnp.zeros((SUBLANES, a_dim), F32).at[0:CONV_WIDTH].set(conv_w)
    g_idx = np.arange(LANES) // GROUP_DIM
    gavg = jnp.asarray((g_idx[:, None] == g_idx[None, :]).astype(np.float32) / GROUP_DIM, BF16)
    assert n_groups * GROUP_DIM == a_dim and w_in.shape[1] == 5 * a_dim
    kern = functools.partial(_even_kernel, tm=tm, seq=seq, alpha=alpha, a_dim=a_dim)
    r_in, r_out, r_shapes, r_scratch = _router_specs(n, d, tm)
    return pl.pallas_call(
        kern,
        grid=(n // tm,),
        in_specs=[pl.BlockSpec((tm, d), lambda i: (i, 0)),
                  _full(w_in.shape), _full(convw.shape), _full(gavg.shape),
                  _full((1, a_dim)), _full((1, a_dim)), _full(ws_pairs.shape), _full(sb_full.shape),
                  _full(w_out.shape), _full((1, d)), _full((1, d))] + r_in,
        out_specs=[pl.BlockSpec((tm, d), lambda i: (i, 0))] + r_out,
        out_shape=[jax.ShapeDtypeStruct((n, d), F32)] + r_shapes,
        scratch_shapes=[pltpu.VMEM((tm + SUBLANES, a_dim), F32), pltpu.VMEM((tm, 2 * a_dim), BF16)] + r_scratch,
        compiler_params=_cparams(),
        name="even_mixer",
    )(x, w_in.astype(BF16), convw, gavg, sgu_ln_g.reshape(1, a_dim), sgu_ln_b.reshape(1, a_dim),
      ws_pairs, sb_full, w_out.astype(BF16), ln_g.reshape(1, d), ln_b.reshape(1, d),
      *_router_operands(w_router, b_router))


MISC_KR = 0
MISC_KR_ROT = MLA_ROPE
MISC_FZ = 2 * MLA_ROPE


def _odd_proj_kernel(x_ref, win_ref, qg_ref, kvg_ref, wq_ref, wqr_ref, wk_ref, wkr_ref, wv_ref,
                     fb_ref, ctab_ref, stab_ref, ttab_ref,
                     q_ref, k_ref, v_ref, fq_ref, fk_ref, fv_ref, fcol_ref, frow_ref, carry_ref,
                     *, tm, seq, q_rank, kv_rank, fox_dim, q_scale, fq_scale):
    i = pl.program_id(0)
    x = x_ref[...]
    proj = _dot(x.astype(BF16), win_ref[...])
    o = 0
    c_q = proj[:, o:o + q_rank]; o += q_rank
    c_kv = proj[:, o:o + kv_rank]; o += kv_rank
    misc = proj[:, o:o + LANES]; o += LANES
    fq_ref[...] = (proj[:, o:o + fox_dim] * fq_scale).astype(BF16); o += fox_dim
    fk_ref[...] = proj[:, o:o + fox_dim].astype(BF16); o += fox_dim
    fv_ref[...] = proj[:, o:o + fox_dim].astype(BF16)

    cqn = (c_q * lax.rsqrt(jnp.mean(c_q * c_q, axis=-1, keepdims=True) + NORM_EPS) * qg_ref[...]).astype(BF16)
    ckn = (c_kv * lax.rsqrt(jnp.mean(c_kv * c_kv, axis=-1, keepdims=True) + NORM_EPS) * kvg_ref[...]).astype(BF16)

    q1 = _dot(cqn, wq_ref[...])
    q2 = _dot(cqn, wqr_ref[...])
    ctab = ctab_ref[...]
    stab = stab_ref[...]
    for hh in range(MLA_HEADS):
        sl = slice(hh * LANES, (hh + 1) * LANES)
        q_ref[:, sl] = ((q1[:, sl] * ctab + q2[:, sl] * stab) * q_scale).astype(BF16)

    km_hi, km_lo = _split2(misc * ttab_ref[...])
    k_ref[...] = (_dot(ckn, wk_ref[...]) + _dot(km_hi, wkr_ref[...]) + _dot(km_lo, wkr_ref[...])).astype(BF16)
    v_ref[...] = _dot(ckn, wv_ref[...]).astype(BF16)

    lane = lax.broadcasted_iota(I32, (tm, LANES), 1)
    z = misc + fb_ref[...]
    lf = -(jnp.maximum(-z, 0.0) + jnp.log1p(jnp.exp(-jnp.abs(z))))
    lf = jnp.where((lane >= MISC_FZ) & (lane < MISC_FZ + FOX_HEADS), lf, 0.0)
    r = lax.broadcasted_iota(I32, (tm, tm), 0)
    c = lax.broadcasted_iota(I32, (tm, tm), 1)
    tri = jnp.where(c <= r, 1.0, 0.0).astype(BF16)
    hi, mid, lo = _split3(lf)
    incl = _dot(tri, hi) + _dot(tri, mid) + _dot(tri, lo)

    @pl.when((i * tm) % seq == 0)
    def _():
        carry_ref[...] = jnp.zeros((SUBLANES, LANES), F32)

    fcum = incl + carry_ref[0:1, :]
    carry_ref[...] = jnp.broadcast_to(fcum[tm - 1:tm, :], (SUBLANES, LANES))
    fsc = fcum * LOG2E
    fcol_ref[...] = fsc
    frow_ref[...] = fsc.T[MISC_FZ:MISC_FZ + FOX_HEADS, :]


def _rope_tables(seq):
    half = MLA_ROPE // 2
    inv_freq = ROPE_THETA ** (-np.arange(0, MLA_ROPE, 2, dtype=np.float64) / MLA_ROPE)
    ang = np.arange(seq, dtype=np.float64)[:, None] * inv_freq[None, :]
    cos = np.concatenate([np.cos(ang), np.cos(ang)], axis=1)
    sin = np.concatenate([np.sin(ang), np.sin(ang)], axis=1)
    assert cos.shape[1] == 2 * half
    ctab = np.zeros((seq, LANES)); ctab[:, :MLA_NOPE] = 1.0; ctab[:, MLA_NOPE:MLA_NOPE + MLA_ROPE] = cos
    stab = np.zeros((seq, LANES)); stab[:, MLA_NOPE:MLA_NOPE + MLA_ROPE] = sin
    ttab = np.zeros((seq, LANES)); ttab[:, MISC_KR:MISC_KR + MLA_ROPE] = cos
    ttab[:, MISC_KR_ROT:MISC_KR_ROT + MLA_ROPE] = sin
    return (jnp.asarray(ctab, F32), jnp.asarray(stab, F32), jnp.asarray(ttab, F32))


def _rot_cols(w):
    half = MLA_ROPE // 2
    return jnp.concatenate([-w[..., half:], w[..., :half]], axis=-1)


def _odd_proj(x, w_in, q_norm_g, kv_norm_g, w_uq, w_ukv, f_bias, *, batch, seq):
    n, d = x.shape
    q_rank, kv_rank = q_norm_g.shape[0], kv_norm_g.shape[0]
    fox_dim = FOX_HEADS * FOX_HEAD_DIM
    tm = min(ROW_TILE, seq)
    o = 0
    w_cq = w_in[:, o:o + q_rank]; o += q_rank
    w_ckv = w_in[:, o:o + kv_rank]; o += kv_rank
    w_kr = w_in[:, o:o + MLA_ROPE]; o += MLA_ROPE
    w_f = w_in[:, o:o + 3 * fox_dim]; o += 3 * fox_dim
    w_fz = w_in[:, o:o + FOX_HEADS]
    w_misc = jnp.zeros((d, LANES), F32)
    w_misc = w_misc.at[:, MISC_KR:MISC_KR + MLA_ROPE].set(w_kr)
    w_misc = w_misc.at[:, MISC_KR_ROT:MISC_KR_ROT + MLA_ROPE].set(_rot_cols(w_kr))
    w_misc = w_misc.at[:, MISC_FZ:MISC_FZ + FOX_HEADS].set(w_fz)
    w_in_p = jnp.concatenate([w_cq, w_ckv, w_misc, w_f], axis=1).astype(BF16)
    dq = MLA_NOPE + MLA_ROPE
    wq3 = w_uq.reshape(q_rank, MLA_HEADS, dq)
    wq = jnp.zeros((q_rank, MLA_HEADS, LANES), F32).at[:, :, :dq].set(wq3)
    wqr = jnp.zeros((q_rank, MLA_HEADS, LANES), F32).at[:, :, MLA_NOPE:dq].set(_rot_cols(wq3[:, :, MLA_NOPE:]))
    wkv3 = w_ukv.reshape(kv_rank, MLA_HEADS, MLA_NOPE + MLA_V)
    wk = jnp.zeros((kv_rank, MLA_HEADS, LANES), F32).at[:, :, :MLA_NOPE].set(wkv3[:, :, :MLA_NOPE])
    wv = wkv3[:, :, MLA_NOPE:].reshape(kv_rank, MLA_HEADS * MLA_V)
    place = np.zeros((LANES, MLA_HEADS, LANES), np.float32)
    for j in range(MLA_ROPE):
        place[MISC_KR + j, :, MLA_NOPE + j] = 1.0
        place[MISC_KR_ROT + j, :, MLA_NOPE + j] = 1.0
    wkr = jnp.asarray(place.reshape(LANES, MLA_HEADS * LANES), BF16)
    fb = jnp.zeros((1, LANES), F32).at[0, MISC_FZ:MISC_FZ + FOX_HEADS].set(f_bias)
    ctab, stab, ttab = _rope_tables(seq)
    hl = MLA_HEADS * LANES
    nseq = seq // tm
    row = lambda i: (i, 0)
    tab = lambda i: (i % nseq, 0)
    kern = functools.partial(_odd_proj_kernel, tm=tm, seq=seq, q_rank=q_rank, kv_rank=kv_rank, fox_dim=fox_dim,
                             q_scale=LOG2E / math.sqrt(MLA_NOPE + MLA_ROPE), fq_scale=LOG2E / math.sqrt(FOX_HEAD_DIM))
    outs = pl.pallas_call(
        kern,
        grid=(n // tm,),
        in_specs=[pl.BlockSpec((tm, d), row), _full(w_in_p.shape), _full((1, q_rank)), _full((1, kv_rank)),
                  _full((q_rank, hl)), _full((q_rank, hl)), _full((kv_rank, hl)), _full((LANES, hl)),
                  _full((kv_rank, MLA_HEADS * MLA_V)), _full((1, LANES)),
                  pl.BlockSpec((tm, LANES), tab), pl.BlockSpec((tm, LANES), tab), pl.BlockSpec((tm, LANES), tab)],
        out_specs=[pl.BlockSpec((tm, hl), row), pl.BlockSpec((tm, hl), row),
                   pl.BlockSpec((tm, MLA_HEADS * MLA_V), row),
                   pl.BlockSpec((tm, fox_dim), row), pl.BlockSpec((tm, fox_dim), row), pl.BlockSpec((tm, fox_dim), row),
                   pl.BlockSpec((tm, LANES), row),
                   pl.BlockSpec((FOX_HEADS, tm), lambda i: (i // nseq, i % nseq))],
        out_shape=[jax.ShapeDtypeStruct((n, hl), BF16), jax.ShapeDtypeStruct((n, hl), BF16),
                   jax.ShapeDtypeStruct((n, MLA_HEADS * MLA_V), BF16),
                   jax.ShapeDtypeStruct((n, fox_dim), BF16), jax.ShapeDtypeStruct((n, fox_dim), BF16),
                   jax.ShapeDtypeStruct((n, fox_dim), BF16),
                   jax.ShapeDtypeStruct((n, LANES), F32),
                   jax.ShapeDtypeStruct((batch * FOX_HEADS, seq), F32)],
        scratch_shapes=[pltpu.VMEM((SUBLANES, LANES), F32)],
        compiler_params=_cparams(),
        name="odd_proj",
    )(x, w_in_p, q_norm_g.reshape(1, q_rank), kv_norm_g.reshape(1, kv_rank),
      wq.reshape(q_rank, hl).astype(BF16), wqr.reshape(q_rank, hl).astype(BF16),
      wk.reshape(kv_rank, hl).astype(BF16), wkr, wv.astype(BF16), fb, ctab, stab, ttab)
    return outs


def _attn_kernel(*refs, tq, fox, head_lanes):
    if fox:
        q_ref, k_ref, v_ref, fcol_ref, frow_ref, o_ref = refs
    else:
        q_ref, k_ref, v_ref, o_ref = refs
    hp = pl.program_id(1)
    qi = pl.program_id(2)
    lane = lax.broadcasted_iota(I32, (tq, LANES), 1)
    half = tq // 2
    qrel = lax.broadcasted_iota(I32, (half, 1), 0)
    krel = lax.broadcasted_iota(I32, (1, half), 1)
    if fox:
        allowed = krel <= qrel
    else:
        allowed = (krel // CHUNK) <= (qrel // CHUNK)
    qs, fqs = [], []
    for hh in range(2):
        if head_lanes == LANES:
            qs.append(q_ref[0, :, hh * LANES:(hh + 1) * LANES])
        else:
            in_head = (lane >= hh * head_lanes) & (lane < (hh + 1) * head_lanes)
            qs.append(jnp.where(in_head, q_ref[0], jnp.zeros((), BF16)))
        if fox:
            fqs.append(jnp.sum(jnp.where(lane == MISC_FZ + hp * 2 + hh, fcol_ref[0], 0.0),
                               axis=-1, keepdims=True))

    def update(carry, hh, r0, nr, start, nk, masked):
        m, l, acc = carry
        if head_lanes == LANES:
            kt = k_ref[0, pl.ds(start, nk), hh * LANES:(hh + 1) * LANES]
        else:
            kt = k_ref[0, pl.ds(start, nk), :]
        s = lax.dot_general(qs[hh][r0:r0 + nr], kt, (((1,), (1,)), ((), ())), preferred_element_type=F32)
        if fox:
            s = s + (fqs[hh][r0:r0 + nr] - frow_ref[pl.ds(hp * 2 + hh, 1), pl.ds(start, nk)])
        if masked:
            s = jnp.where(allowed, s, NEG_INF)
        m_new = jnp.maximum(m, jnp.max(s, axis=-1, keepdims=True))
        a = jnp.exp2(m - m_new)
        p = jnp.exp2(s - m_new)
        l = a * l + jnp.sum(p, axis=-1, keepdims=True)
        acc = a * acc + _dot(p.astype(BF16), v_ref[0, pl.ds(start, nk), :])
        return m_new, l, acc

    def full_step(j, carry):
        start = pl.multiple_of(j * tq, tq)
        return tuple(update(carry[hh], hh, 0, tq, start, tq, False) for hh in range(2))

    one = (jnp.full((tq, 1), NEG_INF, F32), jnp.zeros((tq, 1), F32), jnp.zeros((tq, LANES), F32))
    carry = lax.fori_loop(0, qi, full_step, (one, one))

    d0 = pl.multiple_of(qi * tq, tq)
    d1 = pl.multiple_of(qi * tq + half, half)
    outs = []
    for hh in range(2):
        top = tuple(t[:half] for t in carry[hh])
        bot = tuple(t[half:] for t in carry[hh])
        top = update(top, hh, 0, half, d0, half, True)
        bot = update(bot, hh, half, half, d0, half, False)
        bot = update(bot, hh, half, half, d1, half, True)
        outs.append(jnp.concatenate([top[2] / top[1], bot[2] / bot[1]], axis=0))
    o_ref[0] = jnp.where(lane < LANES // 2, outs[0], outs[1]).astype(o_ref.dtype)


def _attention(q, k, v, fcol, frow, *, batch, seq, fox, head_lanes):
    tq = min(ATTN_TILE, seq)
    n_heads = v.shape[2] // MLA_V
    qk_w = 2 * head_lanes
    in_specs = [pl.BlockSpec((1, tq, qk_w), lambda b, h, i: (b, i, h)),
                pl.BlockSpec((1, seq, qk_w), lambda b, h, i: (b, 0, h)),
                pl.BlockSpec((1, seq, LANES), lambda b, h, i: (b, 0, h))]
    args = [q, k, v]
    if fox:
        in_specs += [pl.BlockSpec((1, tq, LANES), lambda b, h, i: (b, i, 0)),
                     pl.BlockSpec((FOX_HEADS, seq), lambda b, h, i: (b, 0))]
        args += [fcol, frow]
    kern = functools.partial(_attn_kernel, tq=tq, fox=fox, head_lanes=head_lanes)
    return pl.pallas_call(
        kern,
        grid=(batch, n_heads // 2, seq // tq),
        in_specs=in_specs,
        out_specs=pl.BlockSpec((1, tq, LANES), lambda b, h, i: (b, i, h)),
        out_shape=jax.ShapeDtypeStruct((batch, seq, n_heads * MLA_V), BF16),
        compiler_params=_cparams(3),
        name="fox_attention" if fox else "mla_attention",
    )(*args)


def _outproj_kernel(x_ref, ya_ref, yb_ref, w_ref, g_ref, b_ref, whl_ref, br_ref, xo_ref, meta_ref, cnt_ref,
                    run_ref, *, alpha, tm):
    mix = jnp.concatenate([ya_ref[...], yb_ref[...]], axis=-1)
    out = _dot(mix, w_ref[...])
    xn = _layernorm(alpha * x_ref[...] + out, g_ref[...], b_ref[...])
    xo_ref[...] = xn
    _route(xn, whl_ref, br_ref, meta_ref, cnt_ref, run_ref, tm)


def _outproj_ln(x, ya, yb, w_out, ln_g, ln_b, w_router, b_router, *, alpha):
    n, d = x.shape
    tm = min(ROW_TILE, n)
    row = lambda i: (i, 0)
    r_in, r_out, r_shapes, r_scratch = _router_specs(n, d, tm)
    return pl.pallas_call(
        functools.partial(_outproj_kernel, alpha=alpha, tm=tm),
        grid=(n // tm,),
        in_specs=[pl.BlockSpec((tm, d), row), pl.BlockSpec((tm, ya.shape[1]), row),
                  pl.BlockSpec((tm, yb.shape[1]), row), _full(w_out.shape), _full((1, d)), _full((1, d))] + r_in,
        out_specs=[pl.BlockSpec((tm, d), row)] + r_out,
        out_shape=[jax.ShapeDtypeStruct((n, d), F32)] + r_shapes,
        scratch_shapes=r_scratch,
        compiler_params=_cparams(),
        name="odd_outproj",
    )(x, ya, yb, w_out.astype(BF16), ln_g.reshape(1, d), ln_b.reshape(1, d), *_router_operands(w_router, b_router))


def _odd_layer(x, w_in, q_norm_g, kv_norm_g, w_uq, w_ukv, f_bias, w_out, ln_g, ln_b, w_router, b_router,
               *, batch, seq, alpha):
    n, d = x.shape
    q, k, v, fq, fk, fv, fcol, frow = _odd_proj(x, w_in, q_norm_g, kv_norm_g, w_uq, w_ukv, f_bias,
                                                batch=batch, seq=seq)
    r3 = lambda t: t.reshape(batch, seq, t.shape[1])
    y_c = _attention(r3(q), r3(k), r3(v), None, None, batch=batch, seq=seq, fox=False, head_lanes=LANES)
    y_d = _attention(r3(fq), r3(fk), r3(fv), r3(fcol), frow, batch=batch, seq=seq, fox=True,
                     head_lanes=FOX_HEAD_DIM)
    return _outproj_ln(x, y_c.reshape(n, -1), y_d.reshape(n, -1), w_out, ln_g, ln_b, w_router, b_router,
                       alpha=alpha)


META_IDX = 0
META_RANK = TOP_K
META_GATE = 2 * TOP_K


def _route(x, whl_ref, b_ref, meta_ref, cnt_ref, run_ref, tm):
    @pl.when(pl.program_id(0) == 0)
    def _():
        run_ref[...] = jnp.zeros((SUBLANES, LANES), F32)

    xh, xl = _split2(x)
    both = _dot(xh, whl_ref[...])
    logits = both[:, :LANES] + _dot(xl, whl_ref[:, :LANES]) + both[:, LANES:] + b_ref[...]
    lane = lax.broadcasted_iota(I32, (tm, LANES), 1)
    work = logits
    vals, sels, idxs = [], [], []
    for _ in range(TOP_K):
        m = jnp.max(work, axis=-1, keepdims=True)
        idx = jnp.min(jnp.where(work == m, lane, LANES), axis=-1, keepdims=True)
        sel = lane == idx
        vals.append(m); sels.append(sel); idxs.append(idx)
        work = jnp.where(sel, -jnp.inf, work)
    es = [jnp.exp(vk - vals[0]) for vk in vals]
    den = es[0] + es[1] + es[2] + es[3]
    chosen = jnp.where(sels[0] | sels[1] | sels[2] | sels[3], 1.0, 0.0)
    r = lax.broadcasted_iota(I32, (tm, tm), 0)
    c = lax.broadcasted_iota(I32, (tm, tm), 1)
    tri = jnp.where(c < r, 1.0, 0.0).astype(BF16)
    before = _dot(tri, chosen.astype(BF16)) + run_ref[0:1, :]
    meta = jnp.zeros((tm, LANES), F32)
    for kk in range(TOP_K):
        rank = jnp.sum(jnp.where(sels[kk], before, 0.0), axis=-1, keepdims=True)
        meta = jnp.where(lane == META_IDX + kk, idxs[kk].astype(F32), meta)
        meta = jnp.where(lane == META_RANK + kk, rank, meta)
        meta = jnp.where(lane == META_GATE + kk, es[kk] / den, meta)
    meta_ref[...] = meta
    run = run_ref[0:1, :] + jnp.sum(chosen, axis=0, keepdims=True)
    run_ref[...] = jnp.broadcast_to(run, (SUBLANES, LANES))
    cnt_ref[...] = jnp.broadcast_to(run, (SUBLANES, LANES))


def _plan_kernel(cnt_ref, offs_ref, tile_ref, last_ref, *, te, max_tiles):
    lane = lax.broadcasted_iota(I32, (SUBLANES, LANES), 1)
    cnt = cnt_ref[...]
    padded = jnp.floor((cnt + (te - 1)) / te) * te
    incl = padded
    for s in (1, 2, 4, 8, 16):
        incl = incl + jnp.where(lane >= s, pltpu.roll(incl, s, axis=1), 0.0)
    offs_ref[...] = incl - padded
    last_ref[...] = jnp.where(cnt > 0.0, (incl - te) * ROW_SUBTILES, -1.0).astype(I32)
    ends = incl[0:1, :]
    lane_t = lax.broadcasted_iota(I32, (max_tiles, LANES), 1)
    start = (lax.broadcasted_iota(I32, (max_tiles, LANES), 0) * te).astype(F32)
    done = jnp.where((ends <= start) & (lane_t < N_EXPERTS), 1.0, 0.0)
    expert = jnp.minimum(jnp.sum(done, axis=-1, keepdims=True), N_EXPERTS - 1.0)
    total = jnp.sum(jnp.where(lane_t == N_EXPERTS - 1, ends, 0.0), axis=-1, keepdims=True) / te
    tile_ref[...] = jnp.where(lane_t == 0, expert, jnp.where(lane_t == 1, total, 0.0)).astype(I32)


def _dest_kernel(meta_ref, offs_ref, dest_ref, *, tm):
    lane = lax.broadcasted_iota(I32, (tm, LANES), 1)
    meta = meta_ref[...]
    offs = offs_ref[0:1, :]
    dest = jnp.zeros((tm, LANES), F32)
    for kk in range(TOP_K):
        idx = jnp.sum(jnp.where(lane == META_IDX + kk, meta, 0.0), axis=-1, keepdims=True).astype(I32)
        rank = jnp.sum(jnp.where(lane == META_RANK + kk, meta, 0.0), axis=-1, keepdims=True)
        base = jnp.sum(jnp.where(lane == idx, offs, 0.0), axis=-1, keepdims=True)
        dest = jnp.where(lane == kk, (base + rank) * ROW_SUBTILES, dest)
    dest_ref[...] = dest.astype(I32)


def _to_token_tiles(ref, x, n_rows):
    for c in range(ROW_SUBTILES):
        ref[pl.ds(c, n_rows, stride=ROW_SUBTILES), :] = x[:, c * LANES:(c + 1) * LANES]


def _from_token_tiles(ref, n_rows, lead=(), first=0):
    return jnp.concatenate([ref[lead + (pl.ds(first * ROW_SUBTILES + c, n_rows, stride=ROW_SUBTILES), slice(None))]
                            for c in range(ROW_SUBTILES)], axis=-1)


def _idx_chunk_copy(dest_hbm, idx_smem, isem, step, n_rows):
    slot = step % 2
    off = pl.multiple_of(slot * n_rows, n_rows)
    return pltpu.make_async_copy(dest_hbm.at[pl.ds(step * n_rows, n_rows)], idx_smem.at[pl.ds(off, n_rows)],
                                 isem.at[slot])


def _prefetch_idx(dest_hbm, idx_smem, isem, i, n_steps, n_rows):
    @pl.when(i == 0)
    def _():
        _idx_chunk_copy(dest_hbm, idx_smem, isem, 0, n_rows).start()

    _idx_chunk_copy(dest_hbm, idx_smem, isem, i, n_rows).wait()

    @pl.when(i + 1 < n_steps)
    def _():
        _idx_chunk_copy(dest_hbm, idx_smem, isem, i + 1, n_rows).start()

    return (i % 2) * n_rows


def _dispatch_kernel(last_ref, nt_ref, dest_hbm, x_ref, xs_hbm, idx_smem, xt_ref, zero_ref, isem, sem, zsem,
                     *, tm, te, n_steps, max_tiles):
    i = pl.program_id(0)
    n_rows = tm * TOP_K
    tile_rows = te * ROW_SUBTILES

    @pl.when(i == 0)
    def _():
        zero_ref[...] = jnp.zeros(zero_ref.shape, F32)

        def zero_copy(row):
            return pltpu.make_async_copy(zero_ref, xs_hbm.at[pl.ds(pl.multiple_of(row, ROW_SUBTILES), tile_rows), :],
                                         zsem)

        for e in range(N_EXPERTS):
            @pl.when(last_ref[e] >= 0)
            def _():
                zero_copy(last_ref[e]).start()

        def tail_start(j, carry):
            zero_copy(j * tile_rows).start()
            return carry

        def tail_wait(j, carry):
            zero_copy(j * tile_rows).wait()
            return carry

        lax.fori_loop(nt_ref[0], max_tiles, tail_start, 0)
        for e in range(N_EXPERTS):
            @pl.when(last_ref[e] >= 0)
            def _():
                zero_copy(last_ref[e]).wait()
        lax.fori_loop(nt_ref[0], max_tiles, tail_wait, 0)

    _to_token_tiles(xt_ref, x_ref[...], tm)
    base = _prefetch_idx(dest_hbm, idx_smem, isem, i, n_steps, n_rows)

    def row_copy(r, kk):
        d = pl.multiple_of(idx_smem[base + r * TOP_K + kk], ROW_SUBTILES)
        src = pl.multiple_of(r * ROW_SUBTILES, ROW_SUBTILES)
        return pltpu.make_async_copy(xt_ref.at[pl.ds(src, ROW_SUBTILES), :],
                                     xs_hbm.at[pl.ds(d, ROW_SUBTILES), :], sem)

    def issue(r, carry):
        for kk in range(TOP_K):
            row_copy(r, kk).start(priority=kk % 2)
        return carry

    lax.fori_loop(0, tm, issue, 0, unroll=ISSUE_UNROLL)

    for _ in range(TOP_K):
        pltpu.make_async_copy(xt_ref, xs_hbm.at[pl.ds(0, tm * ROW_SUBTILES), :], sem).wait()


def _expert_kernel(te_ref, nt_ref, xs_ref, w1_ref, b1_ref, w2_ref, b2_ref, perm_ref, ys_ref, w1b_ref, w2b_ref,
                   *, te, d_ff):
    j = pl.program_id(0)
    e = te_ref[j]
    n_tiles = nt_ref[0]
    prev = te_ref[jnp.maximum(j - 1, 0)]

    @pl.when((j == 0) | (e != prev))
    def _():
        perm = perm_ref[...]
        for cb in range(2 * d_ff // (2 * LANES)):
            sl = slice(cb * 2 * LANES, (cb + 1) * 2 * LANES)
            w1b_ref[:, sl] = _dot(w1_ref[0, 0, :, sl].astype(BF16), perm).astype(BF16)
        w2b_ref[...] = w2_ref[0, 0].astype(BF16)

    @pl.when(j < n_tiles)
    def _():
        xb = _from_token_tiles(xs_ref, te).astype(BF16)
        h = _dot(xb, w1b_ref[...]) + b1_ref[0, 0]
        acts = []
        for cb in range(d_ff // LANES):
            glu = jnp.minimum(h[:, cb * 2 * LANES:cb * 2 * LANES + LANES], SWIGLU_LIMIT)
            lin = jnp.clip(h[:, cb * 2 * LANES + LANES:(cb + 1) * 2 * LANES], -SWIGLU_LIMIT, SWIGLU_LIMIT)
            sig = 1.0 / (1.0 + jnp.exp(-SWIGLU_ALPHA * glu))
            acts.append((glu * sig * (lin + 1.0)).astype(BF16))
        act = jnp.concatenate(acts, axis=-1)
        _to_token_tiles(ys_ref, _dot(act, w2b_ref[...]) + b2_ref[0, 0], te)

    @pl.when(j >= n_tiles)
    def _():
        ys_ref[...] = jnp.zeros(ys_ref.shape, F32)


def _combine_kernel(dest_hbm, meta_ref, x_ref, g_ref, b_ref, ys_hbm, xo_ref, idx_smem, buf_ref, isem, sem,
                    *, tm, alpha, n_steps):
    i = pl.program_id(0)
    n_rows = tm * TOP_K
    hm = tm // 2
    base = _prefetch_idx(dest_hbm, idx_smem, isem, i, n_steps, n_rows)

    def row_copy(r, kk, part):
        d = pl.multiple_of(idx_smem[base + r * TOP_K + kk], ROW_SUBTILES)
        dst = pl.multiple_of(r * ROW_SUBTILES, ROW_SUBTILES)
        return pltpu.make_async_copy(ys_hbm.at[pl.ds(d, ROW_SUBTILES), :],
                                     buf_ref.at[kk, pl.ds(dst, ROW_SUBTILES), :], sem.at[part])

    for part in range(2):
        def issue(r, carry, part=part):
            for kk in range(TOP_K):
                row_copy(r, kk, part).start(priority=kk % 2)
            return carry
        lax.fori_loop(part * hm, (part + 1) * hm, issue, 0, unroll=ISSUE_UNROLL)

    lane = lax.broadcasted_iota(I32, (hm, LANES), 1)
    for part in range(2):
        for kk in range(TOP_K):
            pltpu.make_async_copy(ys_hbm.at[pl.ds(0, hm * ROW_SUBTILES), :],
                                  buf_ref.at[kk, pl.ds(part * hm * ROW_SUBTILES, hm * ROW_SUBTILES), :],
                                  sem.at[part]).wait()

        rows = slice(part * hm, (part + 1) * hm)
        meta = meta_ref[rows, :]
        ffn = jnp.zeros((hm, x_ref.shape[1]), F32)
        for kk in range(TOP_K):
            gate = jnp.sum(jnp.where(lane == META_GATE + kk, meta, 0.0), axis=-1, keepdims=True)
            ffn = ffn + gate * _from_token_tiles(buf_ref, hm, lead=(kk,), first=part * hm)
        xo_ref[rows, :] = _layernorm(alpha * x_ref[rows, :] + ffn, g_ref[...], b_ref[...])


def _deinterleave_perm():
    p = np.zeros((2 * LANES, 2 * LANES), np.float32)
    j = np.arange(LANES)
    p[2 * j, j] = 1.0
    p[2 * j + 1, LANES + j] = 1.0
    return jnp.asarray(p, BF16)


def _regroup_expert_biases(b1, b2):
    n_l, n_exp, two_ff = b1.shape
    b1_r = b1.reshape(n_l, n_exp, two_ff // (2 * LANES), LANES, 2).transpose(0, 1, 2, 4, 3)
    return b1_r.reshape(n_l, n_exp, 1, two_ff), b2.reshape(n_l, n_exp, 1, b2.shape[-1])


def _moe_layer(x, meta, counts, w1, b1_r, w2, b2, ln_g, ln_b, *, layer, alpha):
    n, d = x.shape
    assert d == ROW_SUBTILES * LANES
    _, n_exp, _, two_ff = w1.shape
    d_ff = two_ff // 2
    te = EXPERT_TILE
    max_tiles = (n * TOP_K) // te + n_exp
    max_tiles_p = -(-max_tiles // SUBLANES) * SUBLANES
    tm = min(ROW_TILE, n)
    row = lambda i: (i, 0)

    offs, tiles, last_rows = pl.pallas_call(
        functools.partial(_plan_kernel, te=te, max_tiles=max_tiles_p),
        out_shape=[jax.ShapeDtypeStruct((SUBLANES, LANES), F32), jax.ShapeDtypeStruct((max_tiles_p, LANES), I32),
                   jax.ShapeDtypeStruct((SUBLANES, LANES), I32)],
        name="moe_plan",
    )(counts)

    dest = pl.pallas_call(
        functools.partial(_dest_kernel, tm=tm),
        grid=(n // tm,),
        in_specs=[pl.BlockSpec((tm, LANES), row), _full((SUBLANES, LANES))],
        out_specs=pl.BlockSpec((tm, LANES), row),
        out_shape=jax.ShapeDtypeStruct((n, LANES), I32),
        compiler_params=_cparams(),
        name="moe_dest",
    )(meta, offs)
    dest_flat = dest[:, :TOP_K].reshape(n * TOP_K)
    tile_expert = tiles[:, 0]
    n_tiles = tiles[0:1, 1]

    tmv = min(MOVE_TILE, n)
    n_slots = max_tiles * te
    n_steps = n // tmv
    xs = pl.pallas_call(
        functools.partial(_dispatch_kernel, tm=tmv, te=te, n_steps=n_steps, max_tiles=max_tiles),
        grid_spec=pltpu.PrefetchScalarGridSpec(
            num_scalar_prefetch=2,
            grid=(n_steps,),
            in_specs=[pl.BlockSpec(memory_space=pl.ANY), pl.BlockSpec((tmv, d), lambda i, last, nt: (i, 0))],
            out_specs=pl.BlockSpec(memory_space=pl.ANY),
            scratch_shapes=[pltpu.SMEM((2 * tmv * TOP_K,), I32), pltpu.VMEM((tmv * ROW_SUBTILES, LANES), F32),
                            pltpu.VMEM((te * ROW_SUBTILES, LANES), F32),
                            pltpu.SemaphoreType.DMA((2,)), pltpu.SemaphoreType.DMA, pltpu.SemaphoreType.DMA]),
        out_shape=jax.ShapeDtypeStruct((n_slots * ROW_SUBTILES, LANES), F32),
        compiler_params=_cparams(),
        name="moe_dispatch",
    )(last_rows[0, :n_exp], n_tiles, dest_flat, x)

    ys = pl.pallas_call(
        functools.partial(_expert_kernel, te=te, d_ff=d_ff),
        grid_spec=pltpu.PrefetchScalarGridSpec(
            num_scalar_prefetch=2,
            grid=(max_tiles,),
            in_specs=[pl.BlockSpec((te * ROW_SUBTILES, LANES), lambda j, t, nt: (jnp.minimum(j, nt[0] - 1), 0)),
                      pl.BlockSpec((1, 1, d, two_ff), lambda j, t, nt: (layer, t[j], 0, 0)),
                      pl.BlockSpec((1, 1, 1, two_ff), lambda j, t, nt: (layer, t[j], 0, 0)),
                      pl.BlockSpec((1, 1, d_ff, d), lambda j, t, nt: (layer, t[j], 0, 0)),
                      pl.BlockSpec((1, 1, 1, d), lambda j, t, nt: (layer, t[j], 0, 0)),
                      pl.BlockSpec((2 * LANES, 2 * LANES), lambda j, t, nt: (0, 0))],
            out_specs=pl.BlockSpec((te * ROW_SUBTILES, LANES), lambda j, t, nt: (j, 0)),
            scratch_shapes=[pltpu.VMEM((d, two_ff), BF16), pltpu.VMEM((d_ff, d), BF16)]),
        out_shape=jax.ShapeDtypeStruct((n_slots * ROW_SUBTILES, LANES), F32),
        compiler_params=_cparams(),
        name="moe_experts",
    )(tile_expert, n_tiles, xs, w1, b1_r, w2, b2, _deinterleave_perm())

    return pl.pallas_call(
        functools.partial(_combine_kernel, tm=tmv, alpha=alpha, n_steps=n_steps),
        grid=(n_steps,),
        in_specs=[pl.BlockSpec(memory_space=pl.ANY), pl.BlockSpec((tmv, LANES), row), pl.BlockSpec((tmv, d), row),
                  _full((1, d)), _full((1, d)), pl.BlockSpec(memory_space=pl.ANY)],
        out_specs=pl.BlockSpec((tmv, d), row),
        out_shape=jax.ShapeDtypeStruct((n, d), F32),
        scratch_shapes=[pltpu.SMEM((2 * tmv * TOP_K,), I32), pltpu.VMEM((TOP_K, tmv * ROW_SUBTILES, LANES), F32),
                        pltpu.SemaphoreType.DMA((2,)), pltpu.SemaphoreType.DMA((2,))],
        compiler_params=_cparams(),
        name="moe_combine",
    )(dest_flat, meta, x, ln_g.reshape(1, d), ln_b.reshape(1, d), ys)


def kernel(x, ev_w_in, ev_conv_w, ev_sgu_ln_g, ev_sgu_ln_b, ev_sgu_w, ev_sgu_b, ev_w_out, od_w_in, od_q_norm_g, od_kv_norm_g, od_w_uq, od_w_ukv, od_f_bias, od_w_out, ln_mix_g, ln_mix_b, ln_ffn_g, ln_ffn_b, moe_w_router, moe_b_router, moe_w1, moe_b1, moe_w2, moe_b2):
    batch, seq, d = x.shape
    depth = ln_mix_g.shape[0]
    alpha = (2 * depth) ** 0.25
    xf = x.reshape(batch * seq, d)
    b1_r, b2_r = _regroup_expert_biases(moe_b1, moe_b2)
    for layer in range(depth):
        i = layer // 2
        if layer % 2 == 0:
            xf, meta, counts = _even_layer(
                xf, ev_w_in[i], ev_conv_w[i], ev_sgu_ln_g[i], ev_sgu_ln_b[i], ev_sgu_w[i], ev_sgu_b[i], ev_w_out[i],
                ln_mix_g[layer], ln_mix_b[layer], moe_w_router[layer], moe_b_router[layer], seq=seq, alpha=alpha)
        else:
            xf, meta, counts = _odd_layer(
                xf, od_w_in[i], od_q_norm_g[i], od_kv_norm_g[i], od_w_uq[i], od_w_ukv[i], od_f_bias[i], od_w_out[i],
                ln_mix_g[layer], ln_mix_b[layer], moe_w_router[layer], moe_b_router[layer],
                batch=batch, seq=seq, alpha=alpha)
        xf = _moe_layer(xf, meta, counts, moe_w1, b1_r, moe_w2, b2_r,
                        ln_ffn_g[layer], ln_ffn_b[layer], layer=layer, alpha=alpha)
    return xf.reshape(batch, seq, d)
```
